```python
import jax, jax.numpy as jnp
from jax import lax
import numpy as np

D_MODEL = 1024
BATCH = 16
SEQ = 2048
DEPTH = 1
DEC_BATCH = 16
DEC_SEQ = 4096
PAST_LEN = 128

GRID_W = 64
D_MIX = D_MODEL
D_ATTN = D_MIX // 2
D_CONV = D_MIX - D_ATTN
HEAD_DIM = 64
N_HEADS = D_ATTN // HEAD_DIM
NA_ROWS_MAX = 8
NA_COLS = 16
CONV_WIDTH = 31
N_GROUPS = 4
EXPERTS_PER_GROUP = 8
N_EXPERTS = N_GROUPS * EXPERTS_PER_GROUP
TOP_K = 2
D_EXPERT = D_MODEL // 2
MOE_BLOCK = 128
EPS = 1e-6
D_IN = 3 * D_ATTN + 2 * D_CONV

kernel_name = 'hymba_natten_conformer_hmoe_encoder'


def rms_norm(x, g):
    xf = x.astype(jnp.float32)
    y = xf * lax.rsqrt(jnp.mean(xf * xf, axis=-1, keepdims=True) + EPS)
    return (y * g.astype(jnp.float32)).astype(x.dtype)


def layer_norm(x, g, b):
    xf = x.astype(jnp.float32)
    mu = jnp.mean(xf, axis=-1, keepdims=True)
    xc = xf - mu
    var = jnp.mean(xc * xc, axis=-1, keepdims=True)
    y = xc * lax.rsqrt(var + EPS) * g.astype(jnp.float32) + b.astype(jnp.float32)
    return y.astype(x.dtype)


def neighbourhood_attention(q, k, v, rpb):
    B, L, H, Dh = q.shape
    rows = L // GRID_W
    kr = min(NA_ROWS_MAX, rows)
    scale = HEAD_DIM ** -0.5
    qg = q.reshape(B, rows, GRID_W, H, Dh)
    kg = k.reshape(B, rows, GRID_W, H, Dh)
    vg = v.reshape(B, rows, GRID_W, H, Dh)
    col = jnp.arange(GRID_W)
    col_start = jnp.clip(col - NA_COLS // 2, 0, GRID_W - NA_COLS)
    col_idx = col_start[:, None] + jnp.arange(NA_COLS)[None, :]
    col_off = col_idx - col[:, None] + (NA_COLS - 1)
    rpb_cols = rpb[:, :, col_off]

    def row_block(args):
        q_r, r = args
        r_start = jnp.clip(r - kr // 2, 0, rows - kr)
        k_rows = lax.dynamic_slice_in_dim(kg, r_start, kr, axis=1)
        v_rows = lax.dynamic_slice_in_dim(vg, r_start, kr, axis=1)
        k_win = k_rows[:, :, col_idx]
        v_win = v_rows[:, :, col_idx]
        s = jnp.einsum('bqhd,brqjhd->bhqrj', q_r, k_win).astype(jnp.float32) * scale
        row_off = r_start + jnp.arange(kr) - r + (NA_ROWS_MAX - 1)
        bias = jnp.transpose(rpb_cols[:, row_off], (0, 2, 1, 3))
        s = s + bias[None].astype(jnp.float32)
        p = jax.nn.softmax(s.reshape(B, H, GRID_W, kr * NA_COLS), axis=-1)
        p = p.reshape(B, H, GRID_W, kr, NA_COLS).astype(v.dtype)
        return jnp.einsum('bhqrj,brqjhd->bqhd', p, v_win)

    out = lax.map(row_block, (jnp.moveaxis(qg, 1, 0), jnp.arange(rows)))
    return jnp.moveaxis(out, 0, 1).reshape(B, L, H * Dh)


def conformer_conv(u, w_dw, b_dw, ln_g, ln_b):
    a, gt = jnp.split(u, 2, axis=-1)
    z = a * jax.nn.sigmoid(gt)
    z = lax.conv_general_dilated(
        z, w_dw[:, None, :], window_strides=(1,),
        padding=[(CONV_WIDTH // 2, CONV_WIDTH // 2)],
        dimension_numbers=('NWC', 'WIO', 'NWC'),
        feature_group_count=D_CONV) + b_dw
    z = layer_norm(z, ln_g, ln_b)
    return jax.nn.silu(z)


def hier_moe(h, w_group, b_group, w_expert, b_expert, w_gate, w_up, w_down):
    T, D = h.shape
    hf = h.astype(jnp.float32)
    g_logits = hf @ w_group.astype(jnp.float32) + b_group.astype(jnp.float32)
    g_prob = jax.nn.softmax(g_logits, axis=-1)
    g_sel = jnp.argmax(g_logits, axis=-1)
    p_g = jnp.take_along_axis(g_prob, g_sel[:, None], axis=1)[:, 0]
    e_all = jnp.einsum('td,gde->tge', hf, w_expert.astype(jnp.float32)) + b_expert.astype(jnp.float32)
    e_logits = jnp.take_along_axis(e_all, g_sel[:, None, None], axis=1)[:, 0]
    top_v, top_j = lax.top_k(e_logits, TOP_K)
    gates = p_g[:, None] * jax.nn.softmax(top_v, axis=-1)
    expert_idx = (g_sel[:, None] * EXPERTS_PER_GROUP + top_j).astype(jnp.int32)

    A = T * TOP_K
    flat_e = expert_idx.reshape(A)
    flat_tok = jnp.arange(A, dtype=jnp.int32) // TOP_K
    order = jnp.argsort(flat_e)
    sorted_e = flat_e[order]
    counts = jnp.bincount(flat_e, length=N_EXPERTS).astype(jnp.int32)
    padded = ((counts + MOE_BLOCK - 1) // MOE_BLOCK) * MOE_BLOCK
    seg_start = jnp.cumsum(counts) - counts
    pad_end = jnp.cumsum(padded)
    pad_start = pad_end - padded
    rank = jnp.arange(A, dtype=jnp.int32) - seg_start[sorted_e]
    dest = pad_start[sorted_e] + rank
    n_blocks = A // MOE_BLOCK + N_EXPERTS
    P = n_blocks * MOE_BLOCK
    buf_tok = jnp.full((P,), T, dtype=jnp.int32).at[dest].set(flat_tok[order])
    h_pad = jnp.concatenate([h, jnp.zeros((1, D), h.dtype)], axis=0)
    xb = h_pad[buf_tok].reshape(n_blocks, MOE_BLOCK, D)
    block_start = jnp.arange(n_blocks, dtype=jnp.int32) * MOE_BLOCK
    block_e = jnp.clip(jnp.searchsorted(pad_end, block_start, side='right'), 0, N_EXPERTS - 1)

    def expert_block(args):
        x_b, e = args
        hid = jax.nn.silu(x_b @ w_gate[e]) * (x_b @ w_up[e])
        return hid @ w_down[e]

    yb = lax.map(expert_block, (xb, block_e)).reshape(P, D)
    dest_orig = jnp.zeros((A,), jnp.int32).at[order].set(dest)
    y_assign = yb[dest_orig].reshape(T, TOP_K, D)
    return jnp.einsum('tkd,tk->td', y_assign, gates.astype(h.dtype))


def encoder_trunk(x, norm1_g, w_in, rpb, attn_out_g, conv_w, conv_b, conv_ln_g, conv_ln_b,
                  conv_out_g, w_out, norm2_g, w_group, b_group, w_expert, b_expert,
                  w_e_gate, w_e_up, w_e_down, final_g):
    B, L, _ = x.shape
    for l in range(DEPTH):
        h = rms_norm(x, norm1_g[l])
        u = h @ w_in[l]
        q, k, v, uc = jnp.split(u, [D_ATTN, 2 * D_ATTN, 3 * D_ATTN], axis=-1)
        q = q.reshape(B, L, N_HEADS, HEAD_DIM)
        k = k.reshape(B, L, N_HEADS, HEAD_DIM)
        v = v.reshape(B, L, N_HEADS, HEAD_DIM)
        a = neighbourhood_attention(q, k, v, rpb[l])
        c = conformer_conv(uc, conv_w[l], conv_b[l], conv_ln_g[l], conv_ln_b[l])
        m = jnp.concatenate([rms_norm(a, attn_out_g[l]), rms_norm(c, conv_out_g[l])], axis=-1)
        x = x + m @ w_out[l]
        h = rms_norm(x, norm2_g[l]).reshape(B * L, D_MODEL)
        x = x + hier_moe(h, w_group[l], b_group[l], w_expert[l], b_expert[l],
                         w_e_gate[l], w_e_up[l], w_e_down[l]).reshape(B, L, D_MODEL)
    return rms_norm(x, final_g)


def setup_inputs(seed: int = 0) -> dict:
    key = jax.random.key(seed)
    ks = jax.random.split(key, 24)
    f32 = jnp.float32
    nrm = lambda k, s, sc: jax.random.normal(k, s, f32) * sc
    return {
        'x_prompt': nrm(ks[0], (BATCH, SEQ, D_MODEL), 1.0),
        'x_sample': nrm(ks[1], (DEC_BATCH, DEC_SEQ, D_MODEL), 1.0),
        'norm1_g': 1.0 + nrm(ks[2], (DEPTH, D_MODEL), 0.02),
        'w_in': nrm(ks[3], (DEPTH, D_MODEL, D_IN), D_MODEL ** -0.5),
        'rpb': nrm(ks[4], (DEPTH, N_HEADS, 2 * NA_ROWS_MAX - 1, 2 * NA_COLS - 1), 0.1),
        'attn_out_g': 1.0 + nrm(ks[5], (DEPTH, D_ATTN), 0.02),
        'conv_w': nrm(ks[6], (DEPTH, CONV_WIDTH, D_CONV), CONV_WIDTH ** -0.5),
        'conv_b': nrm(ks[7], (DEPTH, D_CONV), 0.02),
        'conv_ln_g': 1.0 + nrm(ks[8], (DEPTH, D_CONV), 0.02),
        'conv_ln_b': nrm(ks[9], (DEPTH, D_CONV), 0.02),
        'conv_out_g': 1.0 + nrm(ks[10], (DEPTH, D_CONV), 0.02),
        'w_out': nrm(ks[11], (DEPTH, D_MIX, D_MODEL), D_MIX ** -0.5),
        'norm2_g': 1.0 + nrm(ks[12], (DEPTH, D_MODEL), 0.02),
        'w_group': nrm(ks[13], (DEPTH, D_MODEL, N_GROUPS), D_MODEL ** -0.5),
        'b_group': nrm(ks[14], (DEPTH, N_GROUPS), 0.01),
        'w_expert': nrm(ks[15], (DEPTH, N_GROUPS, D_MODEL, EXPERTS_PER_GROUP), D_MODEL ** -0.5),
        'b_expert': nrm(ks[16], (DEPTH, N_GROUPS, EXPERTS_PER_GROUP), 0.01),
        'w_e_gate': nrm(ks[17], (DEPTH, N_EXPERTS, D_MODEL, D_EXPERT), D_MODEL ** -0.5),
        'w_e_up': nrm(ks[18], (DEPTH, N_EXPERTS, D_MODEL, D_EXPERT), D_MODEL ** -0.5),
        'w_e_down': nrm(ks[19], (DEPTH, N_EXPERTS, D_EXPERT, D_MODEL), D_EXPERT ** -0.5),
        'final_g': 1.0 + nrm(ks[20], (D_MODEL,), 0.02),
    }


def reference(x_prompt, x_sample, norm1_g, w_in, rpb, attn_out_g, conv_w, conv_b, conv_ln_g,
              conv_ln_b, conv_out_g, w_out, norm2_g, w_group, b_group, w_expert, b_expert,
              w_e_gate, w_e_up, w_e_down, final_g):
    y_prompt = encoder_trunk(x_prompt, norm1_g, w_in, rpb, attn_out_g, conv_w, conv_b, conv_ln_g,
                             conv_ln_b, conv_out_g, w_out, norm2_g, w_group, b_group, w_expert,
                             b_expert, w_e_gate, w_e_up, w_e_down, final_g)
    y_sample = encoder_trunk(x_sample, norm1_g, w_in, rpb, attn_out_g, conv_w, conv_b, conv_ln_g,
                             conv_ln_b, conv_out_g, w_out, norm2_g, w_group, b_group, w_expert,
                             b_expert, w_e_gate, w_e_up, w_e_down, final_g)
    return (y_prompt, y_sample)
```

```python
import functools

import jax
import jax.numpy as jnp
from jax import lax
from jax.experimental import pallas as pl
from jax.experimental.pallas import tpu as pltpu

F32 = jnp.float32
BF16 = jnp.bfloat16
I32 = jnp.int32

D_MODEL = 1024
GRID_W = 64
D_ATTN = 512
D_CONV = 512
HEAD_DIM = 64
N_HEADS = 8
NA_ROWS = 8
NA_COLS = 16
CONV_WIDTH = 31
N_GROUPS = 4
EPG = 8
N_EXPERTS = 32
D_EXPERT = 512
EPS = 1e-6

LANES = 128
HEAD_PAIRS = N_HEADS * HEAD_DIM // LANES
NA_KEYS = NA_ROWS * GRID_W
MASKED = -1e30
TOKEN_TILE = 512
EXPERT_BLOCK = 512
CONV_TILE = 64
CONV_HALO = 16
VMEM_LIMIT = 56 * 1024 * 1024


def _rms(x, g):
    return x * lax.rsqrt(jnp.mean(x * x, axis=-1, keepdims=True) + EPS) * g


def _in_proj_body(x_ref, g_ref, wqkv_ref, wc_ref, qkv_ref, uc_ref):
    h = _rms(x_ref[...], g_ref[...]).astype(BF16)
    qkv_ref[...] = jnp.dot(h, wqkv_ref[...], preferred_element_type=F32).astype(BF16)
    uc_ref[...] = jnp.dot(h, wc_ref[...], preferred_element_type=F32).astype(BF16)


def _in_proj(x2, g, wqkv, wc):
    T = x2.shape[0]
    tm = TOKEN_TILE
    return pl.pallas_call(
        _in_proj_body,
        grid=(T // tm,),
        in_specs=[
            pl.BlockSpec((tm, D_MODEL), lambda i: (i, 0)),
            pl.BlockSpec((1, D_MODEL), lambda i: (0, 0)),
            pl.BlockSpec((D_MODEL, 3 * D_ATTN), lambda i: (0, 0)),
            pl.BlockSpec((D_MODEL, 2 * D_CONV), lambda i: (0, 0)),
        ],
        out_specs=[
            pl.BlockSpec((tm, 3 * D_ATTN), lambda i: (i, 0)),
            pl.BlockSpec((tm, 2 * D_CONV), lambda i: (i, 0)),
        ],
        out_shape=[
            jax.ShapeDtypeStruct((T, 3 * D_ATTN), BF16),
            jax.ShapeDtypeStruct((T, 2 * D_CONV), BF16),
        ],
        compiler_params=pltpu.CompilerParams(
            dimension_semantics=("arbitrary",), vmem_limit_bytes=VMEM_LIMIT),
        name="in_proj",
    )(x2, g, wqkv, wc)


def _na_bias_table(rpb):
    c = jnp.arange(GRID_W)
    col_start = jnp.clip(c - NA_COLS // 2, 0, GRID_W - NA_COLS)
    cp = jnp.arange(GRID_W)
    valid = (cp[None, :] >= col_start[:, None]) & (cp[None, :] < col_start[:, None] + NA_COLS)
    col_off = jnp.clip(cp[None, :] - c[:, None] + (NA_COLS - 1), 0, 2 * NA_COLS - 2)
    p = jnp.arange(NA_ROWS)
    i = jnp.arange(NA_ROWS)
    row_off = i[None, :] - p[:, None] + (NA_ROWS - 1)
    t = rpb[:, row_off[:, None, :, None], col_off[None, :, None, :]]
    t = jnp.where(valid[None, None, :, None, :], t, MASKED)
    t = t.reshape(HEAD_PAIRS, 2, NA_ROWS, GRID_W, NA_KEYS)
    return t.transpose(0, 2, 1, 3, 4).reshape(HEAD_PAIRS, NA_ROWS, LANES, NA_KEYS).astype(F32)


def _attn_body(q_ref, k_ref, v_ref, bias_ref, o_ref, *, rows, rows_per_step):
    rb = pl.program_id(2)
    first_head = lax.broadcasted_iota(I32, (GRID_W, LANES), 1) < HEAD_DIM
    zero = jnp.zeros((GRID_W, LANES), BF16)
    for rr in range(rows_per_step):
        r = rb * rows_per_step + rr
        r_start = jnp.clip(r - NA_ROWS // 2, 0, rows - NA_ROWS)
        p = r - r_start
        q = q_ref[rr * GRID_W:(rr + 1) * GRID_W, :] * jnp.asarray(HEAD_DIM ** -0.5, BF16)
        qbd = jnp.concatenate([jnp.where(first_head, q, zero), jnp.where(first_head, zero, q)], axis=0)
        k0 = pl.multiple_of(r_start * GRID_W, GRID_W)
        ks = k_ref[pl.ds(k0, NA_KEYS), :]
        vs = v_ref[pl.ds(k0, NA_KEYS), :]
        s = lax.dot_general(qbd, ks, (((1,), (1,)), ((), ())), preferred_element_type=F32)
        s = s + bias_ref[p]
        m = jnp.max(s, axis=-1, keepdims=True)
        e = jnp.exp(s - m)
        l = jnp.sum(e, axis=-1, keepdims=True)
        o = jnp.dot(e.astype(BF16), vs, preferred_element_type=F32) / l
        out = jnp.where(first_head, o[:GRID_W], o[GRID_W:])
        o_ref[rr * GRID_W:(rr + 1) * GRID_W, :] = out.astype(BF16)


def _attention(qkv, bias):
    B, L, _ = qkv.shape
    rows = L // GRID_W
    rows_per_step = 8
    tq = rows_per_step * GRID_W
    body = functools.partial(_attn_body, rows=rows, rows_per_step=rows_per_step)
    return pl.pallas_call(
        body,
        grid=(HEAD_PAIRS, B, L // tq),
        in_specs=[
            pl.BlockSpec((None, tq, LANES), lambda hp, b, rb: (b, rb, hp)),
            pl.BlockSpec((None, L, LANES), lambda hp, b, rb: (b, 0, HEAD_PAIRS + hp)),
            pl.BlockSpec((None, L, LANES), lambda hp, b, rb: (b, 0, 2 * HEAD_PAIRS + hp)),
            pl.BlockSpec((None, NA_ROWS, LANES, NA_KEYS), lambda hp, b, rb: (hp, 0, 0, 0)),
        ],
        out_specs=pl.BlockSpec((None, tq, LANES), lambda hp, b, rb: (b, rb, hp)),
        out_shape=jax.ShapeDtypeStruct((B, L, D_ATTN), BF16),
        compiler_params=pltpu.CompilerParams(
            dimension_semantics=("arbitrary", "arbitrary", "arbitrary"),
            vmem_limit_bytes=VMEM_LIMIT),
        name="na_attention",
    )(qkv, qkv, qkv, bias)


def _conv_body(uc_ref, w_ref, b_ref, lng_ref, lnb_ref, og_ref, o_ref, zpad_ref, cbuf_ref, *, L):
    glu_rows = 256
    zpad_ref[0:CONV_HALO, :] = jnp.zeros((CONV_HALO, D_CONV), F32)
    zpad_ref[L + CONV_HALO:L + 2 * CONV_HALO, :] = jnp.zeros((CONV_HALO, D_CONV), F32)

    def glu(i, carry):
        t0 = pl.multiple_of(i * glu_rows, glu_rows)
        a = uc_ref[pl.ds(t0, glu_rows), 0:D_CONV].astype(F32)
        gt = uc_ref[pl.ds(t0, glu_rows), D_CONV:2 * D_CONV].astype(F32)
        zpad_ref[pl.ds(t0 + CONV_HALO, glu_rows), :] = a * jax.nn.sigmoid(gt)
        return carry

    lax.fori_loop(0, L // glu_rows, glu, 0)

    tt = CONV_TILE
    first = CONV_HALO - CONV_WIDTH // 2
    span = tt + 24

    def tile(i, carry):
        t0 = pl.multiple_of(i * tt, tt)
        for j in range(D_CONV // LANES):
            lanes = slice(j * LANES, (j + 1) * LANES)
            chunk = zpad_ref[pl.ds(t0, tt + 2 * CONV_HALO), lanes]
            acc = jnp.broadcast_to(b_ref[:, lanes], (tt, LANES))
            for s in range(8):
                shifted = chunk[s:s + span]
                for a8 in range(4):
                    k = 8 * a8 + s - first
                    if 0 <= k < CONV_WIDTH:
                        acc = acc + w_ref[k:k + 1, lanes] * shifted[8 * a8:8 * a8 + tt]
            cbuf_ref[:, lanes] = acc
        y = cbuf_ref[...]
        mu = jnp.mean(y, axis=-1, keepdims=True)
        yc = y - mu
        var = jnp.mean(yc * yc, axis=-1, keepdims=True)
        yn = yc * lax.rsqrt(var + EPS) * lng_ref[...] + lnb_ref[...]
        sw = yn * jax.nn.sigmoid(yn)
        o_ref[pl.ds(t0, tt), :] = _rms(sw, og_ref[...]).astype(BF16)
        return carry

    lax.fori_loop(0, L // tt, tile, 0)


def _conv(uc, w, b, lng, lnb, og):
    B, L, _ = uc.shape
    vec = pl.BlockSpec((1, D_CONV), lambda i: (0, 0))
    return pl.pallas_call(
        functools.partial(_conv_body, L=L),
        grid=(B,),
        in_specs=[
            pl.BlockSpec((None, L, 2 * D_CONV), lambda i: (i, 0, 0)),
            pl.BlockSpec((CONV_WIDTH + 1, D_CONV), lambda i: (0, 0)),
            vec, vec, vec, vec,
        ],
        out_specs=pl.BlockSpec((None, L, D_CONV), lambda i: (i, 0, 0)),
        out_shape=jax.ShapeDtypeStruct((B, L, D_CONV), BF16),
        scratch_shapes=[
            pltpu.VMEM((L + 2 * CONV_HALO, D_CONV), F32),
            pltpu.VMEM((CONV_TILE, D_CONV), F32),
        ],
        compiler_params=pltpu.CompilerParams(
            dimension_semantics=("arbitrary",), vmem_limit_bytes=VMEM_LIMIT),
        name="conformer_conv",
    )(uc, w, b, lng, lnb, og)


ROUTER_FIRST_EXPERT_LANE = N_GROUPS
COL_E0, COL_E1, COL_R0, COL_R1, COL_G0, COL_G1 = range(6)
ROUTE_ROWS = 8


def _out_proj_body(x_ref, a_ref, c_ref, ag_ref, wa_ref, wc_ref, n2g_ref, wr_ref, br_ref,
                   x1_ref, h2_ref, route_ref, route_t_ref, cnt_ref, carry_ref, *, tm):
    @pl.when(pl.program_id(0) == 0)
    def _():
        carry_ref[...] = jnp.zeros_like(carry_ref)

    an = _rms(a_ref[...].astype(F32), ag_ref[...]).astype(BF16)
    y = jnp.dot(an, wa_ref[...], preferred_element_type=F32)
    y = y + jnp.dot(c_ref[...], wc_ref[...], preferred_element_type=F32)
    x1 = x_ref[...] + y
    x1_ref[...] = x1
    h2 = _rms(x1, n2g_ref[...])
    h2_ref[...] = h2

    logits = jnp.dot(h2, wr_ref[...], preferred_element_type=F32,
                     precision=lax.Precision.HIGHEST) + br_ref[...]
    lane = lax.broadcasted_iota(I32, (tm, LANES), 1).astype(F32)
    ninf = jnp.full((tm, LANES), -jnp.inf, F32)
    big = jnp.full((tm, LANES), float(LANES), F32)

    def first_argmax(vals):
        top = jnp.max(vals, axis=-1, keepdims=True)
        idx = jnp.min(jnp.where(vals == top, lane, big), axis=-1, keepdims=True)
        return top, idx

    is_group = lane < N_GROUPS
    gtop, gsel = first_argmax(jnp.where(is_group, logits, ninf))
    p_g = 1.0 / jnp.sum(jnp.where(is_group, jnp.exp(logits - gtop), 0.0), axis=-1, keepdims=True)
    lo = ROUTER_FIRST_EXPERT_LANE + EPG * gsel
    el = jnp.where((lane >= lo) & (lane < lo + EPG), logits, ninf)
    v0, i0 = first_argmax(el)
    v1, i1 = first_argmax(jnp.where(lane == i0, ninf, el))
    t = jnp.exp(v1 - v0)
    g0 = p_g / (1.0 + t)
    g1 = p_g * t / (1.0 + t)

    onehot = jnp.where((lane == i0) | (lane == i1), 1.0, 0.0)
    ri = lax.broadcasted_iota(I32, (tm, tm), 0)
    ci = lax.broadcasted_iota(I32, (tm, tm), 1)
    tri = jnp.where(ci <= ri, 1.0, 0.0).astype(BF16)
    cum = jnp.dot(tri, onehot.astype(BF16), preferred_element_type=F32) + carry_ref[0:1, :]
    r0 = jnp.sum(jnp.where(lane == i0, cum, 0.0), axis=-1, keepdims=True) - 1.0
    r1 = jnp.sum(jnp.where(lane == i1, cum, 0.0), axis=-1, keepdims=True) - 1.0
    carry = carry_ref[0:1, :] + jnp.sum(onehot, axis=0, keepdims=True)
    carry_ref[...] = jnp.broadcast_to(carry, carry_ref.shape)
    cnt_ref[...] = jnp.broadcast_to(carry, cnt_ref.shape)

    route = jnp.zeros((tm, LANES), F32)
    for col, val in ((COL_E0, i0 - ROUTER_FIRST_EXPERT_LANE), (COL_E1, i1 - ROUTER_FIRST_EXPERT_LANE),
                     (COL_R0, r0), (COL_R1, r1), (COL_G0, g0), (COL_G1, g1)):
        route = jnp.where(lane == col, val, route)
    route_ref[...] = route
    route_t_ref[...] = route.T[0:ROUTE_ROWS, :]


def _out_proj(x2, a2, c2, ag, wa, wc, n2g, wr, br):
    T = x2.shape[0]
    tm = TOKEN_TILE
    full = lambda shape: pl.BlockSpec(shape, lambda i: (0,) * len(shape))
    return pl.pallas_call(
        functools.partial(_out_proj_body, tm=tm),
        grid=(T // tm,),
        in_specs=[
            pl.BlockSpec((tm, D_MODEL), lambda i: (i, 0)),
            pl.BlockSpec((tm, D_ATTN), lambda i: (i, 0)),
            pl.BlockSpec((tm, D_CONV), lambda i: (i, 0)),
            full((1, D_ATTN)),
            full((D_ATTN, D_MODEL)),
            full((D_CONV, D_MODEL)),
            full((1, D_MODEL)),
            full((D_MODEL, LANES)),
            full((1, LANES)),
        ],
        out_specs=[
            pl.BlockSpec((tm, D_MODEL), lambda i: (i, 0)),
            pl.BlockSpec((tm, D_MODEL), lambda i: (i, 0)),
            pl.BlockSpec((tm, LANES), lambda i: (i, 0)),
            pl.BlockSpec((ROUTE_ROWS, tm), lambda i: (0, i)),
            full((8, LANES)),
        ],
        out_shape=[
            jax.ShapeDtypeStruct((T, D_MODEL), F32),
            jax.ShapeDtypeStruct((T, D_MODEL), F32),
            jax.ShapeDtypeStruct((T, LANES), F32),
            jax.ShapeDtypeStruct((ROUTE_ROWS, T), F32),
            jax.ShapeDtypeStruct((8, LANES), F32),
        ],
        scratch_shapes=[pltpu.VMEM((8, LANES), F32)],
        compiler_params=pltpu.CompilerParams(
            dimension_semantics=("arbitrary",), vmem_limit_bytes=VMEM_LIMIT),
        name="out_proj_router",
    )(x2, a2, c2, ag, wa, wc, n2g, wr, br)


def _load_dest(i, d0_hbm, d1_hbm, s0, s1, isem):
    c0 = pltpu.make_async_copy(d0_hbm.at[i], s0, isem.at[0])
    c1 = pltpu.make_async_copy(d1_hbm.at[i], s1, isem.at[1])
    c0.start()
    c1.start()
    c0.wait()
    c1.wait()


def _dispatch_body(d0_hbm, d1_hbm, h_ref, xb_in, xb_out, s0, s1, isem, sem, *, tm):
    del xb_in
    _load_dest(pl.program_id(0), d0_hbm, d1_hbm, s0, s1, isem)

    def issue(t, carry):
        row = h_ref.at[pl.ds(t, 1)]
        pltpu.make_async_copy(row, xb_out.at[pl.ds(s0[t], 1)], sem).start()
        pltpu.make_async_copy(row, xb_out.at[pl.ds(s1[t], 1)], sem).start()
        return carry

    lax.fori_loop(0, tm, issue, 0, unroll=8)
    for _ in range(2):
        pltpu.make_async_copy(h_ref, xb_out.at[pl.ds(0, tm)], sem).wait()


def _dispatch(d0, d1, h2, n_slots):
    T = h2.shape[0]
    tm = TOKEN_TILE
    xb0 = jnp.zeros((n_slots, D_MODEL), F32)
    return pl.pallas_call(
        functools.partial(_dispatch_body, tm=tm),
        grid=(T // tm,),
        in_specs=[
            pl.BlockSpec(memory_space=pl.ANY),
            pl.BlockSpec(memory_space=pl.ANY),
            pl.BlockSpec((tm, D_MODEL), lambda i: (i, 0)),
            pl.BlockSpec(memory_space=pl.ANY),
        ],
        out_specs=pl.BlockSpec(memory_space=pl.ANY),
        out_shape=jax.ShapeDtypeStruct((n_slots, D_MODEL), F32),
        scratch_shapes=[
            pltpu.SMEM((tm,), I32),
            pltpu.SMEM((tm,), I32),
            pltpu.SemaphoreType.DMA((2,)),
            pltpu.SemaphoreType.DMA(()),
        ],
        input_output_aliases={3: 0},
        compiler_params=pltpu.CompilerParams(
            dimension_semantics=("arbitrary",), vmem_limit_bytes=VMEM_LIMIT,
            has_side_effects=True),
        name="moe_dispatch",
    )(d0.reshape(T // tm, tm), d1.reshape(T // tm, tm), h2, xb0)


def _expert_body(be_ref, nu_ref, xb_ref, wg_ref, wu_ref, wd_ref, y_ref):
    del be_ref
    used = pl.program_id(0) < nu_ref[0]

    @pl.when(used)
    def _():
        x = xb_ref[...].astype(BF16)
        g = jnp.dot(x, wg_ref[...], preferred_element_type=F32)
        u = jnp.dot(x, wu_ref[...], preferred_element_type=F32)
        hid = (g * jax.nn.sigmoid(g) * u).astype(BF16)
        y_ref[...] = jnp.dot(hid, wd_ref[...], preferred_element_type=F32)

    @pl.when(jnp.logical_not(used))
    def _():
        y_ref[...] = jnp.zeros_like(y_ref)


def _experts(block_e, n_used, xb, wg, wu, wd):
    n_slots = xb.shape[0]
    blk = EXPERT_BLOCK
    grid_spec = pltpu.PrefetchScalarGridSpec(
        num_scalar_prefetch=2,
        grid=(n_slots // blk,),
        in_specs=[
            pl.BlockSpec((blk, D_MODEL), lambda b, be, nu: (b, 0)),
            pl.BlockSpec((None, D_MODEL, D_EXPERT), lambda b, be, nu: (be[b], 0, 0)),
            pl.BlockSpec((None, D_MODEL, D_EXPERT), lambda b, be, nu: (be[b], 0, 0)),
            pl.BlockSpec((None, D_EXPERT, D_MODEL), lambda b, be, nu: (be[b], 0, 0)),
        ],
        out_specs=pl.BlockSpec((blk, D_MODEL), lambda b, be, nu: (b, 0)),
    )
    return pl.pallas_call(
        _expert_body,
        grid_spec=grid_spec,
        out_shape=jax.ShapeDtypeStruct((n_slots, D_MODEL), F32),
        compiler_params=pltpu.CompilerParams(
            dimension_semantics=("arbitrary",), vmem_limit_bytes=VMEM_LIMIT),
        name="moe_experts",
    )(block_e, n_used, xb, wg, wu, wd)


def _combine_body(d0_hbm, d1_hbm, x1_ref, route_ref, fg_ref, y_hbm, o_ref,
                  s0, s1, b0, b1, isem, sem, *, tm):
    _load_dest(pl.program_id(0), d0_hbm, d1_hbm, s0, s1, isem)

    def issue(t, carry):
        pltpu.make_async_copy(y_hbm.at[pl.ds(s0[t], 1)], b0.at[pl.ds(t, 1)], sem).start()
        pltpu.make_async_copy(y_hbm.at[pl.ds(s1[t], 1)], b1.at[pl.ds(t, 1)], sem).start()
        return carry

    lax.fori_loop(0, tm, issue, 0, unroll=8)
    pltpu.make_async_copy(y_hbm.at[pl.ds(0, tm)], b0, sem).wait()
    pltpu.make_async_copy(y_hbm.at[pl.ds(0, tm)], b1, sem).wait()

    route = route_ref[...]
    g0 = route[:, COL_G0:COL_G0 + 1]
    g1 = route[:, COL_G1:COL_G1 + 1]
    x = x1_ref[...] + (g0 * b0[...] + g1 * b1[...])
    o_ref[...] = _rms(x, fg_ref[...])


def _combine(d0, d1, x1, route, fg, y):
    T = x1.shape[0]
    tm = TOKEN_TILE
    return pl.pallas_call(
        functools.partial(_combine_body, tm=tm),
        grid=(T // tm,),
        in_specs=[
            pl.BlockSpec(memory_space=pl.ANY),
            pl.BlockSpec(memory_space=pl.ANY),
            pl.BlockSpec((tm, D_MODEL), lambda i: (i, 0)),
            pl.BlockSpec((tm, LANES), lambda i: (i, 0)),
            pl.BlockSpec((1, D_MODEL), lambda i: (0, 0)),
            pl.BlockSpec(memory_space=pl.ANY),
        ],
        out_specs=pl.BlockSpec((tm, D_MODEL), lambda i: (i, 0)),
        out_shape=jax.ShapeDtypeStruct((T, D_MODEL), F32),
        scratch_shapes=[
            pltpu.SMEM((tm,), I32),
            pltpu.SMEM((tm,), I32),
            pltpu.VMEM((tm, D_MODEL), F32),
            pltpu.VMEM((tm, D_MODEL), F32),
            pltpu.SemaphoreType.DMA((2,)),
            pltpu.SemaphoreType.DMA(()),
        ],
        compiler_params=pltpu.CompilerParams(
            dimension_semantics=("arbitrary",), vmem_limit_bytes=VMEM_LIMIT),
        name="moe_combine",
    )(d0.reshape(T // tm, tm), d1.reshape(T // tm, tm), x1, route, fg, y)


def _trunk(x, p):
    B, L, _ = x.shape
    T = B * L
    x2 = x.reshape(T, D_MODEL)
    qkv, uc = _in_proj(x2, p["norm1_g"], p["wqkv"], p["wconv"])
    a = _attention(qkv.reshape(B, L, 3 * D_ATTN), p["na_bias"])
    c = _conv(uc.reshape(B, L, 2 * D_CONV), p["conv_w"], p["conv_b"], p["conv_ln_g"],
              p["conv_ln_b"], p["conv_out_g"])
    x1, h2, route, route_t, cnt = _out_proj(
        x2, a.reshape(T, D_ATTN), c.reshape(T, D_CONV), p["attn_out_g"], p["wout_a"],
        p["wout_c"], p["norm2_g"], p["w_router"], p["b_router"])

    blk = EXPERT_BLOCK
    e0 = route_t[COL_E0].astype(I32)
    e1 = route_t[COL_E1].astype(I32)
    counts = cnt[0, ROUTER_FIRST_EXPERT_LANE:ROUTER_FIRST_EXPERT_LANE + N_EXPERTS].astype(I32)
    padded = (counts + blk - 1) // blk * blk
    pad_end = jnp.cumsum(padded)
    pad_start = pad_end - padded
    d0 = pad_start[e0] + route_t[COL_R0].astype(I32)
    d1 = pad_start[e1] + route_t[COL_R1].astype(I32)
    n_blocks = 2 * T // blk + N_EXPERTS
    block_start = jnp.arange(n_blocks, dtype=I32) * blk
    block_e = jnp.clip(jnp.searchsorted(pad_end, block_start, side="right"), 0,
                       N_EXPERTS - 1).astype(I32)
    n_used = (pad_end[-1:] // blk).astype(I32)

    xb = _dispatch(d0, d1, h2, n_blocks * blk)
    y = _experts(block_e, n_used, xb, p["w_gate"], p["w_up"], p["w_down"])
    out = _combine(d0, d1, x1, route, p["final_g"], y)
    return out.reshape(B, L, D_MODEL)


def kernel(x_prompt, x_sample, norm1_g, w_in, rpb, attn_out_g, conv_w, conv_b, conv_ln_g,
           conv_ln_b, conv_out_g, w_out, norm2_g, w_group, b_group, w_expert, b_expert,
           w_e_gate, w_e_up, w_e_down, final_g):
    l = 0
    w_router = jnp.concatenate(
        [w_group[l], w_expert[l].transpose(1, 0, 2).reshape(D_MODEL, N_EXPERTS)], axis=1)
    w_router = jnp.pad(w_router, ((0, 0), (0, LANES - w_router.shape[1])))
    b_router = jnp.concatenate([b_group[l], b_expert[l].reshape(N_EXPERTS)])
    b_router = jnp.pad(b_router, (0, LANES - b_router.shape[0])).reshape(1, LANES)
    p = {
        "norm1_g": norm1_g[l].reshape(1, D_MODEL),
        "wqkv": w_in[l][:, :3 * D_ATTN].astype(BF16),
        "wconv": w_in[l][:, 3 * D_ATTN:].astype(BF16),
        "na_bias": _na_bias_table(rpb[l]),
        "attn_out_g": attn_out_g[l].reshape(1, D_ATTN),
        "conv_w": jnp.pad(conv_w[l], ((0, 1), (0, 0))),
        "conv_b": conv_b[l].reshape(1, D_CONV),
        "conv_ln_g": conv_ln_g[l].reshape(1, D_CONV),
        "conv_ln_b": conv_ln_b[l].reshape(1, D_CONV),
        "conv_out_g": conv_out_g[l].reshape(1, D_CONV),
        "wout_a": w_out[l][:D_ATTN].astype(BF16),
        "wout_c": w_out[l][D_ATTN:].astype(BF16),
        "norm2_g": norm2_g[l].reshape(1, D_MODEL),
        "w_router": w_router,
        "b_router": b_router,
        "w_gate": w_e_gate[l].astype(BF16),
        "w_up": w_e_up[l].astype(BF16),
        "w_down": w_e_down[l].astype(BF16),
        "final_g": final_g.reshape(1, D_MODEL),
    }
    return (_trunk(x_prompt, p), _trunk(x_sample, p))
```

```python
import functools

import jax
import jax.numpy as jnp
from jax import lax
from jax.experimental import pallas as pl
from jax.experimental.pallas import tpu as pltpu

F32 = jnp.float32
BF16 = jnp.bfloat16
I32 = jnp.int32

D_MODEL = 1024
GRID_W = 64
D_ATTN = 512
D_CONV = 512
HEAD_DIM = 64
N_HEADS = 8
NA_ROWS = 8
NA_COLS = 16
CONV_WIDTH = 31
N_GROUPS = 4
EPG = 8
N_EXPERTS = 32
D_EXPERT = 512
EPS = 1e-6

LANES = 128
HEAD_PAIRS = N_HEADS * HEAD_DIM // LANES
NA_KEYS = NA_ROWS * GRID_W
MASKED = -1e30
TOKEN_TILE = 512
EXPERT_BLOCK = 512
CONV_TILE = 64
CONV_HALO = 16
VMEM_LIMIT = 56 * 1024 * 1024


def _rms(x, g):
    return x * lax.rsqrt(jnp.mean(x * x, axis=-1, keepdims=True) + EPS) * g


def _in_proj_body(x_ref, g_ref, wqkv_ref, wc_ref, qkv_ref, uc_ref):
    h = _rms(x_ref[...], g_ref[...]).astype(BF16)
    qkv_ref[...] = jnp.dot(h, wqkv_ref[...], preferred_element_type=F32).astype(BF16)
    uc_ref[...] = jnp.dot(h, wc_ref[...], preferred_element_type=F32).astype(BF16)


def _in_proj(x2, g, wqkv, wc):
    T = x2.shape[0]
    tm = TOKEN_TILE
    return pl.pallas_call(
        _in_proj_body,
        grid=(T // tm,),
        in_specs=[
            pl.BlockSpec((tm, D_MODEL), lambda i: (i, 0)),
            pl.BlockSpec((1, D_MODEL), lambda i: (0, 0)),
            pl.BlockSpec((D_MODEL, 3 * D_ATTN), lambda i: (0, 0)),
            pl.BlockSpec((D_MODEL, 2 * D_CONV), lambda i: (0, 0)),
        ],
        out_specs=[
            pl.BlockSpec((tm, 3 * D_ATTN), lambda i: (i, 0)),
            pl.BlockSpec((tm, 2 * D_CONV), lambda i: (i, 0)),
        ],
        out_shape=[
            jax.ShapeDtypeStruct((T, 3 * D_ATTN), BF16),
            jax.ShapeDtypeStruct((T, 2 * D_CONV), BF16),
        ],
        compiler_params=pltpu.CompilerParams(
            dimension_semantics=("arbitrary",), vmem_limit_bytes=VMEM_LIMIT),
        name="in_proj",
    )(x2, g, wqkv, wc)


def _na_bias_table(rpb):
    c = jnp.arange(GRID_W)
    col_start = jnp.clip(c - NA_COLS // 2, 0, GRID_W - NA_COLS)
    cp = jnp.arange(GRID_W)
    valid = (cp[None, :] >= col_start[:, None]) & (cp[None, :] < col_start[:, None] + NA_COLS)
    col_off = cp[None, :] - c[:, None] + (NA_COLS - 1)
    sel = (col_off[None] == jnp.arange(2 * NA_COLS - 1)[:, None, None]) & valid[None]
    a = jnp.einsum("hrd,dcx->hrcx", rpb, sel.astype(F32), precision=lax.Precision.HIGHEST)
    a = jnp.where(valid[None, None], a, MASKED)
    t = jnp.stack([a[:, NA_ROWS - 1 - p:2 * NA_ROWS - 1 - p] for p in range(NA_ROWS)], axis=1)
    t = t.transpose(0, 1, 3, 2, 4)
    t = t.reshape(HEAD_PAIRS, 2, NA_ROWS, GRID_W, NA_KEYS)
    return t.transpose(0, 2, 1, 3, 4).reshape(HEAD_PAIRS, NA_ROWS, LANES, NA_KEYS).astype(F32)


def _attn_body(q_ref, k_ref, v_ref, bias_ref, o_ref, *, rows, rows_per_step):
    rb = pl.program_id(2)
    first_head = lax.broadcasted_iota(I32, (GRID_W, LANES), 1) < HEAD_DIM
    zero = jnp.zeros((GRID_W, LANES), BF16)
    for rr in range(rows_per_step):
        r = rb * rows_per_step + rr
        r_start = jnp.clip(r - NA_ROWS // 2, 0, rows - NA_ROWS)
        p = r - r_start
        q = q_ref[rr * GRID_W:(rr + 1) * GRID_W, :] * jnp.asarray(HEAD_DIM ** -0.5, BF16)
        qbd = jnp.concatenate([jnp.where(first_head, q, zero), jnp.where(first_head, zero, q)], axis=0)
        k0 = pl.multiple_of(r_start * GRID_W, GRID_W)
        ks = k_ref[pl.ds(k0, NA_KEYS), :]
        vs = v_ref[pl.ds(k0, NA_KEYS), :]
        s = lax.dot_general(qbd, ks, (((1,), (1,)), ((), ())), preferred_element_type=F32)
        s = s + bias_ref[p]
        m = jnp.max(s, axis=-1, keepdims=True)
        e = jnp.exp(s - m)
        l = jnp.sum(e, axis=-1, keepdims=True)
        o = jnp.dot(e.astype(BF16), vs, preferred_element_type=F32) / l
        out = jnp.where(first_head, o[:GRID_W], o[GRID_W:])
        o_ref[rr * GRID_W:(rr + 1) * GRID_W, :] = out.astype(BF16)


def _attention(qkv, bias):
    B, L, _ = qkv.shape
    rows = L // GRID_W
    rows_per_step = 8
    tq = rows_per_step * GRID_W
    body = functools.partial(_attn_body, rows=rows, rows_per_step=rows_per_step)
    return pl.pallas_call(
        body,
        grid=(HEAD_PAIRS, B, L // tq),
        in_specs=[
            pl.BlockSpec((None, tq, LANES), lambda hp, b, rb: (b, rb, hp)),
            pl.BlockSpec((None, L, LANES), lambda hp, b, rb: (b, 0, HEAD_PAIRS + hp)),
            pl.BlockSpec((None, L, LANES), lambda hp, b, rb: (b, 0, 2 * HEAD_PAIRS + hp)),
            pl.BlockSpec((None, NA_ROWS, LANES, NA_KEYS), lambda hp, b, rb: (hp, 0, 0, 0)),
        ],
        out_specs=pl.BlockSpec((None, tq, LANES), lambda hp, b, rb: (b, rb, hp)),
        out_shape=jax.ShapeDtypeStruct((B, L, D_ATTN), BF16),
        compiler_params=pltpu.CompilerParams(
            dimension_semantics=("arbitrary", "arbitrary", "arbitrary"),
            vmem_limit_bytes=VMEM_LIMIT),
        name="na_attention",
    )(qkv, qkv, qkv, bias)


def _conv_body(uc_ref, w_ref, b_ref, lng_ref, lnb_ref, og_ref, o_ref, zpad_ref, cbuf_ref, *, L):
    glu_rows = 256
    zpad_ref[0:CONV_HALO, :] = jnp.zeros((CONV_HALO, D_CONV), F32)
    zpad_ref[L + CONV_HALO:L + 2 * CONV_HALO, :] = jnp.zeros((CONV_HALO, D_CONV), F32)

    def glu(i, carry):
        t0 = pl.multiple_of(i * glu_rows, glu_rows)
        a = uc_ref[pl.ds(t0, glu_rows), 0:D_CONV].astype(F32)
        gt = uc_ref[pl.ds(t0, glu_rows), D_CONV:2 * D_CONV].astype(F32)
        zpad_ref[pl.ds(t0 + CONV_HALO, glu_rows), :] = a * jax.nn.sigmoid(gt)
        return carry

    lax.fori_loop(0, L // glu_rows, glu, 0)

    tt = CONV_TILE
    first = CONV_HALO - CONV_WIDTH // 2
    span = tt + 24

    def tile(i, carry):
        t0 = pl.multiple_of(i * tt, tt)
        for j in range(D_CONV // LANES):
            lanes = slice(j * LANES, (j + 1) * LANES)
            chunk = zpad_ref[pl.ds(t0, tt + 2 * CONV_HALO), lanes]
            acc = jnp.broadcast_to(b_ref[:, lanes], (tt, LANES))
            for s in range(8):
                shifted = chunk[s:s + span]
                for a8 in range(4):
                    k = 8 * a8 + s - first
                    if 0 <= k < CONV_WIDTH:
                        acc = acc + w_ref[k:k + 1, lanes] * shifted[8 * a8:8 * a8 + tt]
            cbuf_ref[:, lanes] = acc
        y = cbuf_ref[...]
        mu = jnp.mean(y, axis=-1, keepdims=True)
        yc = y - mu
        var = jnp.mean(yc * yc, axis=-1, keepdims=True)
        yn = yc * lax.rsqrt(var + EPS) * lng_ref[...] + lnb_ref[...]
        sw = yn * jax.nn.sigmoid(yn)
        o_ref[pl.ds(t0, tt), :] = _rms(sw, og_ref[...]).astype(BF16)
        return carry

    lax.fori_loop(0, L // tt, tile, 0)


def _conv(uc, w, b, lng, lnb, og):
    B, L, _ = uc.shape
    vec = pl.BlockSpec((1, D_CONV), lambda i: (0, 0))
    return pl.pallas_call(
        functools.partial(_conv_body, L=L),
        grid=(B,),
        in_specs=[
            pl.BlockSpec((None, L, 2 * D_CONV), lambda i: (i, 0, 0)),
            pl.BlockSpec((CONV_WIDTH + 1, D_CONV), lambda i: (0, 0)),
            vec, vec, vec, vec,
        ],
        out_specs=pl.BlockSpec((None, L, D_CONV), lambda i: (i, 0, 0)),
        out_shape=jax.ShapeDtypeStruct((B, L, D_CONV), BF16),
        scratch_shapes=[
            pltpu.VMEM((L + 2 * CONV_HALO, D_CONV), F32),
            pltpu.VMEM((CONV_TILE, D_CONV), F32),
        ],
        compiler_params=pltpu.CompilerParams(
            dimension_semantics=("arbitrary",), vmem_limit_bytes=VMEM_LIMIT),
        name="conformer_conv",
    )(uc, w, b, lng, lnb, og)


ROUTER_FIRST_EXPERT_LANE = N_GROUPS
COL_E0, COL_E1, COL_R0, COL_R1, COL_G0, COL_G1 = range(6)
ROUTE_ROWS = 8


def _out_proj_body(x_ref, a_ref, c_ref, ag_ref, wa_ref, wc_ref, n2g_ref, wr_ref, br_ref,
                   x1_ref, h2_ref, route_ref, route_t_ref, cnt_ref, carry_ref, *, tm):
    @pl.when(pl.program_id(0) == 0)
    def _():
        carry_ref[...] = jnp.zeros_like(carry_ref)

    an = _rms(a_ref[...].astype(F32), ag_ref[...]).astype(BF16)
    y = jnp.dot(an, wa_ref[...], preferred_element_type=F32)
    y = y + jnp.dot(c_ref[...], wc_ref[...], preferred_element_type=F32)
    x1 = x_ref[...] + y
    x1_ref[...] = x1
    h2 = _rms(x1, n2g_ref[...])
    h2_ref[...] = h2

    h_hi = h2.astype(BF16)
    h_lo = (h2 - h_hi.astype(F32)).astype(BF16)
    split = (jnp.dot(h_hi, wr_ref[...], preferred_element_type=F32)
             + jnp.dot(h_lo, wr_ref[...], preferred_element_type=F32))
    logits = split[:, :LANES] + split[:, LANES:] + br_ref[...]
    lane = lax.broadcasted_iota(I32, (tm, LANES), 1).astype(F32)
    ninf = jnp.full((tm, LANES), -jnp.inf, F32)
    big = jnp.full((tm, LANES), float(LANES), F32)

    def first_argmax(vals):
        top = jnp.max(vals, axis=-1, keepdims=True)
        idx = jnp.min(jnp.where(vals == top, lane, big), axis=-1, keepdims=True)
        return top, idx

    is_group = lane < N_GROUPS
    gtop, gsel = first_argmax(jnp.where(is_group, logits, ninf))
    p_g = 1.0 / jnp.sum(jnp.where(is_group, jnp.exp(logits - gtop), 0.0), axis=-1, keepdims=True)
    lo = ROUTER_FIRST_EXPERT_LANE + EPG * gsel
    el = jnp.where((lane >= lo) & (lane < lo + EPG), logits, ninf)
    v0, i0 = first_argmax(el)
    v1, i1 = first_argmax(jnp.where(lane == i0, ninf, el))
    t = jnp.exp(v1 - v0)
    g0 = p_g / (1.0 + t)
    g1 = p_g * t / (1.0 + t)

    onehot = jnp.where((lane == i0) | (lane == i1), 1.0, 0.0)
    ri = lax.broadcasted_iota(I32, (tm, tm), 0)
    ci = lax.broadcasted_iota(I32, (tm, tm), 1)
    tri = jnp.where(ci <= ri, 1.0, 0.0).astype(BF16)
    cum = jnp.dot(tri, onehot.astype(BF16), preferred_element_type=F32) + carry_ref[0:1, :]
    r0 = jnp.sum(jnp.where(lane == i0, cum, 0.0), axis=-1, keepdims=True) - 1.0
    r1 = jnp.sum(jnp.where(lane == i1, cum, 0.0), axis=-1, keepdims=True) - 1.0
    carry = carry_ref[0:1, :] + jnp.sum(onehot, axis=0, keepdims=True)
    carry_ref[...] = jnp.broadcast_to(carry, carry_ref.shape)
    cnt_ref[...] = jnp.broadcast_to(carry, cnt_ref.shape)

    route = jnp.zeros((tm, LANES), F32)
    for col, val in ((COL_E0, i0 - ROUTER_FIRST_EXPERT_LANE), (COL_E1, i1 - ROUTER_FIRST_EXPERT_LANE),
                     (COL_R0, r0), (COL_R1, r1), (COL_G0, g0), (COL_G1, g1)):
        route = jnp.where(lane == col, val, route)
    route_ref[...] = route
    route_t_ref[...] = route.T[0:ROUTE_ROWS, :]


def _out_proj(x2, a2, c2, ag, wa, wc, n2g, wr, br):
    T = x2.shape[0]
    tm = TOKEN_TILE
    full = lambda shape: pl.BlockSpec(shape, lambda i: (0,) * len(shape))
    return pl.pallas_call(
        functools.partial(_out_proj_body, tm=tm),
        grid=(T // tm,),
        in_specs=[
            pl.BlockSpec((tm, D_MODEL), lambda i: (i, 0)),
            pl.BlockSpec((tm, D_ATTN), lambda i: (i, 0)),
            pl.BlockSpec((tm, D_CONV), lambda i: (i, 0)),
            full((1, D_ATTN)),
            full((D_ATTN, D_MODEL)),
            full((D_CONV, D_MODEL)),
            full((1, D_MODEL)),
            full((D_MODEL, 2 * LANES)),
            full((1, LANES)),
        ],
        out_specs=[
            pl.BlockSpec((tm, D_MODEL), lambda i: (i, 0)),
            pl.BlockSpec((tm, D_MODEL), lambda i: (i, 0)),
            pl.BlockSpec((tm, LANES), lambda i: (i, 0)),
            pl.BlockSpec((ROUTE_ROWS, tm), lambda i: (0, i)),
            full((8, LANES)),
        ],
        out_shape=[
            jax.ShapeDtypeStruct((T, D_MODEL), F32),
            jax.ShapeDtypeStruct((T, D_MODEL), F32),
            jax.ShapeDtypeStruct((T, LANES), F32),
            jax.ShapeDtypeStruct((ROUTE_ROWS, T), F32),
            jax.ShapeDtypeStruct((8, LANES), F32),
        ],
        scratch_shapes=[pltpu.VMEM((8, LANES), F32)],
        compiler_params=pltpu.CompilerParams(
            dimension_semantics=("arbitrary",), vmem_limit_bytes=VMEM_LIMIT),
        name="out_proj_router",
    )(x2, a2, c2, ag, wa, wc, n2g, wr, br)


def _load_dest(i, d0_hbm, d1_hbm, s0, s1, isem):
    c0 = pltpu.make_async_copy(d0_hbm.at[i], s0, isem.at[0])
    c1 = pltpu.make_async_copy(d1_hbm.at[i], s1, isem.at[1])
    c0.start()
    c1.start()
    c0.wait()
    c1.wait()


def _dispatch_body(lb_ref, d0_hbm, d1_hbm, h_ref, xb_out, s0, s1, zero_ref, isem, sem, *, tm):
    blk = EXPERT_BLOCK
    n_blocks = xb_out.shape[0] // blk

    @pl.when(pl.program_id(0) == 0)
    def _():
        zero_ref[...] = jnp.zeros_like(zero_ref)
        n_used = lb_ref[N_EXPERTS]

        def block(b):
            return xb_out.at[pl.ds(pl.multiple_of(b * blk, blk), blk)]

        def fill(e, carry):
            @pl.when(lb_ref[e] >= 0)
            def _():
                pltpu.make_async_copy(zero_ref, block(lb_ref[e]), sem).start()
            return carry

        def drain(e, carry):
            @pl.when(lb_ref[e] >= 0)
            def _():
                pltpu.make_async_copy(zero_ref, block(lb_ref[e]), sem).wait()
            return carry

        def fill_tail(b, carry):
            pltpu.make_async_copy(zero_ref, block(b), sem).start()
            return carry

        def drain_tail(b, carry):
            pltpu.make_async_copy(zero_ref, block(b), sem).wait()
            return carry

        lax.fori_loop(0, N_EXPERTS, fill, 0)
        lax.fori_loop(n_used, n_blocks, fill_tail, 0)
        lax.fori_loop(0, N_EXPERTS, drain, 0)
        lax.fori_loop(n_used, n_blocks, drain_tail, 0)

    _load_dest(pl.program_id(0), d0_hbm, d1_hbm, s0, s1, isem)

    def issue(t, carry):
        row = h_ref.at[pl.ds(t, 1)]
        pltpu.make_async_copy(row, xb_out.at[pl.ds(s0[t], 1)], sem).start()
        pltpu.make_async_copy(row, xb_out.at[pl.ds(s1[t], 1)], sem).start()
        return carry

    lax.fori_loop(0, tm, issue, 0, unroll=8)
    for _ in range(2):
        pltpu.make_async_copy(h_ref, xb_out.at[pl.ds(0, tm)], sem).wait()


def _dispatch(last_blk, d0, d1, h2, n_slots):
    T = h2.shape[0]
    tm = TOKEN_TILE
    grid_spec = pltpu.PrefetchScalarGridSpec(
        num_scalar_prefetch=1,
        grid=(T // tm,),
        in_specs=[
            pl.BlockSpec(memory_space=pl.ANY),
            pl.BlockSpec(memory_space=pl.ANY),
            pl.BlockSpec((tm, D_MODEL), lambda i, lb: (i, 0)),
        ],
        out_specs=pl.BlockSpec(memory_space=pl.ANY),
        scratch_shapes=[
            pltpu.SMEM((tm,), I32),
            pltpu.SMEM((tm,), I32),
            pltpu.VMEM((EXPERT_BLOCK, D_MODEL), F32),
            pltpu.SemaphoreType.DMA((2,)),
            pltpu.SemaphoreType.DMA(()),
        ],
    )
    return pl.pallas_call(
        functools.partial(_dispatch_body, tm=tm),
        grid_spec=grid_spec,
        out_shape=jax.ShapeDtypeStruct((n_slots, D_MODEL), F32),
        compiler_params=pltpu.CompilerParams(
            dimension_semantics=("arbitrary",), vmem_limit_bytes=VMEM_LIMIT,
            has_side_effects=True),
        name="moe_dispatch",
    )(last_blk, d0.reshape(T // tm, tm), d1.reshape(T // tm, tm), h2)


def _expert_body(be_ref, nu_ref, xb_ref, wg_ref, wu_ref, wd_ref, y_ref):
    del be_ref
    used = pl.program_id(0) < nu_ref[0]

    @pl.when(used)
    def _():
        x = xb_ref[...].astype(BF16)
        g = jnp.dot(x, wg_ref[...], preferred_element_type=F32)
        u = jnp.dot(x, wu_ref[...], preferred_element_type=F32)
        hid = (g * jax.nn.sigmoid(g) * u).astype(BF16)
        y_ref[...] = jnp.dot(hid, wd_ref[...], preferred_element_type=F32)

    @pl.when(jnp.logical_not(used))
    def _():
        y_ref[...] = jnp.zeros_like(y_ref)


def _experts(block_e, n_used, xb, wg, wu, wd):
    n_slots = xb.shape[0]
    blk = EXPERT_BLOCK
    grid_spec = pltpu.PrefetchScalarGridSpec(
        num_scalar_prefetch=2,
        grid=(n_slots // blk,),
        in_specs=[
            pl.BlockSpec((blk, D_MODEL), lambda b, be, nu: (b, 0)),
            pl.BlockSpec((None, D_MODEL, D_EXPERT), lambda b, be, nu: (be[b], 0, 0)),
            pl.BlockSpec((None, D_MODEL, D_EXPERT), lambda b, be, nu: (be[b], 0, 0)),
            pl.BlockSpec((None, D_EXPERT, D_MODEL), lambda b, be, nu: (be[b], 0, 0)),
        ],
        out_specs=pl.BlockSpec((blk, D_MODEL), lambda b, be, nu: (b, 0)),
    )
    return pl.pallas_call(
        _expert_body,
        grid_spec=grid_spec,
        out_shape=jax.ShapeDtypeStruct((n_slots, D_MODEL), F32),
        compiler_params=pltpu.CompilerParams(
            dimension_semantics=("arbitrary",), vmem_limit_bytes=VMEM_LIMIT),
        name="moe_experts",
    )(block_e, n_used, xb, wg, wu, wd)


def _combine_body(d0_hbm, d1_hbm, x1_ref, route_ref, fg_ref, y_hbm, o_ref,
                  s0, s1, b0, b1, isem, sem, *, tm):
    _load_dest(pl.program_id(0), d0_hbm, d1_hbm, s0, s1, isem)

    def issue(t, carry):
        pltpu.make_async_copy(y_hbm.at[pl.ds(s0[t], 1)], b0.at[pl.ds(t, 1)], sem).start()
        pltpu.make_async_copy(y_hbm.at[pl.ds(s1[t], 1)], b1.at[pl.ds(t, 1)], sem).start()
        return carry

    lax.fori_loop(0, tm, issue, 0, unroll=8)
    pltpu.make_async_copy(y_hbm.at[pl.ds(0, tm)], b0, sem).wait()
    pltpu.make_async_copy(y_hbm.at[pl.ds(0, tm)], b1, sem).wait()

    route = route_ref[...]
    g0 = route[:, COL_G0:COL_G0 + 1]
    g1 = route[:, COL_G1:COL_G1 + 1]
    x = x1_ref[...] + (g0 * b0[...] + g1 * b1[...])
    o_ref[...] = _rms(x, fg_ref[...])


def _combine(d0, d1, x1, route, fg, y):
    T = x1.shape[0]
    tm = TOKEN_TILE
    return pl.pallas_call(
        functools.partial(_combine_body, tm=tm),
        grid=(T // tm,),
        in_specs=[
            pl.BlockSpec(memory_space=pl.ANY),
            pl.BlockSpec(memory_space=pl.ANY),
            pl.BlockSpec((tm, D_MODEL), lambda i: (i, 0)),
            pl.BlockSpec((tm, LANES), lambda i: (i, 0)),
            pl.BlockSpec((1, D_MODEL), lambda i: (0, 0)),
            pl.BlockSpec(memory_space=pl.ANY),
        ],
        out_specs=pl.BlockSpec((tm, D_MODEL), lambda i: (i, 0)),
        out_shape=jax.ShapeDtypeStruct((T, D_MODEL), F32),
        scratch_shapes=[
            pltpu.SMEM((tm,), I32),
            pltpu.SMEM((tm,), I32),
            pltpu.VMEM((tm, D_MODEL), F32),
            pltpu.VMEM((tm, D_MODEL), F32),
            pltpu.SemaphoreType.DMA((2,)),
            pltpu.SemaphoreType.DMA(()),
        ],
        compiler_params=pltpu.CompilerParams(
            dimension_semantics=("arbitrary",), vmem_limit_bytes=VMEM_LIMIT),
        name="moe_combine",
    )(d0.reshape(T // tm, tm), d1.reshape(T // tm, tm), x1, route, fg, y)


def _trunk(x, p):
    B, L, _ = x.shape
    T = B * L
    x2 = x.reshape(T, D_MODEL)
    qkv, uc = _in_proj(x2, p["norm1_g"], p["wqkv"], p["wconv"])
    a = _attention(qkv.reshape(B, L, 3 * D_ATTN), p["na_bias"])
    c = _conv(uc.reshape(B, L, 2 * D_CONV), p["conv_w"], p["conv_b"], p["conv_ln_g"],
              p["conv_ln_b"], p["conv_out_g"])
    x1, h2, route, route_t, cnt = _out_proj(
        x2, a.reshape(T, D_ATTN), c.reshape(T, D_CONV), p["attn_out_g"], p["wout_a"],
        p["wout_c"], p["norm2_g"], p["w_router"], p["b_router"])

    blk = EXPERT_BLOCK
    e0 = route_t[COL_E0].astype(I32)
    e1 = route_t[COL_E1].astype(I32)
    counts = cnt[0, ROUTER_FIRST_EXPERT_LANE:ROUTER_FIRST_EXPERT_LANE + N_EXPERTS].astype(I32)
    padded = (counts + blk - 1) // blk * blk
    pad_end = jnp.cumsum(padded)
    pad_start = pad_end - padded
    experts = jnp.arange(N_EXPERTS, dtype=I32)

    def slot(e, rank):
        return jnp.sum(jnp.where(e[:, None] == experts[None, :], pad_start[None, :], 0), axis=1) + rank

    d0 = slot(e0, route_t[COL_R0].astype(I32))
    d1 = slot(e1, route_t[COL_R1].astype(I32))
    n_blocks = 2 * T // blk + N_EXPERTS
    block_start = jnp.arange(n_blocks, dtype=I32) * blk
    block_e = jnp.minimum(jnp.sum(block_start[:, None] >= pad_end[None, :], axis=1),
                          N_EXPERTS - 1).astype(I32)
    n_used = (pad_end[-1:] // blk).astype(I32)
    last_blk = jnp.where(padded > 0, pad_end // blk - 1, -1).astype(I32)
    last_blk = jnp.concatenate([last_blk, n_used])

    xb = _dispatch(last_blk, d0, d1, h2, n_blocks * blk)
    y = _experts(block_e, n_used, xb, p["w_gate"], p["w_up"], p["w_down"])
    out = _combine(d0, d1, x1, route, p["final_g"], y)
    return out.reshape(B, L, D_MODEL)


def kernel(x_prompt, x_sample, norm1_g, w_in, rpb, attn_out_g, conv_w, conv_b, conv_ln_g,
           conv_ln_b, conv_out_g, w_out, norm2_g, w_group, b_group, w_expert, b_expert,
           w_e_gate, w_e_up, w_e_down, final_g):
    l = 0
    w_router = jnp.concatenate(
        [w_group[l], w_expert[l].transpose(1, 0, 2).reshape(D_MODEL, N_EXPERTS)], axis=1)
    w_router = jnp.pad(w_router, ((0, 0), (0, LANES - w_router.shape[1])))
    w_router_hi = w_router.astype(BF16)
    w_router_lo = (w_router - w_router_hi.astype(F32)).astype(BF16)
    w_router = jnp.concatenate([w_router_hi, w_router_lo], axis=1)
    b_router = jnp.concatenate([b_group[l], b_expert[l].reshape(N_EXPERTS)])
    b_router = jnp.pad(b_router, (0, LANES - b_router.shape[0])).reshape(1, LANES)
    p = {
        "norm1_g": norm1_g[l].reshape(1, D_MODEL),
        "wqkv": w_in[l][:, :3 * D_ATTN].astype(BF16),
        "wconv": w_in[l][:, 3 * D_ATTN:].astype(BF16),
        "na_bias": _na_bias_table(rpb[l]),
        "attn_out_g": attn_out_g[l].reshape(1, D_ATTN),
        "conv_w": jnp.pad(conv_w[l], ((0, 1), (0, 0))),
        "conv_b": conv_b[l].reshape(1, D_CONV),
        "conv_ln_g": conv_ln_g[l].reshape(1, D_CONV),
        "conv_ln_b": conv_ln_b[l].reshape(1, D_CONV),
        "conv_out_g": conv_out_g[l].reshape(1, D_CONV),
        "wout_a": w_out[l][:D_ATTN].astype(BF16),
        "wout_c": w_out[l][D_ATTN:].astype(BF16),
        "norm2_g": norm2_g[l].reshape(1, D_MODEL),
        "w_router": w_router,
        "b_router": b_router,
        "w_gate": w_e_gate[l].astype(BF16),
        "w_up": w_e_up[l].astype(BF16),
        "w_down": w_e_down[l].astype(BF16),
        "final_g": final_g.reshape(1, D_MODEL),
    }
    return (_trunk(x_prompt, p), _trunk(x_sample, p))
```

```python
import functools

import jax
import jax.numpy as jnp
from jax import lax
from jax.experimental import pallas as pl
from jax.experimental.pallas import tpu as pltpu

F32 = jnp.float32
BF16 = jnp.bfloat16
I32 = jnp.int32

D_MODEL = 1024
GRID_W = 64
D_ATTN = 512
D_CONV = 512
HEAD_DIM = 64
N_HEADS = 8
NA_ROWS = 8
NA_COLS = 16
CONV_WIDTH = 31
N_GROUPS = 4
EPG = 8
N_EXPERTS = 32
D_EXPERT = 512
EPS = 1e-6

LANES = 128
HEAD_PAIRS = N_HEADS * HEAD_DIM // LANES
NA_KEYS = NA_ROWS * GRID_W
MASKED = -1e30
TOKEN_TILE = 512
EXPERT_BLOCK = 512
CONV_TILE = 64
CONV_SEGMENT = 512
CONV_HALO = 16
VMEM_LIMIT = 56 * 1024 * 1024


def _rms(x, g):
    return x * lax.rsqrt(jnp.mean(x * x, axis=-1, keepdims=True) + EPS) * g


def _in_proj_body(x_ref, g_ref, wqkv_ref, wc_ref, qkv_ref, uc_ref):
    h = _rms(x_ref[...], g_ref[...]).astype(BF16)
    qkv_ref[...] = jnp.dot(h, wqkv_ref[...], preferred_element_type=F32).astype(BF16)
    uc_ref[...] = jnp.dot(h, wc_ref[...], preferred_element_type=F32).astype(BF16)


def _in_proj(x2, g, wqkv, wc):
    T = x2.shape[0]
    tm = TOKEN_TILE
    return pl.pallas_call(
        _in_proj_body,
        grid=(T // tm,),
        in_specs=[
            pl.BlockSpec((tm, D_MODEL), lambda i: (i, 0)),
            pl.BlockSpec((1, D_MODEL), lambda i: (0, 0)),
            pl.BlockSpec((D_MODEL, 3 * D_ATTN), lambda i: (0, 0)),
            pl.BlockSpec((D_MODEL, 2 * D_CONV), lambda i: (0, 0)),
        ],
        out_specs=[
            pl.BlockSpec((tm, 3 * D_ATTN), lambda i: (i, 0)),
            pl.BlockSpec((tm, 2 * D_CONV), lambda i: (i, 0)),
        ],
        out_shape=[
            jax.ShapeDtypeStruct((T, 3 * D_ATTN), BF16),
            jax.ShapeDtypeStruct((T, 2 * D_CONV), BF16),
        ],
        compiler_params=pltpu.CompilerParams(
            dimension_semantics=("arbitrary",), vmem_limit_bytes=VMEM_LIMIT),
        name="in_proj",
    )(x2, g, wqkv, wc)


def _na_bias_table(rpb):
    c = jnp.arange(GRID_W)
    col_start = jnp.clip(c - NA_COLS // 2, 0, GRID_W - NA_COLS)
    cp = jnp.arange(GRID_W)
    valid = (cp[None, :] >= col_start[:, None]) & (cp[None, :] < col_start[:, None] + NA_COLS)
    col_off = cp[None, :] - c[:, None] + (NA_COLS - 1)
    sel = (col_off[None] == jnp.arange(2 * NA_COLS - 1)[:, None, None]) & valid[None]
    a = jnp.einsum("hrd,dcx->hrcx", rpb, sel.astype(F32), precision=lax.Precision.HIGHEST)
    a = jnp.where(valid[None, None], a, MASKED)
    t = jnp.stack([a[:, NA_ROWS - 1 - p:2 * NA_ROWS - 1 - p] for p in range(NA_ROWS)], axis=1)
    t = t.transpose(0, 1, 3, 2, 4)
    t = t.reshape(HEAD_PAIRS, 2, NA_ROWS, GRID_W, NA_KEYS)
    return t.transpose(0, 2, 1, 3, 4).reshape(HEAD_PAIRS, NA_ROWS, LANES, NA_KEYS).astype(F32)


def _attn_body(q_ref, k_ref, v_ref, bias_ref, o_ref, s_ref, p_ref, l_ref, *, rows, rows_per_step):
    rb = pl.program_id(2)
    first_head = lax.broadcasted_iota(I32, (GRID_W, LANES), 1) < HEAD_DIM
    zero = jnp.zeros((GRID_W, LANES), BF16)

    def window(rr):
        r = rb * rows_per_step + rr
        r_start = jnp.clip(r - NA_ROWS // 2, 0, rows - NA_ROWS)
        return r - r_start, pl.multiple_of(r_start * GRID_W, GRID_W)

    for rr in range(rows_per_step):
        p, k0 = window(rr)
        q = q_ref[rr * GRID_W:(rr + 1) * GRID_W, :] * jnp.asarray(HEAD_DIM ** -0.5, BF16)
        qbd = jnp.concatenate([jnp.where(first_head, q, zero), jnp.where(first_head, zero, q)], axis=0)
        ks = k_ref[pl.ds(k0, NA_KEYS), :]
        s = lax.dot_general(qbd, ks, (((1,), (1,)), ((), ())), preferred_element_type=F32)
        s_ref[rr] = s + bias_ref[p]
    for rr in range(rows_per_step):
        s = s_ref[rr]
        e = jnp.exp(s - jnp.max(s, axis=-1, keepdims=True))
        l_ref[rr] = jnp.broadcast_to(1.0 / jnp.sum(e, axis=-1, keepdims=True), (LANES, LANES))
        p_ref[rr] = e.astype(BF16)
    for rr in range(rows_per_step):
        _, k0 = window(rr)
        vs = v_ref[pl.ds(k0, NA_KEYS), :]
        o = jnp.dot(p_ref[rr], vs, preferred_element_type=F32) * l_ref[rr]
        out = jnp.where(first_head, o[:GRID_W], o[GRID_W:])
        o_ref[rr * GRID_W:(rr + 1) * GRID_W, :] = out.astype(BF16)


def _attention(qkv, bias):
    B, L, _ = qkv.shape
    rows = L // GRID_W
    rows_per_step = 8
    tq = rows_per_step * GRID_W
    body = functools.partial(_attn_body, rows=rows, rows_per_step=rows_per_step)
    return pl.pallas_call(
        body,
        grid=(HEAD_PAIRS, B, L // tq),
        in_specs=[
            pl.BlockSpec((None, tq, LANES), lambda hp, b, rb: (b, rb, hp)),
            pl.BlockSpec((None, L, LANES), lambda hp, b, rb: (b, 0, HEAD_PAIRS + hp)),
            pl.BlockSpec((None, L, LANES), lambda hp, b, rb: (b, 0, 2 * HEAD_PAIRS + hp)),
            pl.BlockSpec((None, NA_ROWS, LANES, NA_KEYS), lambda hp, b, rb: (hp, 0, 0, 0)),
        ],
        out_specs=pl.BlockSpec((None, tq, LANES), lambda hp, b, rb: (b, rb, hp)),
        out_shape=jax.ShapeDtypeStruct((B, L, D_ATTN), BF16),
        scratch_shapes=[
            pltpu.VMEM((rows_per_step, LANES, NA_KEYS), F32),
            pltpu.VMEM((rows_per_step, LANES, NA_KEYS), BF16),
            pltpu.VMEM((rows_per_step, LANES, LANES), F32),
        ],
        compiler_params=pltpu.CompilerParams(
            dimension_semantics=("arbitrary", "arbitrary", "arbitrary"),
            vmem_limit_bytes=VMEM_LIMIT),
        name="na_attention",
    )(qkv, qkv, qkv, bias)


def _glu(uc_rows):
    a = uc_rows[:, 0:D_CONV].astype(F32)
    gt = uc_rows[:, D_CONV:2 * D_CONV].astype(F32)
    return a * jax.nn.sigmoid(gt)


def _conv_body(uc_ref, w_ref, b_ref, lng_ref, lnb_ref, og_ref, o_ref, zs_ref, cbuf_ref, *, seg):
    sg = pl.program_id(1)
    n_seg = pl.num_programs(1)
    seg0 = pl.multiple_of(sg * seg, seg)
    halo = CONV_HALO
    n_lane_tiles = D_CONV // LANES

    def put_z(u0, n, z):
        for j in range(n_lane_tiles):
            zs_ref[0, j, pl.ds(u0, n), :] = z[:, j * LANES:(j + 1) * LANES]

    glu_rows = 128

    def glu(i, carry):
        u0 = pl.multiple_of(i * glu_rows, glu_rows)
        put_z(halo + u0, glu_rows, _glu(uc_ref[pl.ds(seg0 + u0, glu_rows), :]))
        return carry

    lax.fori_loop(0, seg // glu_rows, glu, 0)

    @pl.when(sg > 0)
    def _():
        put_z(0, halo, _glu(uc_ref[pl.ds(seg0 - halo, halo), :]))

    @pl.when(sg == 0)
    def _():
        put_z(0, halo, jnp.zeros((halo, D_CONV), F32))

    @pl.when(sg < n_seg - 1)
    def _():
        put_z(halo + seg, halo, _glu(uc_ref[pl.ds(seg0 + seg, halo), :]))

    @pl.when(sg == n_seg - 1)
    def _():
        put_z(halo + seg, halo, jnp.zeros((halo, D_CONV), F32))

    def shift_rows(j, u0, n):
        x = zs_ref[0, j, pl.ds(u0, n + 8), :]
        for s in range(1, 8):
            zs_ref[s, j, pl.ds(u0, n), :] = x[s:s + n]

    shift_chunk = 64
    chunks = seg // shift_chunk

    def shift(i, carry):
        shift_rows(i // chunks, pl.multiple_of((i % chunks) * shift_chunk, shift_chunk), shift_chunk)
        return carry

    lax.fori_loop(0, n_lane_tiles * chunks, shift, 0)
    for j in range(n_lane_tiles):
        shift_rows(j, seg, 2 * halo - 8)

    tt = CONV_TILE
    tiles = seg // tt
    first = halo - CONV_WIDTH // 2

    def taps(i, carry):
        j = i // tiles
        t0 = pl.multiple_of((i % tiles) * tt, tt)
        acc = jnp.broadcast_to(b_ref[j], (tt, LANES))
        for k in range(CONV_WIDTH):
            off = k + first
            acc = acc + w_ref[j, k:k + 1, :] * zs_ref[off % 8, j, pl.ds(t0 + 8 * (off // 8), tt), :]
        cbuf_ref[j, pl.ds(t0, tt), :] = acc
        return carry

    lax.fori_loop(0, n_lane_tiles * tiles, taps, 0, unroll=2)

    norm_rows = 128

    def norm(i, carry):
        t0 = pl.multiple_of(i * norm_rows, norm_rows)
        y = jnp.concatenate([cbuf_ref[j, pl.ds(t0, norm_rows), :] for j in range(n_lane_tiles)], axis=1)
        mu = jnp.mean(y, axis=-1, keepdims=True)
        yc = y - mu
        var = jnp.mean(yc * yc, axis=-1, keepdims=True)
        yn = yc * lax.rsqrt(var + EPS) * lng_ref[...] + lnb_ref[...]
        sw = yn * jax.nn.sigmoid(yn)
        o_ref[pl.ds(t0, norm_rows), :] = _rms(sw, og_ref[...]).astype(BF16)
        return carry

    lax.fori_loop(0, seg // norm_rows, norm, 0, unroll=2)


def _conv(uc, w, b, lng, lnb, og):
    B, L, _ = uc.shape
    seg = CONV_SEGMENT
    n_lane_tiles = D_CONV // LANES
    vec = pl.BlockSpec((1, D_CONV), lambda i, s: (0, 0))
    w = w.reshape(CONV_WIDTH + 1, n_lane_tiles, LANES).transpose(1, 0, 2)
    b = b.reshape(n_lane_tiles, 1, LANES)
    return pl.pallas_call(
        functools.partial(_conv_body, seg=seg),
        grid=(B, L // seg),
        in_specs=[
            pl.BlockSpec((None, L, 2 * D_CONV), lambda i, s: (i, 0, 0)),
            pl.BlockSpec((n_lane_tiles, CONV_WIDTH + 1, LANES), lambda i, s: (0, 0, 0)),
            pl.BlockSpec((n_lane_tiles, 1, LANES), lambda i, s: (0, 0, 0)),
            vec, vec, vec,
        ],
        out_specs=pl.BlockSpec((None, seg, D_CONV), lambda i, s: (i, s, 0)),
        out_shape=jax.ShapeDtypeStruct((B, L, D_CONV), BF16),
        scratch_shapes=[
            pltpu.VMEM((8, n_lane_tiles, seg + 2 * CONV_HALO, LANES), F32),
            pltpu.VMEM((n_lane_tiles, seg, LANES), F32),
        ],
        compiler_params=pltpu.CompilerParams(
            dimension_semantics=("arbitrary", "arbitrary"), vmem_limit_bytes=VMEM_LIMIT),
        name="conformer_conv",
    )(uc, w, b, lng, lnb, og)


ROUTER_FIRST_EXPERT_LANE = N_GROUPS
COL_E0, COL_E1, COL_R0, COL_R1, COL_G0, COL_G1 = range(6)
ROUTE_ROWS = 8


def _out_proj_body(x_ref, a_ref, c_ref, ag_ref, wa_ref, wc_ref, n2g_ref, wr_ref, br_ref,
                   x1_ref, h2_ref, route_ref, route_t_ref, cnt_ref, carry_ref, *, tm):
    @pl.when(pl.program_id(0) == 0)
    def _():
        carry_ref[...] = jnp.zeros_like(carry_ref)

    an = _rms(a_ref[...].astype(F32), ag_ref[...]).astype(BF16)
    y = jnp.dot(an, wa_ref[...], preferred_element_type=F32)
    y = y + jnp.dot(c_ref[...], wc_ref[...], preferred_element_type=F32)
    x1 = x_ref[...] + y
    x1_ref[...] = x1
    h2 = _rms(x1, n2g_ref[...])
    h2_ref[...] = h2

    h_hi = h2.astype(BF16)
    h_lo = (h2 - h_hi.astype(F32)).astype(BF16)
    split = (jnp.dot(h_hi, wr_ref[...], preferred_element_type=F32)
             + jnp.dot(h_lo, wr_ref[...], preferred_element_type=F32))
    logits = split[:, :LANES] + split[:, LANES:] + br_ref[...]
    lane = lax.broadcasted_iota(I32, (tm, LANES), 1).astype(F32)
    ninf = jnp.full((tm, LANES), -jnp.inf, F32)
    big = jnp.full((tm, LANES), float(LANES), F32)

    def first_argmax(vals):
        top = jnp.max(vals, axis=-1, keepdims=True)
        idx = jnp.min(jnp.where(vals == top, lane, big), axis=-1, keepdims=True)
        return top, idx

    is_group = lane < N_GROUPS
    gtop, gsel = first_argmax(jnp.where(is_group, logits, ninf))
    p_g = 1.0 / jnp.sum(jnp.where(is_group, jnp.exp(logits - gtop), 0.0), axis=-1, keepdims=True)
    lo = ROUTER_FIRST_EXPERT_LANE + EPG * gsel
    el = jnp.where((lane >= lo) & (lane < lo + EPG), logits, ninf)
    v0, i0 = first_argmax(el)
    v1, i1 = first_argmax(jnp.where(lane == i0, ninf, el))
    t = jnp.exp(v1 - v0)
    g0 = p_g / (1.0 + t)
    g1 = p_g * t / (1.0 + t)

    onehot = jnp.where((lane == i0) | (lane == i1), 1.0, 0.0)
    ri = lax.broadcasted_iota(I32, (tm, tm), 0)
    ci = lax.broadcasted_iota(I32, (tm, tm), 1)
    tri = jnp.where(ci <= ri, 1.0, 0.0).astype(BF16)
    cum = jnp.dot(tri, onehot.astype(BF16), preferred_element_type=F32) + carry_ref[0:1, :]
    r0 = jnp.sum(jnp.where(lane == i0, cum, 0.0), axis=-1, keepdims=True) - 1.0
    r1 = jnp.sum(jnp.where(lane == i1, cum, 0.0), axis=-1, keepdims=True) - 1.0
    carry = carry_ref[0:1, :] + jnp.sum(onehot, axis=0, keepdims=True)
    carry_ref[...] = jnp.broadcast_to(carry, carry_ref.shape)
    cnt_ref[...] = jnp.broadcast_to(carry, cnt_ref.shape)

    route = jnp.zeros((tm, LANES), F32)
    for col, val in ((COL_E0, i0 - ROUTER_FIRST_EXPERT_LANE), (COL_E1, i1 - ROUTER_FIRST_EXPERT_LANE),
                     (COL_R0, r0), (COL_R1, r1), (COL_G0, g0), (COL_G1, g1)):
        route = jnp.where(lane == col, val, route)
    route_ref[...] = route
    route_t_ref[...] = route.T[0:ROUTE_ROWS, :]


def _out_proj(x2, a2, c2, ag, wa, wc, n2g, wr, br):
    T = x2.shape[0]
    tm = TOKEN_TILE
    full = lambda shape: pl.BlockSpec(shape, lambda i: (0,) * len(shape))
    return pl.pallas_call(
        functools.partial(_out_proj_body, tm=tm),
        grid=(T // tm,),
        in_specs=[
            pl.BlockSpec((tm, D_MODEL), lambda i: (i, 0)),
            pl.BlockSpec((tm, D_ATTN), lambda i: (i, 0)),
            pl.BlockSpec((tm, D_CONV), lambda i: (i, 0)),
            full((1, D_ATTN)),
            full((D_ATTN, D_MODEL)),
            full((D_CONV, D_MODEL)),
            full((1, D_MODEL)),
            full((D_MODEL, 2 * LANES)),
            full((1, LANES)),
        ],
        out_specs=[
            pl.BlockSpec((tm, D_MODEL), lambda i: (i, 0)),
            pl.BlockSpec((tm, D_MODEL), lambda i: (i, 0)),
            pl.BlockSpec((tm, LANES), lambda i: (i, 0)),
            pl.BlockSpec((ROUTE_ROWS, tm), lambda i: (0, i)),
            full((8, LANES)),
        ],
        out_shape=[
            jax.ShapeDtypeStruct((T, D_MODEL), F32),
            jax.ShapeDtypeStruct((T, D_MODEL), F32),
            jax.ShapeDtypeStruct((T, LANES), F32),
            jax.ShapeDtypeStruct((ROUTE_ROWS, T), F32),
            jax.ShapeDtypeStruct((8, LANES), F32),
        ],
        scratch_shapes=[pltpu.VMEM((8, LANES), F32)],
        compiler_params=pltpu.CompilerParams(
            dimension_semantics=("arbitrary",), vmem_limit_bytes=VMEM_LIMIT),
        name="out_proj_router",
    )(x2, a2, c2, ag, wa, wc, n2g, wr, br)


def _load_dest(i, d0_hbm, d1_hbm, s0, s1, isem):
    c0 = pltpu.make_async_copy(d0_hbm.at[i], s0, isem.at[0])
    c1 = pltpu.make_async_copy(d1_hbm.at[i], s1, isem.at[1])
    c0.start()
    c1.start()
    c0.wait()
    c1.wait()


def _dispatch_body(lb_ref, d0_hbm, d1_hbm, h_ref, xb_out, s0, s1, zero_ref, isem, sem, *, tm):
    blk = EXPERT_BLOCK
    n_blocks = xb_out.shape[0] // blk

    @pl.when(pl.program_id(0) == 0)
    def _():
        zero_ref[...] = jnp.zeros_like(zero_ref)
        n_used = lb_ref[N_EXPERTS]

        def block(b):
            return xb_out.at[pl.ds(pl.multiple_of(b * blk, blk), blk)]

        def fill(e, carry):
            @pl.when(lb_ref[e] >= 0)
            def _():
                pltpu.make_async_copy(zero_ref, block(lb_ref[e]), sem).start()
            return carry

        def drain(e, carry):
            @pl.when(lb_ref[e] >= 0)
            def _():
                pltpu.make_async_copy(zero_ref, block(lb_ref[e]), sem).wait()
            return carry

        def fill_tail(b, carry):
            pltpu.make_async_copy(zero_ref, block(b), sem).start()
            return carry

        def drain_tail(b, carry):
            pltpu.make_async_copy(zero_ref, block(b), sem).wait()
            return carry

        lax.fori_loop(0, N_EXPERTS, fill, 0)
        lax.fori_loop(n_used, n_blocks, fill_tail, 0)
        lax.fori_loop(0, N_EXPERTS, drain, 0)
        lax.fori_loop(n_used, n_blocks, drain_tail, 0)

    _load_dest(pl.program_id(0), d0_hbm, d1_hbm, s0, s1, isem)

    def issue(t, carry):
        row = h_ref.at[pl.ds(t, 1)]
        pltpu.make_async_copy(row, xb_out.at[pl.ds(s0[t], 1)], sem).start()
        pltpu.make_async_copy(row, xb_out.at[pl.ds(s1[t], 1)], sem).start()
        return carry

    lax.fori_loop(0, tm, issue, 0, unroll=8)
    for _ in range(2):
        pltpu.make_async_copy(h_ref, xb_out.at[pl.ds(0, tm)], sem).wait()


def _dispatch(last_blk, d0, d1, h2, n_slots):
    T = h2.shape[0]
    tm = TOKEN_TILE
    grid_spec = pltpu.PrefetchScalarGridSpec(
        num_scalar_prefetch=1,
        grid=(T // tm,),
        in_specs=[
            pl.BlockSpec(memory_space=pl.ANY),
            pl.BlockSpec(memory_space=pl.ANY),
            pl.BlockSpec((tm, D_MODEL), lambda i, lb: (i, 0)),
        ],
        out_specs=pl.BlockSpec(memory_space=pl.ANY),
        scratch_shapes=[
            pltpu.SMEM((tm,), I32),
            pltpu.SMEM((tm,), I32),
            pltpu.VMEM((EXPERT_BLOCK, D_MODEL), F32),
            pltpu.SemaphoreType.DMA((2,)),
            pltpu.SemaphoreType.DMA(()),
        ],
    )
    return pl.pallas_call(
        functools.partial(_dispatch_body, tm=tm),
        grid_spec=grid_spec,
        out_shape=jax.ShapeDtypeStruct((n_slots, D_MODEL), F32),
        compiler_params=pltpu.CompilerParams(
            dimension_semantics=("arbitrary",), vmem_limit_bytes=VMEM_LIMIT,
            has_side_effects=True),
        name="moe_dispatch",
    )(last_blk, d0.reshape(T // tm, tm), d1.reshape(T // tm, tm), h2)


def _expert_body(be_ref, nu_ref, xb_ref, wg_ref, wu_ref, wd_ref, y_ref):
    del be_ref
    used = pl.program_id(0) < nu_ref[0]

    @pl.when(used)
    def _():
        x = xb_ref[...].astype(BF16)
        g = jnp.dot(x, wg_ref[...], preferred_element_type=F32)
        u = jnp.dot(x, wu_ref[...], preferred_element_type=F32)
        hid = (g * jax.nn.sigmoid(g) * u).astype(BF16)
        y_ref[...] = jnp.dot(hid, wd_ref[...], preferred_element_type=F32)

    @pl.when(jnp.logical_not(used))
    def _():
        y_ref[...] = jnp.zeros_like(y_ref)


def _experts(block_e, n_used, xb, wg, wu, wd):
    n_slots = xb.shape[0]
    blk = EXPERT_BLOCK
    grid_spec = pltpu.PrefetchScalarGridSpec(
        num_scalar_prefetch=2,
        grid=(n_slots // blk,),
        in_specs=[
            pl.BlockSpec((blk, D_MODEL), lambda b, be, nu: (b, 0)),
            pl.BlockSpec((None, D_MODEL, D_EXPERT), lambda b, be, nu: (be[b], 0, 0)),
            pl.BlockSpec((None, D_MODEL, D_EXPERT), lambda b, be, nu: (be[b], 0, 0)),
            pl.BlockSpec((None, D_EXPERT, D_MODEL), lambda b, be, nu: (be[b], 0, 0)),
        ],
        out_specs=pl.BlockSpec((blk, D_MODEL), lambda b, be, nu: (b, 0)),
    )
    return pl.pallas_call(
        _expert_body,
        grid_spec=grid_spec,
        out_shape=jax.ShapeDtypeStruct((n_slots, D_MODEL), F32),
        compiler_params=pltpu.CompilerParams(
            dimension_semantics=("arbitrary",), vmem_limit_bytes=VMEM_LIMIT),
        name="moe_experts",
    )(block_e, n_used, xb, wg, wu, wd)


def _combine_body(d0_hbm, d1_hbm, x1_ref, route_ref, fg_ref, y_hbm, o_ref,
                  s0, s1, b0, b1, isem, sem, *, tm):
    _load_dest(pl.program_id(0), d0_hbm, d1_hbm, s0, s1, isem)

    def issue(t, carry):
        pltpu.make_async_copy(y_hbm.at[pl.ds(s0[t], 1)], b0.at[pl.ds(t, 1)], sem).start()
        pltpu.make_async_copy(y_hbm.at[pl.ds(s1[t], 1)], b1.at[pl.ds(t, 1)], sem).start()
        return carry

    lax.fori_loop(0, tm, issue, 0, unroll=8)
    pltpu.make_async_copy(y_hbm.at[pl.ds(0, tm)], b0, sem).wait()
    pltpu.make_async_copy(y_hbm.at[pl.ds(0, tm)], b1, sem).wait()

    route = route_ref[...]
    g0 = route[:, COL_G0:COL_G0 + 1]
    g1 = route[:, COL_G1:COL_G1 + 1]
    x = x1_ref[...] + (g0 * b0[...] + g1 * b1[...])
    o_ref[...] = _rms(x, fg_ref[...])


def _combine(d0, d1, x1, route, fg, y):
    T = x1.shape[0]
    tm = TOKEN_TILE
    return pl.pallas_call(
        functools.partial(_combine_body, tm=tm),
        grid=(T // tm,),
        in_specs=[
            pl.BlockSpec(memory_space=pl.ANY),
            pl.BlockSpec(memory_space=pl.ANY),
            pl.BlockSpec((tm, D_MODEL), lambda i: (i, 0)),
            pl.BlockSpec((tm, LANES), lambda i: (i, 0)),
            pl.BlockSpec((1, D_MODEL), lambda i: (0, 0)),
            pl.BlockSpec(memory_space=pl.ANY),
        ],
        out_specs=pl.BlockSpec((tm, D_MODEL), lambda i: (i, 0)),
        out_shape=jax.ShapeDtypeStruct((T, D_MODEL), F32),
        scratch_shapes=[
            pltpu.SMEM((tm,), I32),
            pltpu.SMEM((tm,), I32),
            pltpu.VMEM((tm, D_MODEL), F32),
            pltpu.VMEM((tm, D_MODEL), F32),
            pltpu.SemaphoreType.DMA((2,)),
            pltpu.SemaphoreType.DMA(()),
        ],
        compiler_params=pltpu.CompilerParams(
            dimension_semantics=("arbitrary",), vmem_limit_bytes=VMEM_LIMIT),
        name="moe_combine",
    )(d0.reshape(T // tm, tm), d1.reshape(T // tm, tm), x1, route, fg, y)


def _trunk(x, p):
    B, L, _ = x.shape
    T = B * L
    x2 = x.reshape(T, D_MODEL)
    qkv, uc = _in_proj(x2, p["norm1_g"], p["wqkv"], p["wconv"])
    a = _attention(qkv.reshape(B, L, 3 * D_ATTN), p["na_bias"])
    c = _conv(uc.reshape(B, L, 2 * D_CONV), p["conv_w"], p["conv_b"], p["conv_ln_g"],
              p["conv_ln_b"], p["conv_out_g"])
    x1, h2, route, route_t, cnt = _out_proj(
        x2, a.reshape(T, D_ATTN), c.reshape(T, D_CONV), p["attn_out_g"], p["wout_a"],
        p["wout_c"], p["norm2_g"], p["w_router"], p["b_router"])

    blk = EXPERT_BLOCK
    e0 = route_t[COL_E0].astype(I32)
    e1 = route_t[COL_E1].astype(I32)
    counts = cnt[0, ROUTER_FIRST_EXPERT_LANE:ROUTER_FIRST_EXPERT_LANE + N_EXPERTS].astype(I32)
    padded = (counts + blk - 1) // blk * blk
    pad_end = jnp.cumsum(padded)
    pad_start = pad_end - padded
    experts = jnp.arange(N_EXPERTS, dtype=I32)

    def slot(e, rank):
        return jnp.sum(jnp.where(e[:, None] == experts[None, :], pad_start[None, :], 0), axis=1) + rank

    d0 = slot(e0, route_t[COL_R0].astype(I32))
    d1 = slot(e1, route_t[COL_R1].astype(I32))
    n_blocks = 2 * T // blk + N_EXPERTS
    block_start = jnp.arange(n_blocks, dtype=I32) * blk
    block_e = jnp.minimum(jnp.sum(block_start[:, None] >= pad_end[None, :], axis=1),
                          N_EXPERTS - 1).astype(I32)
    n_used = (pad_end[-1:] // blk).astype(I32)
    last_blk = jnp.where(padded > 0, pad_end // blk - 1, -1).astype(I32)
    last_blk = jnp.concatenate([last_blk, n_used])

    xb = _dispatch(last_blk, d0, d1, h2, n_blocks * blk)
    y = _experts(block_e, n_used, xb, p["w_gate"], p["w_up"], p["w_down"])
    out = _combine(d0, d1, x1, route, p["final_g"], y)
    return out.reshape(B, L, D_MODEL)


def kernel(x_prompt, x_sample, norm1_g, w_in, rpb, attn_out_g, conv_w, conv_b, conv_ln_g,
           conv_ln_b, conv_out_g, w_out, norm2_g, w_group, b_group, w_expert, b_expert,
           w_e_gate, w_e_up, w_e_down, final_g):
    l = 0
    w_router = jnp.concatenate(
        [w_group[l], w_expert[l].transpose(1, 0, 2).reshape(D_MODEL, N_EXPERTS)], axis=1)
    w_router = jnp.pad(w_router, ((0, 0), (0, LANES - w_router.shape[1])))
    w_router_hi = w_router.astype(BF16)
    w_router_lo = (w_router - w_router_hi.astype(F32)).astype(BF16)
    w_router = jnp.concatenate([w_router_hi, w_router_lo], axis=1)
    b_router = jnp.concatenate([b_group[l], b_expert[l].reshape(N_EXPERTS)])
    b_router = jnp.pad(b_router, (0, LANES - b_router.shape[0])).reshape(1, LANES)
    p = {
        "norm1_g": norm1_g[l].reshape(1, D_MODEL),
        "wqkv": w_in[l][:, :3 * D_ATTN].astype(BF16),
        "wconv": w_in[l][:, 3 * D_ATTN:].astype(BF16),
        "na_bias": _na_bias_table(rpb[l]),
        "attn_out_g": attn_out_g[l].reshape(1, D_ATTN),
        "conv_w": jnp.pad(conv_w[l], ((0, 1), (0, 0))),
        "conv_b": conv_b[l].reshape(1, D_CONV),
        "conv_ln_g": conv_ln_g[l].reshape(1, D_CONV),
        "conv_ln_b": conv_ln_b[l].reshape(1, D_CONV),
        "conv_out_g": conv_out_g[l].reshape(1, D_CONV),
        "wout_a": w_out[l][:D_ATTN].astype(BF16),
        "wout_c": w_out[l][D_ATTN:].astype(BF16),
        "norm2_g": norm2_g[l].reshape(1, D_MODEL),
        "w_router": w_router,
        "b_router": b_router,
        "w_gate": w_e_gate[l].astype(BF16),
        "w_up": w_e_up[l].astype(BF16),
        "w_down": w_e_down[l].astype(BF16),
        "final_g": final_g.reshape(1, D_MODEL),
    }
    return (_trunk(x_prompt, p), _trunk(x_sample, p))
```

```python
import functools

import jax
import jax.numpy as jnp
from jax import lax
from jax.experimental import pallas as pl
from jax.experimental.pallas import tpu as pltpu

F32 = jnp.float32
BF16 = jnp.bfloat16
I32 = jnp.int32

D_MODEL = 1024
GRID_W = 64
D_ATTN = 512
D_CONV = 512
HEAD_DIM = 64
N_HEADS = 8
NA_ROWS = 8
NA_COLS = 16
CONV_WIDTH = 31
N_GROUPS = 4
EPG = 8
N_EXPERTS = 32
D_EXPERT = 512
EPS = 1e-6

LANES = 128
HEAD_PAIRS = N_HEADS * HEAD_DIM // LANES
NA_KEYS = NA_ROWS * GRID_W
MASKED = -1e30
TOKEN_TILE = 512
EXPERT_BLOCK = 512
CONV_TILE = 64
CONV_SEGMENT = 512
CONV_HALO = 16
VMEM_LIMIT = 56 * 1024 * 1024


def _rms(x, g):
    return x * lax.rsqrt(jnp.mean(x * x, axis=-1, keepdims=True) + EPS) * g


def _in_proj_body(x_ref, g_ref, wqkv_ref, wc_ref, qkv_ref, uc_ref):
    h = _rms(x_ref[...], g_ref[...]).astype(BF16)
    qkv_ref[...] = jnp.dot(h, wqkv_ref[...], preferred_element_type=F32).astype(BF16)
    uc_ref[...] = jnp.dot(h, wc_ref[...], preferred_element_type=F32).astype(BF16)


def _in_proj(x2, g, wqkv, wc):
    T = x2.shape[0]
    tm = TOKEN_TILE
    return pl.pallas_call(
        _in_proj_body,
        grid=(T // tm,),
        in_specs=[
            pl.BlockSpec((tm, D_MODEL), lambda i: (i, 0)),
            pl.BlockSpec((1, D_MODEL), lambda i: (0, 0)),
            pl.BlockSpec((D_MODEL, 3 * D_ATTN), lambda i: (0, 0)),
            pl.BlockSpec((D_MODEL, 2 * D_CONV), lambda i: (0, 0)),
        ],
        out_specs=[
            pl.BlockSpec((tm, 3 * D_ATTN), lambda i: (i, 0)),
            pl.BlockSpec((tm, 2 * D_CONV), lambda i: (i, 0)),
        ],
        out_shape=[
            jax.ShapeDtypeStruct((T, 3 * D_ATTN), BF16),
            jax.ShapeDtypeStruct((T, 2 * D_CONV), BF16),
        ],
        compiler_params=pltpu.CompilerParams(
            dimension_semantics=("arbitrary",), vmem_limit_bytes=VMEM_LIMIT),
        name="in_proj",
    )(x2, g, wqkv, wc)


def _na_bias_table(rpb):
    c = jnp.arange(GRID_W)
    col_start = jnp.clip(c - NA_COLS // 2, 0, GRID_W - NA_COLS)
    cp = jnp.arange(GRID_W)
    valid = (cp[None, :] >= col_start[:, None]) & (cp[None, :] < col_start[:, None] + NA_COLS)
    col_off = cp[None, :] - c[:, None] + (NA_COLS - 1)
    sel = (col_off[None] == jnp.arange(2 * NA_COLS - 1)[:, None, None]) & valid[None]
    a = jnp.einsum("hrd,dcx->hrcx", rpb, sel.astype(F32), precision=lax.Precision.HIGHEST)
    a = jnp.where(valid[None, None], a, MASKED)
    t = jnp.stack([a[:, NA_ROWS - 1 - p:2 * NA_ROWS - 1 - p] for p in range(NA_ROWS)], axis=1)
    t = t.transpose(0, 1, 3, 2, 4)
    t = t.reshape(HEAD_PAIRS, 2, NA_ROWS, GRID_W, NA_KEYS)
    return t.transpose(0, 2, 1, 3, 4).reshape(HEAD_PAIRS, NA_ROWS, LANES, NA_KEYS).astype(F32)


def _attn_body(q_ref, k_ref, v_ref, bias_ref, o_ref, s_ref, p_ref, l_ref, *, rows, rows_per_step):
    rb = pl.program_id(2)
    first_head = lax.broadcasted_iota(I32, (GRID_W, LANES), 1) < HEAD_DIM
    zero = jnp.zeros((GRID_W, LANES), BF16)

    def window(rr):
        r = rb * rows_per_step + rr
        r_start = jnp.clip(r - NA_ROWS // 2, 0, rows - NA_ROWS)
        return r - r_start, pl.multiple_of(r_start * GRID_W, GRID_W)

    for rr in range(rows_per_step):
        p, k0 = window(rr)
        q = q_ref[rr * GRID_W:(rr + 1) * GRID_W, :] * jnp.asarray(HEAD_DIM ** -0.5, BF16)
        qbd = jnp.concatenate([jnp.where(first_head, q, zero), jnp.where(first_head, zero, q)], axis=0)
        ks = k_ref[pl.ds(k0, NA_KEYS), :]
        s = lax.dot_general(qbd, ks, (((1,), (1,)), ((), ())), preferred_element_type=F32)
        s_ref[rr] = s + bias_ref[p]
    for rr in range(rows_per_step):
        s = s_ref[rr]
        e = jnp.exp(s - jnp.max(s, axis=-1, keepdims=True))
        l_ref[rr] = jnp.broadcast_to(1.0 / jnp.sum(e, axis=-1, keepdims=True), (LANES, LANES))
        p_ref[rr] = e.astype(BF16)
    for rr in range(rows_per_step):
        _, k0 = window(rr)
        vs = v_ref[pl.ds(k0, NA_KEYS), :]
        o = jnp.dot(p_ref[rr], vs, preferred_element_type=F32) * l_ref[rr]
        out = jnp.where(first_head, o[:GRID_W], o[GRID_W:])
        o_ref[rr * GRID_W:(rr + 1) * GRID_W, :] = out.astype(BF16)


def _attention(qkv, bias):
    B, L, _ = qkv.shape
    rows = L // GRID_W
    rows_per_step = 8
    tq = rows_per_step * GRID_W
    body = functools.partial(_attn_body, rows=rows, rows_per_step=rows_per_step)
    return pl.pallas_call(
        body,
        grid=(HEAD_PAIRS, B, L // tq),
        in_specs=[
            pl.BlockSpec((None, tq, LANES), lambda hp, b, rb: (b, rb, hp)),
            pl.BlockSpec((None, L, LANES), lambda hp, b, rb: (b, 0, HEAD_PAIRS + hp)),
            pl.BlockSpec((None, L, LANES), lambda hp, b, rb: (b, 0, 2 * HEAD_PAIRS + hp)),
            pl.BlockSpec((None, NA_ROWS, LANES, NA_KEYS), lambda hp, b, rb: (hp, 0, 0, 0)),
        ],
        out_specs=pl.BlockSpec((None, tq, LANES), lambda hp, b, rb: (b, rb, hp)),
        out_shape=jax.ShapeDtypeStruct((B, L, D_ATTN), BF16),
        scratch_shapes=[
            pltpu.VMEM((rows_per_step, LANES, NA_KEYS), F32),
            pltpu.VMEM((rows_per_step, LANES, NA_KEYS), BF16),
            pltpu.VMEM((rows_per_step, LANES, LANES), F32),
        ],
        compiler_params=pltpu.CompilerParams(
            dimension_semantics=("arbitrary", "arbitrary", "arbitrary"),
            vmem_limit_bytes=VMEM_LIMIT),
        name="na_attention",
    )(qkv, qkv, qkv, bias)


def _glu(uc_rows):
    a = uc_rows[:, 0:D_CONV].astype(F32)
    gt = uc_rows[:, D_CONV:2 * D_CONV].astype(F32)
    return a * jax.nn.sigmoid(gt)


def _conv_body(uc_ref, w_ref, b_ref, lng_ref, lnb_ref, og_ref, o_ref, zs_ref, cbuf_ref, *, seg):
    sg = pl.program_id(1)
    n_seg = pl.num_programs(1)
    seg0 = pl.multiple_of(sg * seg, seg)
    halo = CONV_HALO
    n_lane_tiles = D_CONV // LANES

    def put_z(u0, n, z):
        for j in range(n_lane_tiles):
            zs_ref[0, j, pl.ds(u0, n), :] = z[:, j * LANES:(j + 1) * LANES]

    glu_rows = 128

    def glu(i, carry):
        u0 = pl.multiple_of(i * glu_rows, glu_rows)
        put_z(halo + u0, glu_rows, _glu(uc_ref[pl.ds(seg0 + u0, glu_rows), :]))
        return carry

    lax.fori_loop(0, seg // glu_rows, glu, 0)

    @pl.when(sg > 0)
    def _():
        put_z(0, halo, _glu(uc_ref[pl.ds(seg0 - halo, halo), :]))

    @pl.when(sg == 0)
    def _():
        put_z(0, halo, jnp.zeros((halo, D_CONV), F32))

    @pl.when(sg < n_seg - 1)
    def _():
        put_z(halo + seg, halo, _glu(uc_ref[pl.ds(seg0 + seg, halo), :]))

    @pl.when(sg == n_seg - 1)
    def _():
        put_z(halo + seg, halo, jnp.zeros((halo, D_CONV), F32))

    def shift_rows(j, u0, n):
        x = zs_ref[0, j, pl.ds(u0, n + 8), :]
        for s in range(1, 8):
            zs_ref[s, j, pl.ds(u0, n), :] = x[s:s + n]

    shift_chunk = 64
    chunks = seg // shift_chunk

    def shift(i, carry):
        shift_rows(i // chunks, pl.multiple_of((i % chunks) * shift_chunk, shift_chunk), shift_chunk)
        return carry

    lax.fori_loop(0, n_lane_tiles * chunks, shift, 0)
    for j in range(n_lane_tiles):
        shift_rows(j, seg, 2 * halo - 8)

    tt = CONV_TILE
    tiles = seg // tt
    first = halo - CONV_WIDTH // 2

    def taps(i, carry):
        j = i // tiles
        t0 = pl.multiple_of((i % tiles) * tt, tt)
        acc = jnp.broadcast_to(b_ref[j], (tt, LANES))
        for k in range(CONV_WIDTH):
            off = k + first
            acc = acc + w_ref[j, k:k + 1, :] * zs_ref[off % 8, j, pl.ds(t0 + 8 * (off // 8), tt), :]
        cbuf_ref[j, pl.ds(t0, tt), :] = acc
        return carry

    lax.fori_loop(0, n_lane_tiles * tiles, taps, 0, unroll=2)

    norm_rows = 128

    def norm(i, carry):
        t0 = pl.multiple_of(i * norm_rows, norm_rows)
        y = jnp.concatenate([cbuf_ref[j, pl.ds(t0, norm_rows), :] for j in range(n_lane_tiles)], axis=1)
        mu = jnp.mean(y, axis=-1, keepdims=True)
        yc = y - mu
        var = jnp.mean(yc * yc, axis=-1, keepdims=True)
        yn = yc * lax.rsqrt(var + EPS) * lng_ref[...] + lnb_ref[...]
        sw = yn * jax.nn.sigmoid(yn)
        o_ref[pl.ds(t0, norm_rows), :] = _rms(sw, og_ref[...]).astype(BF16)
        return carry

    lax.fori_loop(0, seg // norm_rows, norm, 0, unroll=2)


def _conv(uc, w, b, lng, lnb, og):
    B, L, _ = uc.shape
    seg = CONV_SEGMENT
    n_lane_tiles = D_CONV // LANES
    vec = pl.BlockSpec((1, D_CONV), lambda i, s: (0, 0))
    w = w.reshape(CONV_WIDTH + 1, n_lane_tiles, LANES).transpose(1, 0, 2)
    b = b.reshape(n_lane_tiles, 1, LANES)
    return pl.pallas_call(
        functools.partial(_conv_body, seg=seg),
        grid=(B, L // seg),
        in_specs=[
            pl.BlockSpec((None, L, 2 * D_CONV), lambda i, s: (i, 0, 0)),
            pl.BlockSpec((n_lane_tiles, CONV_WIDTH + 1, LANES), lambda i, s: (0, 0, 0)),
            pl.BlockSpec((n_lane_tiles, 1, LANES), lambda i, s: (0, 0, 0)),
            vec, vec, vec,
        ],
        out_specs=pl.BlockSpec((None, seg, D_CONV), lambda i, s: (i, s, 0)),
        out_shape=jax.ShapeDtypeStruct((B, L, D_CONV), BF16),
        scratch_shapes=[
            pltpu.VMEM((8, n_lane_tiles, seg + 2 * CONV_HALO, LANES), F32),
            pltpu.VMEM((n_lane_tiles, seg, LANES), F32),
        ],
        compiler_params=pltpu.CompilerParams(
            dimension_semantics=("arbitrary", "arbitrary"), vmem_limit_bytes=VMEM_LIMIT),
        name="conformer_conv",
    )(uc, w, b, lng, lnb, og)


ROUTER_FIRST_EXPERT_LANE = N_GROUPS
COL_E0, COL_E1, COL_R0, COL_R1, COL_G0, COL_G1 = range(6)
ROUTE_ROWS = 8


def _out_proj_body(x_ref, a_ref, c_ref, ag_ref, wa_ref, wc_ref, n2g_ref, wr_ref, br_ref,
                   x1_ref, h2_ref, route_ref, route_t_ref, cnt_ref, carry_ref, *, tm):
    @pl.when(pl.program_id(0) == 0)
    def _():
        carry_ref[...] = jnp.zeros_like(carry_ref)

    an = _rms(a_ref[...].astype(F32), ag_ref[...]).astype(BF16)
    y = jnp.dot(an, wa_ref[...], preferred_element_type=F32)
    y = y + jnp.dot(c_ref[...], wc_ref[...], preferred_element_type=F32)
    x1 = x_ref[...] + y
    x1_ref[...] = x1
    h2 = _rms(x1, n2g_ref[...])
    h2_ref[...] = h2

    h_hi = h2.astype(BF16)
    h_lo = (h2 - h_hi.astype(F32)).astype(BF16)
    split = (jnp.dot(h_hi, wr_ref[...], preferred_element_type=F32)
             + jnp.dot(h_lo, wr_ref[...], preferred_element_type=F32))
    logits = split[:, :LANES] + split[:, LANES:] + br_ref[...]
    lane = lax.broadcasted_iota(I32, (tm, LANES), 1).astype(F32)
    ninf = jnp.full((tm, LANES), -jnp.inf, F32)
    big = jnp.full((tm, LANES), float(LANES), F32)

    def first_argmax(vals):
        top = jnp.max(vals, axis=-1, keepdims=True)
        idx = jnp.min(jnp.where(vals == top, lane, big), axis=-1, keepdims=True)
        return top, idx

    is_group = lane < N_GROUPS
    gtop, gsel = first_argmax(jnp.where(is_group, logits, ninf))
    p_g = 1.0 / jnp.sum(jnp.where(is_group, jnp.exp(logits - gtop), 0.0), axis=-1, keepdims=True)
    lo = ROUTER_FIRST_EXPERT_LANE + EPG * gsel
    el = jnp.where((lane >= lo) & (lane < lo + EPG), logits, ninf)
    v0, i0 = first_argmax(el)
    v1, i1 = first_argmax(jnp.where(lane == i0, ninf, el))
    t = jnp.exp(v1 - v0)
    g0 = p_g / (1.0 + t)
    g1 = p_g * t / (1.0 + t)

    onehot = jnp.where((lane == i0) | (lane == i1), 1.0, 0.0)
    ri = lax.broadcasted_iota(I32, (tm, tm), 0)
    ci = lax.broadcasted_iota(I32, (tm, tm), 1)
    tri = jnp.where(ci <= ri, 1.0, 0.0).astype(BF16)
    cum = jnp.dot(tri, onehot.astype(BF16), preferred_element_type=F32) + carry_ref[0:1, :]
    r0 = jnp.sum(jnp.where(lane == i0, cum, 0.0), axis=-1, keepdims=True) - 1.0
    r1 = jnp.sum(jnp.where(lane == i1, cum, 0.0), axis=-1, keepdims=True) - 1.0
    carry = carry_ref[0:1, :] + jnp.sum(onehot, axis=0, keepdims=True)
    carry_ref[...] = jnp.broadcast_to(carry, carry_ref.shape)
    cnt_ref[...] = jnp.broadcast_to(carry, cnt_ref.shape)

    route = jnp.zeros((tm, LANES), F32)
    for col, val in ((COL_E0, i0 - ROUTER_FIRST_EXPERT_LANE), (COL_E1, i1 - ROUTER_FIRST_EXPERT_LANE),
                     (COL_R0, r0), (COL_R1, r1), (COL_G0, g0), (COL_G1, g1)):
        route = jnp.where(lane == col, val, route)
    route_ref[...] = route
    route_t_ref[...] = route.T[0:ROUTE_ROWS, :]


def _out_proj(x2, a2, c2, ag, wa, wc, n2g, wr, br):
    T = x2.shape[0]
    tm = TOKEN_TILE
    full = lambda shape: pl.BlockSpec(shape, lambda i: (0,) * len(shape))
    return pl.pallas_call(
        functools.partial(_out_proj_body, tm=tm),
        grid=(T // tm,),
        in_specs=[
            pl.BlockSpec((tm, D_MODEL), lambda i: (i, 0)),
            pl.BlockSpec((tm, D_ATTN), lambda i: (i, 0)),
            pl.BlockSpec((tm, D_CONV), lambda i: (i, 0)),
            full((1, D_ATTN)),
            full((D_ATTN, D_MODEL)),
            full((D_CONV, D_MODEL)),
            full((1, D_MODEL)),
            full((D_MODEL, 2 * LANES)),
            full((1, LANES)),
        ],
        out_specs=[
            pl.BlockSpec((tm, D_MODEL), lambda i: (i, 0)),
            pl.BlockSpec((tm, D_MODEL), lambda i: (i, 0)),
            pl.BlockSpec((tm, LANES), lambda i: (i, 0)),
            pl.BlockSpec((ROUTE_ROWS, tm), lambda i: (0, i)),
            full((8, LANES)),
        ],
        out_shape=[
            jax.ShapeDtypeStruct((T, D_MODEL), F32),
            jax.ShapeDtypeStruct((T, D_MODEL), F32),
            jax.ShapeDtypeStruct((T, LANES), F32),
            jax.ShapeDtypeStruct((ROUTE_ROWS, T), F32),
            jax.ShapeDtypeStruct((8, LANES), F32),
        ],
        scratch_shapes=[pltpu.VMEM((8, LANES), F32)],
        compiler_params=pltpu.CompilerParams(
            dimension_semantics=("arbitrary",), vmem_limit_bytes=VMEM_LIMIT),
        name="out_proj_router",
    )(x2, a2, c2, ag, wa, wc, n2g, wr, br)


def _load_dest(i, d0_hbm, d1_hbm, s0, s1, isem):
    c0 = pltpu.make_async_copy(d0_hbm.at[i], s0, isem.at[0])
    c1 = pltpu.make_async_copy(d1_hbm.at[i], s1, isem.at[1])
    c0.start()
    c1.start()
    c0.wait()
    c1.wait()


def _dispatch_body(lb_ref, d0_hbm, d1_hbm, h_ref, xb_out, s0, s1, zero_ref, isem, sem, *, tm):
    blk = EXPERT_BLOCK
    n_blocks = xb_out.shape[0] // blk

    @pl.when(pl.program_id(0) == 0)
    def _():
        zero_ref[...] = jnp.zeros_like(zero_ref)
        n_used = lb_ref[N_EXPERTS]

        def block(b):
            return xb_out.at[pl.ds(pl.multiple_of(b * blk, blk), blk)]

        def fill(e, carry):
            @pl.when(lb_ref[e] >= 0)
            def _():
                pltpu.make_async_copy(zero_ref, block(lb_ref[e]), sem).start()
            return carry

        def drain(e, carry):
            @pl.when(lb_ref[e] >= 0)
            def _():
                pltpu.make_async_copy(zero_ref, block(lb_ref[e]), sem).wait()
            return carry

        def fill_tail(b, carry):
            pltpu.make_async_copy(zero_ref, block(b), sem).start()
            return carry

        def drain_tail(b, carry):
            pltpu.make_async_copy(zero_ref, block(b), sem).wait()
            return carry

        lax.fori_loop(0, N_EXPERTS, fill, 0)
        lax.fori_loop(n_used, n_blocks, fill_tail, 0)
        lax.fori_loop(0, N_EXPERTS, drain, 0)
        lax.fori_loop(n_used, n_blocks, drain_tail, 0)

    _load_dest(pl.program_id(0), d0_hbm, d1_hbm, s0, s1, isem)

    def issue(t, carry):
        row = h_ref.at[pl.ds(t, 1)]
        pltpu.make_async_copy(row, xb_out.at[pl.ds(s0[t], 1)], sem).start(priority=0)
        pltpu.make_async_copy(row, xb_out.at[pl.ds(s1[t], 1)], sem).start(priority=1)
        return carry

    lax.fori_loop(0, tm, issue, 0, unroll=8)
    for _ in range(2):
        pltpu.make_async_copy(h_ref, xb_out.at[pl.ds(0, tm)], sem).wait()


def _dispatch(last_blk, d0, d1, h2, n_slots):
    T = h2.shape[0]
    tm = TOKEN_TILE
    grid_spec = pltpu.PrefetchScalarGridSpec(
        num_scalar_prefetch=1,
        grid=(T // tm,),
        in_specs=[
            pl.BlockSpec(memory_space=pl.ANY),
            pl.BlockSpec(memory_space=pl.ANY),
            pl.BlockSpec((tm, D_MODEL), lambda i, lb: (i, 0)),
        ],
        out_specs=pl.BlockSpec(memory_space=pl.ANY),
        scratch_shapes=[
            pltpu.SMEM((tm,), I32),
            pltpu.SMEM((tm,), I32),
            pltpu.VMEM((EXPERT_BLOCK, D_MODEL), F32),
            pltpu.SemaphoreType.DMA((2,)),
            pltpu.SemaphoreType.DMA(()),
        ],
    )
    return pl.pallas_call(
        functools.partial(_dispatch_body, tm=tm),
        grid_spec=grid_spec,
        out_shape=jax.ShapeDtypeStruct((n_slots, D_MODEL), F32),
        compiler_params=pltpu.CompilerParams(
            dimension_semantics=("arbitrary",), vmem_limit_bytes=VMEM_LIMIT,
            has_side_effects=True),
        name="moe_dispatch",
    )(last_blk, d0.reshape(T // tm, tm), d1.reshape(T // tm, tm), h2)


def _expert_body(be_ref, nu_ref, xb_ref, wg_ref, wu_ref, wd_ref, y_ref):
    del be_ref
    used = pl.program_id(0) < nu_ref[0]

    @pl.when(used)
    def _():
        x = xb_ref[...].astype(BF16)
        g = jnp.dot(x, wg_ref[...], preferred_element_type=F32)
        u = jnp.dot(x, wu_ref[...], preferred_element_type=F32)
        hid = (g * jax.nn.sigmoid(g) * u).astype(BF16)
        y_ref[...] = jnp.dot(hid, wd_ref[...], preferred_element_type=F32)

    @pl.when(jnp.logical_not(used))
    def _():
        y_ref[...] = jnp.zeros_like(y_ref)


def _experts(block_e, n_used, xb, wg, wu, wd):
    n_slots = xb.shape[0]
    blk = EXPERT_BLOCK
    grid_spec = pltpu.PrefetchScalarGridSpec(
        num_scalar_prefetch=2,
        grid=(n_slots // blk,),
        in_specs=[
            pl.BlockSpec((blk, D_MODEL), lambda b, be, nu: (b, 0)),
            pl.BlockSpec((None, D_MODEL, D_EXPERT), lambda b, be, nu: (be[b], 0, 0)),
            pl.BlockSpec((None, D_MODEL, D_EXPERT), lambda b, be, nu: (be[b], 0, 0)),
            pl.BlockSpec((None, D_EXPERT, D_MODEL), lambda b, be, nu: (be[b], 0, 0)),
        ],
        out_specs=pl.BlockSpec((blk, D_MODEL), lambda b, be, nu: (b, 0)),
    )
    return pl.pallas_call(
        _expert_body,
        grid_spec=grid_spec,
        out_shape=jax.ShapeDtypeStruct((n_slots, D_MODEL), F32),
        compiler_params=pltpu.CompilerParams(
            dimension_semantics=("arbitrary",), vmem_limit_bytes=VMEM_LIMIT),
        name="moe_experts",
    )(block_e, n_used, xb, wg, wu, wd)


def _combine_body(d0_hbm, d1_hbm, x1_ref, route_ref, fg_ref, y_hbm, o_ref,
                  s0, s1, b0, b1, isem, sem, *, tm):
    _load_dest(pl.program_id(0), d0_hbm, d1_hbm, s0, s1, isem)

    def issue(t, carry):
        pltpu.make_async_copy(y_hbm.at[pl.ds(s0[t], 1)], b0.at[pl.ds(t, 1)], sem).start(priority=0)
        pltpu.make_async_copy(y_hbm.at[pl.ds(s1[t], 1)], b1.at[pl.ds(t, 1)], sem).start(priority=1)
        return carry

    lax.fori_loop(0, tm, issue, 0, unroll=8)
    pltpu.make_async_copy(y_hbm.at[pl.ds(0, tm)], b0, sem).wait()
    pltpu.make_async_copy(y_hbm.at[pl.ds(0, tm)], b1, sem).wait()

    route = route_ref[...]
    g0 = route[:, COL_G0:COL_G0 + 1]
    g1 = route[:, COL_G1:COL_G1 + 1]
    x = x1_ref[...] + (g0 * b0[...] + g1 * b1[...])
    o_ref[...] = _rms(x, fg_ref[...])


def _combine(d0, d1, x1, route, fg, y):
    T = x1.shape[0]
    tm = TOKEN_TILE
    return pl.pallas_call(
        functools.partial(_combine_body, tm=tm),
        grid=(T // tm,),
        in_specs=[
            pl.BlockSpec(memory_space=pl.ANY),
            pl.BlockSpec(memory_space=pl.ANY),
            pl.BlockSpec((tm, D_MODEL), lambda i: (i, 0)),
            pl.BlockSpec((tm, LANES), lambda i: (i, 0)),
            pl.BlockSpec((1, D_MODEL), lambda i: (0, 0)),
            pl.BlockSpec(memory_space=pl.ANY),
        ],
        out_specs=pl.BlockSpec((tm, D_MODEL), lambda i: (i, 0)),
        out_shape=jax.ShapeDtypeStruct((T, D_MODEL), F32),
        scratch_shapes=[
            pltpu.SMEM((tm,), I32),
            pltpu.SMEM((tm,), I32),
            pltpu.VMEM((tm, D_MODEL), F32),
            pltpu.VMEM((tm, D_MODEL), F32),
            pltpu.SemaphoreType.DMA((2,)),
            pltpu.SemaphoreType.DMA(()),
        ],
        compiler_params=pltpu.CompilerParams(
            dimension_semantics=("arbitrary",), vmem_limit_bytes=VMEM_LIMIT),
        name="moe_combine",
    )(d0.reshape(T // tm, tm), d1.reshape(T // tm, tm), x1, route, fg, y)


def _trunk(x, p):
    B, L, _ = x.shape
    T = B * L
    x2 = x.reshape(T, D_MODEL)
    qkv, uc = _in_proj(x2, p["norm1_g"], p["wqkv"], p["wconv"])
    a = _attention(qkv.reshape(B, L, 3 * D_ATTN), p["na_bias"])
    c = _conv(uc.reshape(B, L, 2 * D_CONV), p["conv_w"], p["conv_b"], p["conv_ln_g"],
              p["conv_ln_b"], p["conv_out_g"])
    x1, h2, route, route_t, cnt = _out_proj(
        x2, a.reshape(T, D_ATTN), c.reshape(T, D_CONV), p["attn_out_g"], p["wout_a"],
        p["wout_c"], p["norm2_g"], p["w_router"], p["b_router"])

    blk = EXPERT_BLOCK
    e0 = route_t[COL_E0].astype(I32)
    e1 = route_t[COL_E1].astype(I32)
    counts = cnt[0, ROUTER_FIRST_EXPERT_LANE:ROUTER_FIRST_EXPERT_LANE + N_EXPERTS].astype(I32)
    padded = (counts + blk - 1) // blk * blk
    pad_end = jnp.cumsum(padded)
    pad_start = pad_end - padded
    experts = jnp.arange(N_EXPERTS, dtype=I32)

    def slot(e, rank):
        return jnp.sum(jnp.where(e[:, None] == experts[None, :], pad_start[None, :], 0), axis=1) + rank

    d0 = slot(e0, route_t[COL_R0].astype(I32))
    d1 = slot(e1, route_t[COL_R1].astype(I32))
    n_blocks = 2 * T // blk + N_EXPERTS
    block_start = jnp.arange(n_blocks, dtype=I32) * blk
    block_e = jnp.minimum(jnp.sum(block_start[:, None] >= pad_end[None, :], axis=1),
                          N_EXPERTS - 1).astype(I32)
    n_used = (pad_end[-1:] // blk).astype(I32)
    last_blk = jnp.where(padded > 0, pad_end // blk - 1, -1).astype(I32)
    last_blk = jnp.concatenate([last_blk, n_used])

    xb = _dispatch(last_blk, d0, d1, h2, n_blocks * blk)
    y = _experts(block_e, n_used, xb, p["w_gate"], p["w_up"], p["w_down"])
    out = _combine(d0, d1, x1, route, p["final_g"], y)
    return out.reshape(B, L, D_MODEL)


def kernel(x_prompt, x_sample, norm1_g, w_in, rpb, attn_out_g, conv_w, conv_b, conv_ln_g,
           conv_ln_b, conv_out_g, w_out, norm2_g, w_group, b_group, w_expert, b_expert,
           w_e_gate, w_e_up, w_e_down, final_g):
    l = 0
    w_router = jnp.concatenate(
        [w_group[l], w_expert[l].transpose(1, 0, 2).reshape(D_MODEL, N_EXPERTS)], axis=1)
    w_router = jnp.pad(w_router, ((0, 0), (0, LANES - w_router.shape[1])))
    w_router_hi = w_router.astype(BF16)
    w_router_lo = (w_router - w_router_hi.astype(F32)).astype(BF16)
    w_router = jnp.concatenate([w_router_hi, w_router_lo], axis=1)
    b_router = jnp.concatenate([b_group[l], b_expert[l].reshape(N_EXPERTS)])
    b_router = jnp.pad(b_router, (0, LANES - b_router.shape[0])).reshape(1, LANES)
    p = {
        "norm1_g": norm1_g[l].reshape(1, D_MODEL),
        "wqkv": w_in[l][:, :3 * D_ATTN].astype(BF16),
        "wconv": w_in[l][:, 3 * D_ATTN:].astype(BF16),
        "na_bias": _na_bias_table(rpb[l]),
        "attn_out_g": attn_out_g[l].reshape(1, D_ATTN),
        "conv_w": jnp.pad(conv_w[l], ((0, 1), (0, 0))),
        "conv_b": conv_b[l].reshape(1, D_CONV),
        "conv_ln_g": conv_ln_g[l].reshape(1, D_CONV),
        "conv_ln_b": conv_ln_b[l].reshape(1, D_CONV),
        "conv_out_g": conv_out_g[l].reshape(1, D_CONV),
        "wout_a": w_out[l][:D_ATTN].astype(BF16),
        "wout_c": w_out[l][D_ATTN:].astype(BF16),
        "norm2_g": norm2_g[l].reshape(1, D_MODEL),
        "w_router": w_router,
        "b_router": b_router,
        "w_gate": w_e_gate[l].astype(BF16),
        "w_up": w_e_up[l].astype(BF16),
        "w_down": w_e_down[l].astype(BF16),
        "final_g": final_g.reshape(1, D_MODEL),
    }
    return (_trunk(x_prompt, p), _trunk(x_sample, p))
```

```python
import functools

import jax
import jax.numpy as jnp
from jax import lax
from jax.experimental import pallas as pl
from jax.experimental.pallas import tpu as pltpu

F32 = jnp.float32
BF16 = jnp.bfloat16
I32 = jnp.int32

D_MODEL = 1024
GRID_W = 64
D_ATTN = 512
D_CONV = 512
HEAD_DIM = 64
N_HEADS = 8
NA_ROWS = 8
NA_COLS = 16
CONV_WIDTH = 31
N_GROUPS = 4
EPG = 8
N_EXPERTS = 32
D_EXPERT = 512
EPS = 1e-6

LANES = 128
HEAD_PAIRS = N_HEADS * HEAD_DIM // LANES
NA_KEYS = NA_ROWS * GRID_W
MASKED = -1e30
TOKEN_TILE = 512
EXPERT_BLOCK = 512
CONV_TILE = 64
CONV_SEGMENT = 512
CONV_HALO = 16
VMEM_LIMIT = 56 * 1024 * 1024


def _rms(x, g):
    return x * lax.rsqrt(jnp.mean(x * x, axis=-1, keepdims=True) + EPS) * g


def _in_proj_body(x_ref, g_ref, wqkv_ref, wc_ref, qkv_ref, uc_ref):
    h = _rms(x_ref[...], g_ref[...]).astype(BF16)
    qkv_ref[...] = jnp.dot(h, wqkv_ref[...], preferred_element_type=F32).astype(BF16)
    uc_ref[...] = jnp.dot(h, wc_ref[...], preferred_element_type=F32).astype(BF16)


def _in_proj(x2, g, wqkv, wc):
    T = x2.shape[0]
    tm = TOKEN_TILE
    return pl.pallas_call(
        _in_proj_body,
        grid=(T // tm,),
        in_specs=[
            pl.BlockSpec((tm, D_MODEL), lambda i: (i, 0)),
            pl.BlockSpec((1, D_MODEL), lambda i: (0, 0)),
            pl.BlockSpec((D_MODEL, 3 * D_ATTN), lambda i: (0, 0)),
            pl.BlockSpec((D_MODEL, 2 * D_CONV), lambda i: (0, 0)),
        ],
        out_specs=[
            pl.BlockSpec((tm, 3 * D_ATTN), lambda i: (i, 0)),
            pl.BlockSpec((tm, 2 * D_CONV), lambda i: (i, 0)),
        ],
        out_shape=[
            jax.ShapeDtypeStruct((T, 3 * D_ATTN), BF16),
            jax.ShapeDtypeStruct((T, 2 * D_CONV), BF16),
        ],
        compiler_params=pltpu.CompilerParams(
            dimension_semantics=("arbitrary",), vmem_limit_bytes=VMEM_LIMIT),
        name="in_proj",
    )(x2, g, wqkv, wc)


def _na_bias_table(rpb):
    c = jnp.arange(GRID_W)
    col_start = jnp.clip(c - NA_COLS // 2, 0, GRID_W - NA_COLS)
    cp = jnp.arange(GRID_W)
    valid = (cp[None, :] >= col_start[:, None]) & (cp[None, :] < col_start[:, None] + NA_COLS)
    col_off = cp[None, :] - c[:, None] + (NA_COLS - 1)
    sel = (col_off[None] == jnp.arange(2 * NA_COLS - 1)[:, None, None]) & valid[None]
    a = jnp.einsum("hrd,dcx->hrcx", rpb, sel.astype(F32), precision=lax.Precision.HIGHEST)
    a = jnp.where(valid[None, None], a, MASKED)
    t = jnp.stack([a[:, NA_ROWS - 1 - p:2 * NA_ROWS - 1 - p] for p in range(NA_ROWS)], axis=1)
    t = t.transpose(0, 1, 3, 2, 4)
    t = t.reshape(HEAD_PAIRS, 2, NA_ROWS, GRID_W, NA_KEYS)
    return t.transpose(0, 2, 1, 3, 4).reshape(HEAD_PAIRS, NA_ROWS, LANES, NA_KEYS).astype(F32)


def _attn_body(q_ref, k_ref, v_ref, bias_ref, o_ref, s_ref, p_ref, l_ref, *, rows, rows_per_step):
    rb = pl.program_id(2)
    first_head = lax.broadcasted_iota(I32, (GRID_W, LANES), 1) < HEAD_DIM
    zero = jnp.zeros((GRID_W, LANES), BF16)

    def window(rr):
        r = rb * rows_per_step + rr
        r_start = jnp.clip(r - NA_ROWS // 2, 0, rows - NA_ROWS)
        return r - r_start, pl.multiple_of(r_start * GRID_W, GRID_W)

    for rr in range(rows_per_step):
        p, k0 = window(rr)
        q = q_ref[rr * GRID_W:(rr + 1) * GRID_W, :] * jnp.asarray(HEAD_DIM ** -0.5, BF16)
        qbd = jnp.concatenate([jnp.where(first_head, q, zero), jnp.where(first_head, zero, q)], axis=0)
        ks = k_ref[pl.ds(k0, NA_KEYS), :]
        s = lax.dot_general(qbd, ks, (((1,), (1,)), ((), ())), preferred_element_type=F32)
        s_ref[rr] = s + bias_ref[p]
    for rr in range(rows_per_step):
        s = s_ref[rr]
        e = jnp.exp(s - jnp.max(s, axis=-1, keepdims=True))
        l_ref[rr] = jnp.broadcast_to(1.0 / jnp.sum(e, axis=-1, keepdims=True), (LANES, LANES))
        p_ref[rr] = e.astype(BF16)
    for rr in range(rows_per_step):
        _, k0 = window(rr)
        vs = v_ref[pl.ds(k0, NA_KEYS), :]
        o = jnp.dot(p_ref[rr], vs, preferred_element_type=F32) * l_ref[rr]
        out = jnp.where(first_head, o[:GRID_W], o[GRID_W:])
        o_ref[rr * GRID_W:(rr + 1) * GRID_W, :] = out.astype(BF16)


def _attention(qkv, bias):
    B, L, _ = qkv.shape
    rows = L // GRID_W
    rows_per_step = 8
    tq = rows_per_step * GRID_W
    body = functools.partial(_attn_body, rows=rows, rows_per_step=rows_per_step)
    return pl.pallas_call(
        body,
        grid=(HEAD_PAIRS, B, L // tq),
        in_specs=[
            pl.BlockSpec((None, tq, LANES), lambda hp, b, rb: (b, rb, hp)),
            pl.BlockSpec((None, L, LANES), lambda hp, b, rb: (b, 0, HEAD_PAIRS + hp)),
            pl.BlockSpec((None, L, LANES), lambda hp, b, rb: (b, 0, 2 * HEAD_PAIRS + hp)),
            pl.BlockSpec((None, NA_ROWS, LANES, NA_KEYS), lambda hp, b, rb: (hp, 0, 0, 0)),
        ],
        out_specs=pl.BlockSpec((None, tq, LANES), lambda hp, b, rb: (b, rb, hp)),
        out_shape=jax.ShapeDtypeStruct((B, L, D_ATTN), BF16),
        scratch_shapes=[
            pltpu.VMEM((rows_per_step, LANES, NA_KEYS), F32),
            pltpu.VMEM((rows_per_step, LANES, NA_KEYS), BF16),
            pltpu.VMEM((rows_per_step, LANES, LANES), F32),
        ],
        compiler_params=pltpu.CompilerParams(
            dimension_semantics=("arbitrary", "arbitrary", "arbitrary"),
            vmem_limit_bytes=VMEM_LIMIT),
        name="na_attention",
    )(qkv, qkv, qkv, bias)


def _glu(uc_rows):
    a = uc_rows[:, 0:D_CONV].astype(F32)
    gt = uc_rows[:, D_CONV:2 * D_CONV].astype(F32)
    return a * jax.nn.sigmoid(gt)


def _conv_body(uc_ref, w_ref, b_ref, lng_ref, lnb_ref, og_ref, o_ref, zs_ref, cbuf_ref, *, seg):
    sg = pl.program_id(1)
    n_seg = pl.num_programs(1)
    seg0 = pl.multiple_of(sg * seg, seg)
    halo = CONV_HALO
    n_lane_tiles = D_CONV // LANES

    def put_z(u0, n, z):
        for j in range(n_lane_tiles):
            zs_ref[0, j, pl.ds(u0, n), :] = z[:, j * LANES:(j + 1) * LANES]

    glu_rows = 128

    def glu(i, carry):
        u0 = pl.multiple_of(i * glu_rows, glu_rows)
        put_z(halo + u0, glu_rows, _glu(uc_ref[pl.ds(seg0 + u0, glu_rows), :]))
        return carry

    lax.fori_loop(0, seg // glu_rows, glu, 0)

    @pl.when(sg > 0)
    def _():
        put_z(0, halo, _glu(uc_ref[pl.ds(seg0 - halo, halo), :]))

    @pl.when(sg == 0)
    def _():
        put_z(0, halo, jnp.zeros((halo, D_CONV), F32))

    @pl.when(sg < n_seg - 1)
    def _():
        put_z(halo + seg, halo, _glu(uc_ref[pl.ds(seg0 + seg, halo), :]))

    @pl.when(sg == n_seg - 1)
    def _():
        put_z(halo + seg, halo, jnp.zeros((halo, D_CONV), F32))

    def shift_rows(j, u0, n):
        x = zs_ref[0, j, pl.ds(u0, n + 8), :]
        for s in range(1, 8):
            zs_ref[s, j, pl.ds(u0, n), :] = x[s:s + n]

    shift_chunk = 64
    chunks = seg // shift_chunk

    def shift(i, carry):
        shift_rows(i // chunks, pl.multiple_of((i % chunks) * shift_chunk, shift_chunk), shift_chunk)
        return carry

    lax.fori_loop(0, n_lane_tiles * chunks, shift, 0)
    for j in range(n_lane_tiles):
        shift_rows(j, seg, 2 * halo - 8)

    tt = CONV_TILE
    tiles = seg // tt
    first = halo - CONV_WIDTH // 2

    def taps(i, carry):
        j = i // tiles
        t0 = pl.multiple_of((i % tiles) * tt, tt)
        acc = jnp.broadcast_to(b_ref[j], (tt, LANES))
        for k in range(CONV_WIDTH):
            off = k + first
            acc = acc + w_ref[j, k:k + 1, :] * zs_ref[off % 8, j, pl.ds(t0 + 8 * (off // 8), tt), :]
        cbuf_ref[j, pl.ds(t0, tt), :] = acc
        return carry

    lax.fori_loop(0, n_lane_tiles * tiles, taps, 0, unroll=2)

    norm_rows = 128

    def norm(i, carry):
        t0 = pl.multiple_of(i * norm_rows, norm_rows)
        y = jnp.concatenate([cbuf_ref[j, pl.ds(t0, norm_rows), :] for j in range(n_lane_tiles)], axis=1)
        mu = jnp.mean(y, axis=-1, keepdims=True)
        yc = y - mu
        var = jnp.mean(yc * yc, axis=-1, keepdims=True)
        yn = yc * lax.rsqrt(var + EPS) * lng_ref[...] + lnb_ref[...]
        sw = yn * jax.nn.sigmoid(yn)
        o_ref[pl.ds(t0, norm_rows), :] = _rms(sw, og_ref[...]).astype(BF16)
        return carry

    lax.fori_loop(0, seg // norm_rows, norm, 0, unroll=2)


def _conv(uc, w, b, lng, lnb, og):
    B, L, _ = uc.shape
    seg = CONV_SEGMENT
    n_lane_tiles = D_CONV // LANES
    vec = pl.BlockSpec((1, D_CONV), lambda i, s: (0, 0))
    w = w.reshape(CONV_WIDTH + 1, n_lane_tiles, LANES).transpose(1, 0, 2)
    b = b.reshape(n_lane_tiles, 1, LANES)
    return pl.pallas_call(
        functools.partial(_conv_body, seg=seg),
        grid=(B, L // seg),
        in_specs=[
            pl.BlockSpec((None, L, 2 * D_CONV), lambda i, s: (i, 0, 0)),
            pl.BlockSpec((n_lane_tiles, CONV_WIDTH + 1, LANES), lambda i, s: (0, 0, 0)),
            pl.BlockSpec((n_lane_tiles, 1, LANES), lambda i, s: (0, 0, 0)),
            vec, vec, vec,
        ],
        out_specs=pl.BlockSpec((None, seg, D_CONV), lambda i, s: (i, s, 0)),
        out_shape=jax.ShapeDtypeStruct((B, L, D_CONV), BF16),
        scratch_shapes=[
            pltpu.VMEM((8, n_lane_tiles, seg + 2 * CONV_HALO, LANES), F32),
            pltpu.VMEM((n_lane_tiles, seg, LANES), F32),
        ],
        compiler_params=pltpu.CompilerParams(
            dimension_semantics=("arbitrary", "arbitrary"), vmem_limit_bytes=VMEM_LIMIT),
        name="conformer_conv",
    )(uc, w, b, lng, lnb, og)


GROUP_ROWS = 8
TILE_GROUPS = (2 * TOKEN_TILE + (GROUP_ROWS - 1) * N_EXPERTS + GROUP_ROWS - 1) // GROUP_ROWS
TILE_GROUPS = (TILE_GROUPS + 15) // 16 * 16
TILE_SLOTS = TILE_GROUPS * GROUP_ROWS
BLOCK_GROUPS = EXPERT_BLOCK // GROUP_ROWS
ROUTER_ROWS = 128
EXPERT_ROW0 = 8
COL_P0, COL_P1, COL_W0, COL_W1 = range(4)


def _sorted_out_proj_body(x_ref, a_ref, c_ref, ag_ref, wa_ref, wc_ref, n2g_ref, wrt_ref, brt_ref,
                          x1_ref, hs_ref, route_ref, grp_ref, *, tm):
    an = _rms(a_ref[...].astype(F32), ag_ref[...]).astype(BF16)
    y = jnp.dot(an, wa_ref[...], preferred_element_type=F32)
    y = y + jnp.dot(c_ref[...], wc_ref[...], preferred_element_type=F32)
    x1 = x_ref[...] + y
    x1_ref[...] = x1
    h2 = _rms(x1, n2g_ref[...])

    h_hi = h2.astype(BF16)
    h_lo = (h2 - h_hi.astype(F32)).astype(BF16)
    nt = (((1,), (1,)), ((), ()))
    split = (lax.dot_general(wrt_ref[...], h_hi, nt, preferred_element_type=F32)
             + lax.dot_general(wrt_ref[...], h_lo, nt, preferred_element_type=F32))
    logits = split[:ROUTER_ROWS] + split[ROUTER_ROWS:] + brt_ref[...]

    gtop = logits[0:1]
    gsel = jnp.zeros((1, tm), F32)
    for g in range(1, N_GROUPS):
        cand = logits[g:g + 1]
        better = cand > gtop
        gsel = jnp.where(better, float(g), gsel)
        gtop = jnp.where(better, cand, gtop)
    denom = jnp.zeros((1, tm), F32)
    for g in range(N_GROUPS):
        denom = denom + jnp.exp(logits[g:g + 1] - gtop)
    p_g = 1.0 / denom

    el = logits[EXPERT_ROW0:EXPERT_ROW0 + EPG]
    for g in range(1, N_GROUPS):
        el = jnp.where(gsel == float(g), logits[EXPERT_ROW0 + g * EPG:EXPERT_ROW0 + (g + 1) * EPG], el)
    ninf = jnp.full((1, tm), -jnp.inf, F32)
    v0, v1 = ninf, ninf
    i0 = jnp.zeros((1, tm), F32)
    i1 = jnp.zeros((1, tm), F32)
    for j in range(EPG):
        cand = el[j:j + 1]
        gt0 = cand > v0
        gt1 = cand > v1
        v1 = jnp.where(gt0, v0, jnp.where(gt1, cand, v1))
        i1 = jnp.where(gt0, i0, jnp.where(gt1, float(j), i1))
        v0 = jnp.where(gt0, cand, v0)
        i0 = jnp.where(gt0, float(j), i0)
    t = jnp.exp(v1 - v0)
    w0 = p_g / (1.0 + t)
    w1 = p_g * t / (1.0 + t)
    e0 = gsel * EPG + i0
    e1 = gsel * EPG + i1

    expert = lax.broadcasted_iota(I32, (N_EXPERTS, tm), 0).astype(F32)
    hit0 = expert == e0
    hit1 = expert == e1
    onehot = jnp.where(hit0 | hit1, 1.0, 0.0).astype(BF16)
    ri = lax.broadcasted_iota(I32, (tm, tm), 0)
    ci = lax.broadcasted_iota(I32, (tm, tm), 1)
    upper = jnp.where(ri <= ci, 1.0, 0.0).astype(BF16)
    cum = jnp.dot(onehot, upper, preferred_element_type=F32)
    count = jnp.dot(onehot, jnp.ones((tm, LANES), BF16), preferred_element_type=F32)
    groups = jnp.floor((count + (GROUP_ROWS - 1)) * (1.0 / GROUP_ROWS))
    er = lax.broadcasted_iota(I32, (N_EXPERTS, N_EXPERTS), 0)
    ec = lax.broadcasted_iota(I32, (N_EXPERTS, N_EXPERTS), 1)
    below = jnp.where(ec < er, 1.0, 0.0).astype(BF16)
    run_start = jnp.dot(below, groups.astype(BF16), preferred_element_type=F32) * GROUP_ROWS
    run_start = jnp.concatenate([run_start] * (tm // LANES), axis=1)
    pos = run_start + cum - 1.0
    p0 = jnp.sum(jnp.where(hit0, pos, 0.0), axis=0, keepdims=True)
    p1 = jnp.sum(jnp.where(hit1, pos, 0.0), axis=0, keepdims=True)
    grp_ref[...] = groups

    slot = lax.broadcasted_iota(I32, (TILE_SLOTS, tm), 0).astype(F32)
    perm = jnp.where((slot == p0) | (slot == p1), 1.0, 0.0).astype(BF16)
    hs_ref[...] = jnp.dot(perm, h_hi, preferred_element_type=F32)

    rows = jnp.concatenate([p0, p1, w0, w1, jnp.zeros((ROUTER_ROWS - 4, tm), F32)], axis=0)
    route_ref[...] = rows.T


def _sorted_out_proj(x2, a2, c2, ag, wa, wc, n2g, wrt, brt):
    T = x2.shape[0]
    tm = TOKEN_TILE
    nt = T // tm
    full = lambda shape: pl.BlockSpec(shape, lambda i: (0,) * len(shape))
    return pl.pallas_call(
        functools.partial(_sorted_out_proj_body, tm=tm),
        grid=(nt,),
        in_specs=[
            pl.BlockSpec((tm, D_MODEL), lambda i: (i, 0)),
            pl.BlockSpec((tm, D_ATTN), lambda i: (i, 0)),
            pl.BlockSpec((tm, D_CONV), lambda i: (i, 0)),
            full((1, D_ATTN)),
            full((D_ATTN, D_MODEL)),
            full((D_CONV, D_MODEL)),
            full((1, D_MODEL)),
            full((2 * ROUTER_ROWS, D_MODEL)),
            full((ROUTER_ROWS, 1)),
        ],
        out_specs=[
            pl.BlockSpec((tm, D_MODEL), lambda i: (i, 0)),
            pl.BlockSpec((TILE_SLOTS, D_MODEL), lambda i: (i, 0)),
            pl.BlockSpec((tm, LANES), lambda i: (i, 0)),
            pl.BlockSpec((None, N_EXPERTS, LANES), lambda i: (i, 0, 0)),
        ],
        out_shape=[
            jax.ShapeDtypeStruct((T, D_MODEL), F32),
            jax.ShapeDtypeStruct((nt * TILE_SLOTS, D_MODEL), F32),
            jax.ShapeDtypeStruct((T, LANES), F32),
            jax.ShapeDtypeStruct((nt, N_EXPERTS, LANES), F32),
        ],
        compiler_params=pltpu.CompilerParams(
            dimension_semantics=("arbitrary",), vmem_limit_bytes=VMEM_LIMIT),
        name="out_proj_router",
    )(x2, a2, c2, ag, wa, wc, n2g, wrt, brt)


def _group_copies(src_ref, hs_hbm, ys_hbm, xbuf, ybuf, gsem, ssem, block, slot, trash0):
    def group_index(i):
        return src_ref[block * BLOCK_GROUPS + i]

    def gather(i):
        g = jnp.maximum(group_index(i), 0)
        rows = pl.ds(pl.multiple_of(i * GROUP_ROWS, GROUP_ROWS), GROUP_ROWS)
        return pltpu.make_async_copy(hs_hbm.at[g], xbuf.at[slot, rows], gsem.at[slot])

    def scatter(i):
        g = group_index(i)
        g = jnp.where(g < 0, trash0 + slot * BLOCK_GROUPS + i, g)
        rows = pl.ds(pl.multiple_of(i * GROUP_ROWS, GROUP_ROWS), GROUP_ROWS)
        return pltpu.make_async_copy(ybuf.at[slot, rows], ys_hbm.at[g], ssem.at[slot])

    return gather, scatter


def _sorted_expert_body(be_ref, nu_ref, src_ref, tg_ref, hs_hbm, wg_ref, wu_ref, wd_ref, ys_hbm,
                        xbuf, ybuf, zero_ref, gsem, ssem, zsem, *, n_blocks, n_tiles, trash0):
    del be_ref
    b = pl.program_id(0)
    n_used = nu_ref[0]
    slot = b % 2

    @pl.when(b == 0)
    def _():
        zero_ref[...] = jnp.zeros_like(zero_ref)

        def zero_copy(g):
            return pltpu.make_async_copy(zero_ref, ys_hbm.at[g], zsem)

        def over_unused(fn):
            def tile(c, carry):
                def group(g, carry2):
                    fn(c * TILE_GROUPS + g)
                    return carry2
                return lax.fori_loop(tg_ref[c], TILE_GROUPS, group, carry)
            lax.fori_loop(0, n_tiles, tile, 0)

            def trash(i, carry):
                fn(trash0 + i)
                return carry
            lax.fori_loop(0, 2 * BLOCK_GROUPS, trash, 0)

        over_unused(lambda g: zero_copy(g).start())
        over_unused(lambda g: zero_copy(g).wait())

    def for_groups(fn):
        def step(i, carry):
            fn(i)
            return carry
        lax.fori_loop(0, BLOCK_GROUPS, step, 0, unroll=8)

    def copies(block, slot_):
        return _group_copies(src_ref, hs_hbm, ys_hbm, xbuf, ybuf, gsem, ssem, block, slot_, trash0)

    def start_gathers(block, slot_):
        gather, _ = copies(block, slot_)
        for_groups(lambda i: gather(i).start())

    def wait_gathers(block, slot_):
        gather, _ = copies(block, slot_)
        for_groups(lambda i: gather(i).wait())

    def start_scatters(block, slot_):
        _, scatter = copies(block, slot_)
        for_groups(lambda i: scatter(i).start())

    def wait_scatters(block, slot_):
        _, scatter = copies(block, slot_)
        for_groups(lambda i: scatter(i).wait())

    @pl.when((b == 0) & (n_used > 0))
    def _():
        start_gathers(0, 0)

    @pl.when((b >= 2) & (b - 2 < n_used))
    def _():
        wait_scatters(b - 2, slot)

    @pl.when(b < n_used)
    def _():
        wait_gathers(b, slot)

        @pl.when(b + 1 < n_used)
        def _():
            start_gathers(b + 1, 1 - slot)

        x = xbuf[slot].astype(BF16)
        g = jnp.dot(x, wg_ref[...], preferred_element_type=F32)
        u = jnp.dot(x, wu_ref[...], preferred_element_type=F32)
        hid = (g * jax.nn.sigmoid(g) * u).astype(BF16)
        ybuf[slot] = jnp.dot(hid, wd_ref[...], preferred_element_type=F32)
        start_scatters(b, slot)

    @pl.when(b == n_blocks - 1)
    def _():
        @pl.when((b >= 1) & (b - 1 < n_used))
        def _():
            wait_scatters(b - 1, 1 - slot)

        @pl.when(b < n_used)
        def _():
            wait_scatters(b, slot)


def _sorted_experts(block_e, n_used, src, tile_groups, hs3, wg, wu, wd):
    n_groups = hs3.shape[0]
    n_tiles = tile_groups.shape[0]
    n_blocks = block_e.shape[0]
    blk = EXPERT_BLOCK
    grid_spec = pltpu.PrefetchScalarGridSpec(
        num_scalar_prefetch=4,
        grid=(n_blocks,),
        in_specs=[
            pl.BlockSpec(memory_space=pl.ANY),
            pl.BlockSpec((None, D_MODEL, D_EXPERT), lambda b, be, nu, src, tg: (be[b], 0, 0)),
            pl.BlockSpec((None, D_MODEL, D_EXPERT), lambda b, be, nu, src, tg: (be[b], 0, 0)),
            pl.BlockSpec((None, D_EXPERT, D_MODEL), lambda b, be, nu, src, tg: (be[b], 0, 0)),
        ],
        out_specs=pl.BlockSpec(memory_space=pl.ANY),
        scratch_shapes=[
            pltpu.VMEM((2, blk, D_MODEL), F32),
            pltpu.VMEM((2, blk, D_MODEL), F32),
            pltpu.VMEM((GROUP_ROWS, D_MODEL), F32),
            pltpu.SemaphoreType.DMA((2,)),
            pltpu.SemaphoreType.DMA((2,)),
            pltpu.SemaphoreType.DMA(()),
        ],
    )
    return pl.pallas_call(
        functools.partial(_sorted_expert_body, n_blocks=n_blocks, n_tiles=n_tiles, trash0=n_groups),
        grid_spec=grid_spec,
        out_shape=jax.ShapeDtypeStruct((n_groups + 2 * BLOCK_GROUPS, GROUP_ROWS, D_MODEL), F32),
        compiler_params=pltpu.CompilerParams(
            dimension_semantics=("arbitrary",), vmem_limit_bytes=VMEM_LIMIT),
        name="moe_experts",
    )(block_e, n_used, src, tile_groups, hs3, wg, wu, wd)


def _sorted_combine_body(x1_ref, route_ref, fg_ref, ys_ref, o_ref, *, tm):
    ys = ys_ref[...].astype(BF16)
    route = route_ref[...]
    slot = lax.broadcasted_iota(I32, (tm, TILE_SLOTS), 1).astype(F32)
    gate = (jnp.where(slot == route[:, COL_P0:COL_P0 + 1], route[:, COL_W0:COL_W0 + 1], 0.0)
            + jnp.where(slot == route[:, COL_P1:COL_P1 + 1], route[:, COL_W1:COL_W1 + 1], 0.0))
    moe = jnp.dot(gate.astype(BF16), ys, preferred_element_type=F32)
    o_ref[...] = _rms(x1_ref[...] + moe, fg_ref[...])


def _sorted_combine(x1, route, fg, ys2):
    T = x1.shape[0]
    tm = TOKEN_TILE
    return pl.pallas_call(
        functools.partial(_sorted_combine_body, tm=tm),
        grid=(T // tm,),
        in_specs=[
            pl.BlockSpec((tm, D_MODEL), lambda i: (i, 0)),
            pl.BlockSpec((tm, LANES), lambda i: (i, 0)),
            pl.BlockSpec((1, D_MODEL), lambda i: (0, 0)),
            pl.BlockSpec((TILE_SLOTS, D_MODEL), lambda i: (i, 0)),
        ],
        out_specs=pl.BlockSpec((tm, D_MODEL), lambda i: (i, 0)),
        out_shape=jax.ShapeDtypeStruct((T, D_MODEL), F32),
        compiler_params=pltpu.CompilerParams(
            dimension_semantics=("arbitrary",), vmem_limit_bytes=VMEM_LIMIT),
        name="moe_combine",
    )(x1, route, fg, ys2)


def _moe_plan(groups):
    nt = groups.shape[0]
    run_start = jnp.cumsum(groups, axis=1) - groups
    cum_tiles = jnp.cumsum(groups, axis=0)
    total = cum_tiles[-1]
    blocks = (total + BLOCK_GROUPS - 1) // BLOCK_GROUPS
    blk_end = jnp.cumsum(blocks)
    blk_start = blk_end - blocks
    n_blocks = (2 * TOKEN_TILE + (GROUP_ROWS - 1) * N_EXPERTS) * nt // EXPERT_BLOCK + N_EXPERTS
    b = jnp.arange(n_blocks, dtype=I32)
    block_e = jnp.minimum(jnp.sum(b[:, None] >= blk_end[None, :], axis=1), N_EXPERTS - 1).astype(I32)
    n_used = blk_end[-1:].astype(I32)

    sg = jnp.arange(n_blocks * BLOCK_GROUPS, dtype=I32)
    sb = sg // BLOCK_GROUPS
    onehot_e = (block_e[sb][:, None] == jnp.arange(N_EXPERTS, dtype=I32)[None, :]).astype(F32)

    def pick(table):
        return jnp.dot(onehot_e, table.astype(F32), precision=lax.Precision.HIGHEST).astype(I32)

    j = sg - pick(blk_start[:, None])[:, 0] * BLOCK_GROUPS
    valid = (sb < n_used[0]) & (j < pick(total[:, None])[:, 0])
    cum_e = pick(cum_tiles.T)
    tile = jnp.minimum(jnp.sum(cum_e <= j[:, None], axis=1), nt - 1)
    tile_hot = tile[:, None] == jnp.arange(nt, dtype=I32)[None, :]
    before = jnp.sum(jnp.where(tile_hot, cum_e - pick(groups.T), 0), axis=1)
    start = jnp.sum(jnp.where(tile_hot, pick(run_start.T), 0), axis=1)
    src = jnp.where(valid, tile * TILE_GROUPS + start + (j - before), -1).astype(I32)
    tile_groups = jnp.sum(groups, axis=1).astype(I32)
    return block_e, n_used, src, tile_groups


def _trunk(x, p):
    B, L, _ = x.shape
    T = B * L
    x2 = x.reshape(T, D_MODEL)
    qkv, uc = _in_proj(x2, p["norm1_g"], p["wqkv"], p["wconv"])
    a = _attention(qkv.reshape(B, L, 3 * D_ATTN), p["na_bias"])
    c = _conv(uc.reshape(B, L, 2 * D_CONV), p["conv_w"], p["conv_b"], p["conv_ln_g"],
              p["conv_ln_b"], p["conv_out_g"])
    x1, hs, route, grp = _sorted_out_proj(
        x2, a.reshape(T, D_ATTN), c.reshape(T, D_CONV), p["attn_out_g"], p["wout_a"],
        p["wout_c"], p["norm2_g"], p["w_router_t"], p["b_router_t"])
    block_e, n_used, src, tile_groups = _moe_plan(grp[:, :, 0].astype(I32))
    hs3 = hs.reshape(hs.shape[0] // GROUP_ROWS, GROUP_ROWS, D_MODEL)
    ys3 = _sorted_experts(block_e, n_used, src, tile_groups, hs3, p["w_gate"], p["w_up"], p["w_down"])
    ys2 = ys3.reshape(ys3.shape[0] * GROUP_ROWS, D_MODEL)
    out = _sorted_combine(x1, route, p["final_g"], ys2)
    return out.reshape(B, L, D_MODEL)


def kernel(x_prompt, x_sample, norm1_g, w_in, rpb, attn_out_g, conv_w, conv_b, conv_ln_g,
           conv_ln_b, conv_out_g, w_out, norm2_g, w_group, b_group, w_expert, b_expert,
           w_e_gate, w_e_up, w_e_down, final_g):
    l = 0
    w_router_t = jnp.zeros((ROUTER_ROWS, D_MODEL), F32)
    w_router_t = w_router_t.at[:N_GROUPS].set(w_group[l].T)
    w_router_t = w_router_t.at[EXPERT_ROW0:EXPERT_ROW0 + N_EXPERTS].set(
        w_expert[l].transpose(0, 2, 1).reshape(N_EXPERTS, D_MODEL))
    w_router_hi = w_router_t.astype(BF16)
    w_router_lo = (w_router_t - w_router_hi.astype(F32)).astype(BF16)
    w_router_t = jnp.concatenate([w_router_hi, w_router_lo], axis=0)
    b_router_t = jnp.zeros((ROUTER_ROWS,), F32)
    b_router_t = b_router_t.at[:N_GROUPS].set(b_group[l])
    b_router_t = b_router_t.at[EXPERT_ROW0:EXPERT_ROW0 + N_EXPERTS].set(b_expert[l].reshape(N_EXPERTS))
    b_router_t = b_router_t.reshape(ROUTER_ROWS, 1)
    p = {
        "norm1_g": norm1_g[l].reshape(1, D_MODEL),
        "wqkv": w_in[l][:, :3 * D_ATTN].astype(BF16),
        "wconv": w_in[l][:, 3 * D_ATTN:].astype(BF16),
        "na_bias": _na_bias_table(rpb[l]),
        "attn_out_g": attn_out_g[l].reshape(1, D_ATTN),
        "conv_w": jnp.pad(conv_w[l], ((0, 1), (0, 0))),
        "conv_b": conv_b[l].reshape(1, D_CONV),
        "conv_ln_g": conv_ln_g[l].reshape(1, D_CONV),
        "conv_ln_b": conv_ln_b[l].reshape(1, D_CONV),
        "conv_out_g": conv_out_g[l].reshape(1, D_CONV),
        "wout_a": w_out[l][:D_ATTN].astype(BF16),
        "wout_c": w_out[l][D_ATTN:].astype(BF16),
        "norm2_g": norm2_g[l].reshape(1, D_MODEL),
        "w_router_t": w_router_t,
        "b_router_t": b_router_t,
        "w_gate": w_e_gate[l].astype(BF16),
        "w_up": w_e_up[l].astype(BF16),
        "w_down": w_e_down[l].astype(BF16),
        "final_g": final_g.reshape(1, D_MODEL),
    }
    return (_trunk(x_prompt, p), _trunk(x_sample, p))
```

```python
import functools

import jax
import jax.numpy as jnp
from jax import lax
from jax.experimental import pallas as pl
from jax.experimental.pallas import tpu as pltpu

F32 = jnp.float32
BF16 = jnp.bfloat16
I32 = jnp.int32

D_MODEL = 1024
GRID_W = 64
D_ATTN = 512
D_CONV = 512
HEAD_DIM = 64
N_HEADS = 8
NA_ROWS = 8
NA_COLS = 16
CONV_WIDTH = 31
N_GROUPS = 4
EPG = 8
N_EXPERTS = 32
D_EXPERT = 512
EPS = 1e-6

LANES = 128
HEAD_PAIRS = N_HEADS * HEAD_DIM // LANES
NA_KEYS = NA_ROWS * GRID_W
MASKED = -1e30
TOKEN_TILE = 512
EXPERT_BLOCK = 512
CONV_TILE = 64
CONV_SEGMENT = 512
CONV_HALO = 16
VMEM_LIMIT = 56 * 1024 * 1024


def _rms(x, g):
    return x * lax.rsqrt(jnp.mean(x * x, axis=-1, keepdims=True) + EPS) * g


def _in_proj_body(x_ref, g_ref, wqkv_ref, wc_ref, qkv_ref, uc_ref):
    h = _rms(x_ref[...], g_ref[...]).astype(BF16)
    qkv_ref[...] = jnp.dot(h, wqkv_ref[...], preferred_element_type=F32).astype(BF16)
    uc_ref[...] = jnp.dot(h, wc_ref[...], preferred_element_type=F32).astype(BF16)


def _in_proj(x2, g, wqkv, wc):
    T = x2.shape[0]
    tm = TOKEN_TILE
    return pl.pallas_call(
        _in_proj_body,
        grid=(T // tm,),
        in_specs=[
            pl.BlockSpec((tm, D_MODEL), lambda i: (i, 0)),
            pl.BlockSpec((1, D_MODEL), lambda i: (0, 0)),
            pl.BlockSpec((D_MODEL, 3 * D_ATTN), lambda i: (0, 0)),
            pl.BlockSpec((D_MODEL, 2 * D_CONV), lambda i: (0, 0)),
        ],
        out_specs=[
            pl.BlockSpec((tm, 3 * D_ATTN), lambda i: (i, 0)),
            pl.BlockSpec((tm, 2 * D_CONV), lambda i: (i, 0)),
        ],
        out_shape=[
            jax.ShapeDtypeStruct((T, 3 * D_ATTN), BF16),
            jax.ShapeDtypeStruct((T, 2 * D_CONV), BF16),
        ],
        compiler_params=pltpu.CompilerParams(
            dimension_semantics=("arbitrary",), vmem_limit_bytes=VMEM_LIMIT),
        name="in_proj",
    )(x2, g, wqkv, wc)


def _na_bias_table(rpb):
    c = jnp.arange(GRID_W)
    col_start = jnp.clip(c - NA_COLS // 2, 0, GRID_W - NA_COLS)
    cp = jnp.arange(GRID_W)
    valid = (cp[None, :] >= col_start[:, None]) & (cp[None, :] < col_start[:, None] + NA_COLS)
    col_off = cp[None, :] - c[:, None] + (NA_COLS - 1)
    sel = (col_off[None] == jnp.arange(2 * NA_COLS - 1)[:, None, None]) & valid[None]
    a = jnp.einsum("hrd,dcx->hrcx", rpb, sel.astype(F32), precision=lax.Precision.HIGHEST)
    a = jnp.where(valid[None, None], a, MASKED)
    t = jnp.stack([a[:, NA_ROWS - 1 - p:2 * NA_ROWS - 1 - p] for p in range(NA_ROWS)], axis=1)
    t = t.transpose(0, 1, 3, 2, 4)
    t = t.reshape(HEAD_PAIRS, 2, NA_ROWS, GRID_W, NA_KEYS)
    return t.transpose(0, 2, 1, 3, 4).reshape(HEAD_PAIRS, NA_ROWS, LANES, NA_KEYS).astype(F32)


def _attn_body(q_ref, k_ref, v_ref, bias_ref, o_ref, s_ref, p_ref, l_ref, *, rows, rows_per_step):
    rb = pl.program_id(2)
    first_head = lax.broadcasted_iota(I32, (GRID_W, LANES), 1) < HEAD_DIM
    zero = jnp.zeros((GRID_W, LANES), BF16)

    def window(rr):
        r = rb * rows_per_step + rr
        r_start = jnp.clip(r - NA_ROWS // 2, 0, rows - NA_ROWS)
        return r - r_start, pl.multiple_of(r_start * GRID_W, GRID_W)

    for rr in range(rows_per_step):
        p, k0 = window(rr)
        q = q_ref[rr * GRID_W:(rr + 1) * GRID_W, :] * jnp.asarray(HEAD_DIM ** -0.5, BF16)
        qbd = jnp.concatenate([jnp.where(first_head, q, zero), jnp.where(first_head, zero, q)], axis=0)
        ks = k_ref[pl.ds(k0, NA_KEYS), :]
        s = lax.dot_general(qbd, ks, (((1,), (1,)), ((), ())), preferred_element_type=F32)
        s_ref[rr] = s + bias_ref[p]
    for rr in range(rows_per_step):
        s = s_ref[rr]
        e = jnp.exp(s - jnp.max(s, axis=-1, keepdims=True))
        l_ref[rr] = jnp.broadcast_to(1.0 / jnp.sum(e, axis=-1, keepdims=True), (LANES, LANES))
        p_ref[rr] = e.astype(BF16)
    for rr in range(rows_per_step):
        _, k0 = window(rr)
        vs = v_ref[pl.ds(k0, NA_KEYS), :]
        o = jnp.dot(p_ref[rr], vs, preferred_element_type=F32) * l_ref[rr]
        out = jnp.where(first_head, o[:GRID_W], o[GRID_W:])
        o_ref[rr * GRID_W:(rr + 1) * GRID_W, :] = out.astype(BF16)


def _attention(qkv, bias):
    B, L, _ = qkv.shape
    rows = L // GRID_W
    rows_per_step = 8
    tq = rows_per_step * GRID_W
    body = functools.partial(_attn_body, rows=rows, rows_per_step=rows_per_step)
    return pl.pallas_call(
        body,
        grid=(HEAD_PAIRS, B, L // tq),
        in_specs=[
            pl.BlockSpec((None, tq, LANES), lambda hp, b, rb: (b, rb, hp)),
            pl.BlockSpec((None, L, LANES), lambda hp, b, rb: (b, 0, HEAD_PAIRS + hp)),
            pl.BlockSpec((None, L, LANES), lambda hp, b, rb: (b, 0, 2 * HEAD_PAIRS + hp)),
            pl.BlockSpec((None, NA_ROWS, LANES, NA_KEYS), lambda hp, b, rb: (hp, 0, 0, 0)),
        ],
        out_specs=pl.BlockSpec((None, tq, LANES), lambda hp, b, rb: (b, rb, hp)),
        out_shape=jax.ShapeDtypeStruct((B, L, D_ATTN), BF16),
        scratch_shapes=[
            pltpu.VMEM((rows_per_step, LANES, NA_KEYS), F32),
            pltpu.VMEM((rows_per_step, LANES, NA_KEYS), BF16),
            pltpu.VMEM((rows_per_step, LANES, LANES), F32),
        ],
        compiler_params=pltpu.CompilerParams(
            dimension_semantics=("arbitrary", "arbitrary", "arbitrary"),
            vmem_limit_bytes=VMEM_LIMIT),
        name="na_attention",
    )(qkv, qkv, qkv, bias)


def _glu(uc_rows):
    a = uc_rows[:, 0:D_CONV].astype(F32)
    gt = uc_rows[:, D_CONV:2 * D_CONV].astype(F32)
    return a * jax.nn.sigmoid(gt)


def _conv_body(uc_ref, w_ref, b_ref, lng_ref, lnb_ref, og_ref, o_ref, zs_ref, cbuf_ref, *, seg):
    sg = pl.program_id(1)
    n_seg = pl.num_programs(1)
    seg0 = pl.multiple_of(sg * seg, seg)
    halo = CONV_HALO
    n_lane_tiles = D_CONV // LANES

    def put_z(u0, n, z):
        for j in range(n_lane_tiles):
            zs_ref[0, j, pl.ds(u0, n), :] = z[:, j * LANES:(j + 1) * LANES]

    glu_rows = 128

    def glu(i, carry):
        u0 = pl.multiple_of(i * glu_rows, glu_rows)
        put_z(halo + u0, glu_rows, _glu(uc_ref[pl.ds(seg0 + u0, glu_rows), :]))
        return carry

    lax.fori_loop(0, seg // glu_rows, glu, 0)

    @pl.when(sg > 0)
    def _():
        put_z(0, halo, _glu(uc_ref[pl.ds(seg0 - halo, halo), :]))

    @pl.when(sg == 0)
    def _():
        put_z(0, halo, jnp.zeros((halo, D_CONV), F32))

    @pl.when(sg < n_seg - 1)
    def _():
        put_z(halo + seg, halo, _glu(uc_ref[pl.ds(seg0 + seg, halo), :]))

    @pl.when(sg == n_seg - 1)
    def _():
        put_z(halo + seg, halo, jnp.zeros((halo, D_CONV), F32))

    def shift_rows(j, u0, n):
        x = zs_ref[0, j, pl.ds(u0, n + 8), :]
        for s in range(1, 8):
            zs_ref[s, j, pl.ds(u0, n), :] = x[s:s + n]

    shift_chunk = 64
    chunks = seg // shift_chunk

    def shift(i, carry):
        shift_rows(i // chunks, pl.multiple_of((i % chunks) * shift_chunk, shift_chunk), shift_chunk)
        return carry

    lax.fori_loop(0, n_lane_tiles * chunks, shift, 0)
    for j in range(n_lane_tiles):
        shift_rows(j, seg, 2 * halo - 8)

    tt = CONV_TILE
    tiles = seg // tt
    first = halo - CONV_WIDTH // 2

    def taps(i, carry):
        j = i // tiles
        t0 = pl.multiple_of((i % tiles) * tt, tt)
        acc = jnp.broadcast_to(b_ref[j], (tt, LANES))
        for k in range(CONV_WIDTH):
            off = k + first
            acc = acc + w_ref[j, k:k + 1, :] * zs_ref[off % 8, j, pl.ds(t0 + 8 * (off // 8), tt), :]
        cbuf_ref[j, pl.ds(t0, tt), :] = acc
        return carry

    lax.fori_loop(0, n_lane_tiles * tiles, taps, 0, unroll=2)

    norm_rows = 128

    def norm(i, carry):
        t0 = pl.multiple_of(i * norm_rows, norm_rows)
        y = jnp.concatenate([cbuf_ref[j, pl.ds(t0, norm_rows), :] for j in range(n_lane_tiles)], axis=1)
        mu = jnp.mean(y, axis=-1, keepdims=True)
        yc = y - mu
        var = jnp.mean(yc * yc, axis=-1, keepdims=True)
        yn = yc * lax.rsqrt(var + EPS) * lng_ref[...] + lnb_ref[...]
        sw = yn * jax.nn.sigmoid(yn)
        o_ref[pl.ds(t0, norm_rows), :] = _rms(sw, og_ref[...]).astype(BF16)
        return carry

    lax.fori_loop(0, seg // norm_rows, norm, 0, unroll=2)


def _conv(uc, w, b, lng, lnb, og):
    B, L, _ = uc.shape
    seg = CONV_SEGMENT
    n_lane_tiles = D_CONV // LANES
    vec = pl.BlockSpec((1, D_CONV), lambda i, s: (0, 0))
    w = w.reshape(CONV_WIDTH + 1, n_lane_tiles, LANES).transpose(1, 0, 2)
    b = b.reshape(n_lane_tiles, 1, LANES)
    return pl.pallas_call(
        functools.partial(_conv_body, seg=seg),
        grid=(B, L // seg),
        in_specs=[
            pl.BlockSpec((None, L, 2 * D_CONV), lambda i, s: (i, 0, 0)),
            pl.BlockSpec((n_lane_tiles, CONV_WIDTH + 1, LANES), lambda i, s: (0, 0, 0)),
            pl.BlockSpec((n_lane_tiles, 1, LANES), lambda i, s: (0, 0, 0)),
            vec, vec, vec,
        ],
        out_specs=pl.BlockSpec((None, seg, D_CONV), lambda i, s: (i, s, 0)),
        out_shape=jax.ShapeDtypeStruct((B, L, D_CONV), BF16),
        scratch_shapes=[
            pltpu.VMEM((8, n_lane_tiles, seg + 2 * CONV_HALO, LANES), F32),
            pltpu.VMEM((n_lane_tiles, seg, LANES), F32),
        ],
        compiler_params=pltpu.CompilerParams(
            dimension_semantics=("arbitrary", "arbitrary"), vmem_limit_bytes=VMEM_LIMIT),
        name="conformer_conv",
    )(uc, w, b, lng, lnb, og)


GROUP_ROWS = 8
TILE_GROUPS = (2 * TOKEN_TILE + (GROUP_ROWS - 1) * N_EXPERTS + GROUP_ROWS - 1) // GROUP_ROWS
TILE_GROUPS = (TILE_GROUPS + 15) // 16 * 16
TILE_SLOTS = TILE_GROUPS * GROUP_ROWS
BLOCK_GROUPS = EXPERT_BLOCK // GROUP_ROWS
ROUTER_ROWS = 128
EXPERT_ROW0 = 8
COL_P0, COL_P1, COL_W0, COL_W1 = range(4)


U32 = jnp.uint32
PACKED = D_MODEL // 2


def _pack_bf16_pairs(x):
    bits = pltpu.bitcast(x, U32)
    return bits[:, PACKED:] | (bits[:, :PACKED] >> 16)


def _unpack_bf16_pairs(words):
    lo = pltpu.bitcast(words << 16, F32)
    hi = pltpu.bitcast(words & jnp.uint32(0xFFFF0000), F32)
    return jnp.concatenate([lo, hi], axis=1).astype(BF16)


def _sorted_out_proj_body(x_ref, a_ref, c_ref, ag_ref, wa_ref, wc_ref, n2g_ref, wrt_ref, brt_ref,
                          x1_ref, hs_ref, route_ref, grp_ref, *, tm):
    an = _rms(a_ref[...].astype(F32), ag_ref[...]).astype(BF16)
    y = jnp.dot(an, wa_ref[...], preferred_element_type=F32)
    y = y + jnp.dot(c_ref[...], wc_ref[...], preferred_element_type=F32)
    x1 = x_ref[...] + y
    x1_ref[...] = x1
    h2 = _rms(x1, n2g_ref[...])

    h_hi = h2.astype(BF16)
    h_lo = (h2 - h_hi.astype(F32)).astype(BF16)
    nt = (((1,), (1,)), ((), ()))
    split = (lax.dot_general(wrt_ref[...], h_hi, nt, preferred_element_type=F32)
             + lax.dot_general(wrt_ref[...], h_lo, nt, preferred_element_type=F32))
    logits = split[:ROUTER_ROWS] + split[ROUTER_ROWS:] + brt_ref[...]

    gtop = logits[0:1]
    gsel = jnp.zeros((1, tm), F32)
    for g in range(1, N_GROUPS):
        cand = logits[g:g + 1]
        better = cand > gtop
        gsel = jnp.where(better, float(g), gsel)
        gtop = jnp.where(better, cand, gtop)
    denom = jnp.zeros((1, tm), F32)
    for g in range(N_GROUPS):
        denom = denom + jnp.exp(logits[g:g + 1] - gtop)
    p_g = 1.0 / denom

    el = logits[EXPERT_ROW0:EXPERT_ROW0 + EPG]
    for g in range(1, N_GROUPS):
        el = jnp.where(gsel == float(g), logits[EXPERT_ROW0 + g * EPG:EXPERT_ROW0 + (g + 1) * EPG], el)
    ninf = jnp.full((1, tm), -jnp.inf, F32)
    v0, v1 = ninf, ninf
    i0 = jnp.zeros((1, tm), F32)
    i1 = jnp.zeros((1, tm), F32)
    for j in range(EPG):
        cand = el[j:j + 1]
        gt0 = cand > v0
        gt1 = cand > v1
        v1 = jnp.where(gt0, v0, jnp.where(gt1, cand, v1))
        i1 = jnp.where(gt0, i0, jnp.where(gt1, float(j), i1))
        v0 = jnp.where(gt0, cand, v0)
        i0 = jnp.where(gt0, float(j), i0)
    t = jnp.exp(v1 - v0)
    w0 = p_g / (1.0 + t)
    w1 = p_g * t / (1.0 + t)
    e0 = gsel * EPG + i0
    e1 = gsel * EPG + i1

    expert = lax.broadcasted_iota(I32, (N_EXPERTS, tm), 0).astype(F32)
    hit0 = expert == e0
    hit1 = expert == e1
    onehot = jnp.where(hit0 | hit1, 1.0, 0.0).astype(BF16)
    ri = lax.broadcasted_iota(I32, (tm, tm), 0)
    ci = lax.broadcasted_iota(I32, (tm, tm), 1)
    upper = jnp.where(ri <= ci, 1.0, 0.0).astype(BF16)
    cum = jnp.dot(onehot, upper, preferred_element_type=F32)
    count = jnp.dot(onehot, jnp.ones((tm, LANES), BF16), preferred_element_type=F32)
    groups = jnp.floor((count + (GROUP_ROWS - 1)) * (1.0 / GROUP_ROWS))
    er = lax.broadcasted_iota(I32, (N_EXPERTS, N_EXPERTS), 0)
    ec = lax.broadcasted_iota(I32, (N_EXPERTS, N_EXPERTS), 1)
    below = jnp.where(ec < er, 1.0, 0.0).astype(BF16)
    run_start = jnp.dot(below, groups.astype(BF16), preferred_element_type=F32) * GROUP_ROWS
    run_start = jnp.concatenate([run_start] * (tm // LANES), axis=1)
    pos = run_start + cum - 1.0
    p0 = jnp.sum(jnp.where(hit0, pos, 0.0), axis=0, keepdims=True)
    p1 = jnp.sum(jnp.where(hit1, pos, 0.0), axis=0, keepdims=True)
    grp_ref[...] = groups

    slot = lax.broadcasted_iota(I32, (TILE_SLOTS, tm), 0).astype(F32)
    perm = jnp.where((slot == p0) | (slot == p1), 1.0, 0.0).astype(BF16)
    hs_ref[...] = _pack_bf16_pairs(jnp.dot(perm, h_hi, preferred_element_type=F32))

    rows = jnp.concatenate([p0, p1, w0, w1, jnp.zeros((ROUTER_ROWS - 4, tm), F32)], axis=0)
    route_ref[...] = rows.T


def _sorted_out_proj(x2, a2, c2, ag, wa, wc, n2g, wrt, brt):
    T = x2.shape[0]
    tm = TOKEN_TILE
    nt = T // tm
    full = lambda shape: pl.BlockSpec(shape, lambda i: (0,) * len(shape))
    return pl.pallas_call(
        functools.partial(_sorted_out_proj_body, tm=tm),
        grid=(nt,),
        in_specs=[
            pl.BlockSpec((tm, D_MODEL), lambda i: (i, 0)),
            pl.BlockSpec((tm, D_ATTN), lambda i: (i, 0)),
            pl.BlockSpec((tm, D_CONV), lambda i: (i, 0)),
            full((1, D_ATTN)),
            full((D_ATTN, D_MODEL)),
            full((D_CONV, D_MODEL)),
            full((1, D_MODEL)),
            full((2 * ROUTER_ROWS, D_MODEL)),
            full((ROUTER_ROWS, 1)),
        ],
        out_specs=[
            pl.BlockSpec((tm, D_MODEL), lambda i: (i, 0)),
            pl.BlockSpec((TILE_SLOTS, PACKED), lambda i: (i, 0)),
            pl.BlockSpec((tm, LANES), lambda i: (i, 0)),
            pl.BlockSpec((None, N_EXPERTS, LANES), lambda i: (i, 0, 0)),
        ],
        out_shape=[
            jax.ShapeDtypeStruct((T, D_MODEL), F32),
            jax.ShapeDtypeStruct((nt * TILE_SLOTS, PACKED), U32),
            jax.ShapeDtypeStruct((T, LANES), F32),
            jax.ShapeDtypeStruct((nt, N_EXPERTS, LANES), F32),
        ],
        compiler_params=pltpu.CompilerParams(
            dimension_semantics=("arbitrary",), vmem_limit_bytes=VMEM_LIMIT),
        name="out_proj_router",
    )(x2, a2, c2, ag, wa, wc, n2g, wrt, brt)


def _group_copies(src_ref, hs_hbm, ys_hbm, xbuf, ybuf, gsem, ssem, block, slot, trash0):
    def group_index(i):
        return src_ref[block * BLOCK_GROUPS + i]

    def gather(i):
        g = jnp.maximum(group_index(i), 0)
        rows = pl.ds(pl.multiple_of(i * GROUP_ROWS, GROUP_ROWS), GROUP_ROWS)
        return pltpu.make_async_copy(hs_hbm.at[g], xbuf.at[slot, rows], gsem.at[slot])

    def scatter(i):
        g = group_index(i)
        g = jnp.where(g < 0, trash0 + slot * BLOCK_GROUPS + i, g)
        rows = pl.ds(pl.multiple_of(i * GROUP_ROWS, GROUP_ROWS), GROUP_ROWS)
        return pltpu.make_async_copy(ybuf.at[slot, rows], ys_hbm.at[g], ssem.at[slot])

    return gather, scatter


def _sorted_expert_body(be_ref, nu_ref, src_ref, tg_ref, hs_hbm, wg_ref, wu_ref, wd_ref, ys_hbm,
                        xbuf, ybuf, zero_ref, gsem, ssem, zsem, *, n_blocks, n_tiles, trash0):
    del be_ref
    b = pl.program_id(0)
    n_used = nu_ref[0]
    slot = b % 2

    @pl.when(b == 0)
    def _():
        zero_ref[...] = jnp.zeros_like(zero_ref)

        def zero_copy(g):
            return pltpu.make_async_copy(zero_ref, ys_hbm.at[g], zsem)

        def over_unused(fn):
            def tile(c, carry):
                def group(g, carry2):
                    fn(c * TILE_GROUPS + g)
                    return carry2
                return lax.fori_loop(tg_ref[c], TILE_GROUPS, group, carry)
            lax.fori_loop(0, n_tiles, tile, 0)

            def trash(i, carry):
                fn(trash0 + i)
                return carry
            lax.fori_loop(0, 2 * BLOCK_GROUPS, trash, 0)

        over_unused(lambda g: zero_copy(g).start())
        over_unused(lambda g: zero_copy(g).wait())

    def for_groups(fn):
        def step(i, carry):
            fn(i)
            return carry
        lax.fori_loop(0, BLOCK_GROUPS, step, 0, unroll=8)

    def copies(block, slot_):
        return _group_copies(src_ref, hs_hbm, ys_hbm, xbuf, ybuf, gsem, ssem, block, slot_, trash0)

    def start_gathers(block, slot_):
        gather, _ = copies(block, slot_)
        for_groups(lambda i: gather(i).start())

    def wait_gathers(block, slot_):
        gather, _ = copies(block, slot_)
        for_groups(lambda i: gather(i).wait())

    def start_scatters(block, slot_):
        _, scatter = copies(block, slot_)
        for_groups(lambda i: scatter(i).start())

    def wait_scatters(block, slot_):
        _, scatter = copies(block, slot_)
        for_groups(lambda i: scatter(i).wait())

    @pl.when((b == 0) & (n_used > 0))
    def _():
        start_gathers(0, 0)

    @pl.when((b >= 2) & (b - 2 < n_used))
    def _():
        wait_scatters(b - 2, slot)

    @pl.when(b < n_used)
    def _():
        wait_gathers(b, slot)

        @pl.when(b + 1 < n_used)
        def _():
            start_gathers(b + 1, 1 - slot)

        x = _unpack_bf16_pairs(xbuf[slot])
        g = jnp.dot(x, wg_ref[...], preferred_element_type=F32)
        u = jnp.dot(x, wu_ref[...], preferred_element_type=F32)
        hid = (g * jax.nn.sigmoid(g) * u).astype(BF16)
        y = jnp.dot(hid, wd_ref[...], preferred_element_type=F32)
        ybuf[slot] = _pack_bf16_pairs(y.astype(BF16).astype(F32))
        start_scatters(b, slot)

    @pl.when(b == n_blocks - 1)
    def _():
        @pl.when((b >= 1) & (b - 1 < n_used))
        def _():
            wait_scatters(b - 1, 1 - slot)

        @pl.when(b < n_used)
        def _():
            wait_scatters(b, slot)


def _sorted_experts(block_e, n_used, src, tile_groups, hs3, wg, wu, wd):
    n_groups = hs3.shape[0]
    n_tiles = tile_groups.shape[0]
    n_blocks = block_e.shape[0]
    blk = EXPERT_BLOCK
    grid_spec = pltpu.PrefetchScalarGridSpec(
        num_scalar_prefetch=4,
        grid=(n_blocks,),
        in_specs=[
            pl.BlockSpec(memory_space=pl.ANY),
            pl.BlockSpec((None, D_MODEL, D_EXPERT), lambda b, be, nu, src, tg: (be[b], 0, 0)),
            pl.BlockSpec((None, D_MODEL, D_EXPERT), lambda b, be, nu, src, tg: (be[b], 0, 0)),
            pl.BlockSpec((None, D_EXPERT, D_MODEL), lambda b, be, nu, src, tg: (be[b], 0, 0)),
        ],
        out_specs=pl.BlockSpec(memory_space=pl.ANY),
        scratch_shapes=[
            pltpu.VMEM((2, blk, PACKED), U32),
            pltpu.VMEM((2, blk, PACKED), U32),
            pltpu.VMEM((GROUP_ROWS, PACKED), U32),
            pltpu.SemaphoreType.DMA((2,)),
            pltpu.SemaphoreType.DMA((2,)),
            pltpu.SemaphoreType.DMA(()),
        ],
    )
    return pl.pallas_call(
        functools.partial(_sorted_expert_body, n_blocks=n_blocks, n_tiles=n_tiles, trash0=n_groups),
        grid_spec=grid_spec,
        out_shape=jax.ShapeDtypeStruct((n_groups + 2 * BLOCK_GROUPS, GROUP_ROWS, PACKED), U32),
        compiler_params=pltpu.CompilerParams(
            dimension_semantics=("arbitrary",), vmem_limit_bytes=VMEM_LIMIT),
        name="moe_experts",
    )(block_e, n_used, src, tile_groups, hs3, wg, wu, wd)


def _sorted_combine_body(x1_ref, route_ref, fg_ref, ys_ref, o_ref, *, tm):
    ys = _unpack_bf16_pairs(ys_ref[...])
    route = route_ref[...]
    slot = lax.broadcasted_iota(I32, (tm, TILE_SLOTS), 1).astype(F32)
    gate = (jnp.where(slot == route[:, COL_P0:COL_P0 + 1], route[:, COL_W0:COL_W0 + 1], 0.0)
            + jnp.where(slot == route[:, COL_P1:COL_P1 + 1], route[:, COL_W1:COL_W1 + 1], 0.0))
    moe = jnp.dot(gate.astype(BF16), ys, preferred_element_type=F32)
    o_ref[...] = _rms(x1_ref[...] + moe, fg_ref[...])


def _sorted_combine(x1, route, fg, ys2):
    T = x1.shape[0]
    tm = TOKEN_TILE
    return pl.pallas_call(
        functools.partial(_sorted_combine_body, tm=tm),
        grid=(T // tm,),
        in_specs=[
            pl.BlockSpec((tm, D_MODEL), lambda i: (i, 0)),
            pl.BlockSpec((tm, LANES), lambda i: (i, 0)),
            pl.BlockSpec((1, D_MODEL), lambda i: (0, 0)),
            pl.BlockSpec((TILE_SLOTS, PACKED), lambda i: (i, 0)),
        ],
        out_specs=pl.BlockSpec((tm, D_MODEL), lambda i: (i, 0)),
        out_shape=jax.ShapeDtypeStruct((T, D_MODEL), F32),
        compiler_params=pltpu.CompilerParams(
            dimension_semantics=("arbitrary",), vmem_limit_bytes=VMEM_LIMIT),
        name="moe_combine",
    )(x1, route, fg, ys2)


def _moe_plan(groups):
    nt = groups.shape[0]
    run_start = jnp.cumsum(groups, axis=1) - groups
    cum_tiles = jnp.cumsum(groups, axis=0)
    total = cum_tiles[-1]
    blocks = (total + BLOCK_GROUPS - 1) // BLOCK_GROUPS
    blk_end = jnp.cumsum(blocks)
    blk_start = blk_end - blocks
    n_blocks = (2 * TOKEN_TILE + (GROUP_ROWS - 1) * N_EXPERTS) * nt // EXPERT_BLOCK + N_EXPERTS
    b = jnp.arange(n_blocks, dtype=I32)
    block_e = jnp.minimum(jnp.sum(b[:, None] >= blk_end[None, :], axis=1), N_EXPERTS - 1).astype(I32)
    n_used = blk_end[-1:].astype(I32)

    sg = jnp.arange(n_blocks * BLOCK_GROUPS, dtype=I32)
    sb = sg // BLOCK_GROUPS
    sg_e = jnp.broadcast_to(block_e[:, None], (n_blocks, BLOCK_GROUPS)).reshape(-1)
    onehot_e = (sg_e[:, None] == jnp.arange(N_EXPERTS, dtype=I32)[None, :]).astype(F32)

    def pick(table):
        return jnp.dot(onehot_e, table.astype(F32), precision=lax.Precision.HIGHEST).astype(I32)

    j = sg - pick(blk_start[:, None])[:, 0] * BLOCK_GROUPS
    valid = (sb < n_used[0]) & (j < pick(total[:, None])[:, 0])
    cum_e = pick(cum_tiles.T)
    tile = jnp.minimum(jnp.sum(cum_e <= j[:, None], axis=1), nt - 1)
    tile_hot = tile[:, None] == jnp.arange(nt, dtype=I32)[None, :]
    before = jnp.sum(jnp.where(tile_hot, cum_e - pick(groups.T), 0), axis=1)
    start = jnp.sum(jnp.where(tile_hot, pick(run_start.T), 0), axis=1)
    src = jnp.where(valid, tile * TILE_GROUPS + start + (j - before), -1).astype(I32)
    tile_groups = jnp.sum(groups, axis=1).astype(I32)
    return block_e, n_used, src, tile_groups


def _trunk(x, p):
    B, L, _ = x.shape
    T = B * L
    x2 = x.reshape(T, D_MODEL)
    qkv, uc = _in_proj(x2, p["norm1_g"], p["wqkv"], p["wconv"])
    a = _attention(qkv.reshape(B, L, 3 * D_ATTN), p["na_bias"])
    c = _conv(uc.reshape(B, L, 2 * D_CONV), p["conv_w"], p["conv_b"], p["conv_ln_g"],
              p["conv_ln_b"], p["conv_out_g"])
    x1, hs, route, grp = _sorted_out_proj(
        x2, a.reshape(T, D_ATTN), c.reshape(T, D_CONV), p["attn_out_g"], p["wout_a"],
        p["wout_c"], p["norm2_g"], p["w_router_t"], p["b_router_t"])
    block_e, n_used, src, tile_groups = _moe_plan(grp[:, :, 0].astype(I32))
    hs3 = hs.reshape(hs.shape[0] // GROUP_ROWS, GROUP_ROWS, PACKED)
    ys3 = _sorted_experts(block_e, n_used, src, tile_groups, hs3, p["w_gate"], p["w_up"], p["w_down"])
    ys2 = ys3.reshape(ys3.shape[0] * GROUP_ROWS, PACKED)
    out = _sorted_combine(x1, route, p["final_g"], ys2)
    return out.reshape(B, L, D_MODEL)


def kernel(x_prompt, x_sample, norm1_g, w_in, rpb, attn_out_g, conv_w, conv_b, conv_ln_g,
           conv_ln_b, conv_out_g, w_out, norm2_g, w_group, b_group, w_expert, b_expert,
           w_e_gate, w_e_up, w_e_down, final_g):
    l = 0
    w_router_t = jnp.zeros((ROUTER_ROWS, D_MODEL), F32)
    w_router_t = w_router_t.at[:N_GROUPS].set(w_group[l].T)
    w_router_t = w_router_t.at[EXPERT_ROW0:EXPERT_ROW0 + N_EXPERTS].set(
        w_expert[l].transpose(0, 2, 1).reshape(N_EXPERTS, D_MODEL))
    w_router_hi = w_router_t.astype(BF16)
    w_router_lo = (w_router_t - w_router_hi.astype(F32)).astype(BF16)
    w_router_t = jnp.concatenate([w_router_hi, w_router_lo], axis=0)
    b_router_t = jnp.zeros((ROUTER_ROWS,), F32)
    b_router_t = b_router_t.at[:N_GROUPS].set(b_group[l])
    b_router_t = b_router_t.at[EXPERT_ROW0:EXPERT_ROW0 + N_EXPERTS].set(b_expert[l].reshape(N_EXPERTS))
    b_router_t = b_router_t.reshape(ROUTER_ROWS, 1)
    p = {
        "norm1_g": norm1_g[l].reshape(1, D_MODEL),
        "wqkv": w_in[l][:, :3 * D_ATTN].astype(BF16),
        "wconv": w_in[l][:, 3 * D_ATTN:].astype(BF16),
        "na_bias": _na_bias_table(rpb[l]),
        "attn_out_g": attn_out_g[l].reshape(1, D_ATTN),
        "conv_w": jnp.pad(conv_w[l], ((0, 1), (0, 0))),
        "conv_b": conv_b[l].reshape(1, D_CONV),
        "conv_ln_g": conv_ln_g[l].reshape(1, D_CONV),
        "conv_ln_b": conv_ln_b[l].reshape(1, D_CONV),
        "conv_out_g": conv_out_g[l].reshape(1, D_CONV),
        "wout_a": w_out[l][:D_ATTN].astype(BF16),
        "wout_c": w_out[l][D_ATTN:].astype(BF16),
        "norm2_g": norm2_g[l].reshape(1, D_MODEL),
        "w_router_t": w_router_t,
        "b_router_t": b_router_t,
        "w_gate": w_e_gate[l].astype(BF16),
        "w_up": w_e_up[l].astype(BF16),
        "w_down": w_e_down[l].astype(BF16),
        "final_g": final_g.reshape(1, D_MODEL),
    }
    return (_trunk(x_prompt, p), _trunk(x_sample, p))
```

```python
import functools

import jax
import jax.numpy as jnp
from jax import lax
from jax.experimental import pallas as pl
from jax.experimental.pallas import tpu as pltpu

F32 = jnp.float32
BF16 = jnp.bfloat16
I32 = jnp.int32

D_MODEL = 1024
GRID_W = 64
D_ATTN = 512
D_CONV = 512
HEAD_DIM = 64
N_HEADS = 8
NA_ROWS = 8
NA_COLS = 16
CONV_WIDTH = 31
N_GROUPS = 4
EPG = 8
N_EXPERTS = 32
D_EXPERT = 512
EPS = 1e-6

LANES = 128
HEAD_PAIRS = N_HEADS * HEAD_DIM // LANES
NA_KEYS = NA_ROWS * GRID_W
MASKED = -1e30
TOKEN_TILE = 512
EXPERT_BLOCK = 512
CONV_TILE = 64
CONV_SEGMENT = 512
CONV_HALO = 16
VMEM_LIMIT = 56 * 1024 * 1024


def _rms(x, g):
    return x * lax.rsqrt(jnp.mean(x * x, axis=-1, keepdims=True) + EPS) * g


def _in_proj_body(x_ref, g_ref, wqkv_ref, wc_ref, qkv_ref, uc_ref):
    h = _rms(x_ref[...], g_ref[...]).astype(BF16)
    qkv_ref[...] = jnp.dot(h, wqkv_ref[...], preferred_element_type=F32).astype(BF16)
    uc_ref[...] = jnp.dot(h, wc_ref[...], preferred_element_type=F32).astype(BF16)


def _in_proj(x2, g, wqkv, wc):
    T = x2.shape[0]
    tm = TOKEN_TILE
    return pl.pallas_call(
        _in_proj_body,
        grid=(T // tm,),
        in_specs=[
            pl.BlockSpec((tm, D_MODEL), lambda i: (i, 0)),
            pl.BlockSpec((1, D_MODEL), lambda i: (0, 0)),
            pl.BlockSpec((D_MODEL, 3 * D_ATTN), lambda i: (0, 0)),
            pl.BlockSpec((D_MODEL, 2 * D_CONV), lambda i: (0, 0)),
        ],
        out_specs=[
            pl.BlockSpec((tm, 3 * D_ATTN), lambda i: (i, 0)),
            pl.BlockSpec((tm, 2 * D_CONV), lambda i: (i, 0)),
        ],
        out_shape=[
            jax.ShapeDtypeStruct((T, 3 * D_ATTN), BF16),
            jax.ShapeDtypeStruct((T, 2 * D_CONV), BF16),
        ],
        compiler_params=pltpu.CompilerParams(
            dimension_semantics=("arbitrary",), vmem_limit_bytes=VMEM_LIMIT),
        name="in_proj",
    )(x2, g, wqkv, wc)


def _na_bias_table(rpb):
    c = jnp.arange(GRID_W)
    col_start = jnp.clip(c - NA_COLS // 2, 0, GRID_W - NA_COLS)
    cp = jnp.arange(GRID_W)
    valid = (cp[None, :] >= col_start[:, None]) & (cp[None, :] < col_start[:, None] + NA_COLS)
    col_off = cp[None, :] - c[:, None] + (NA_COLS - 1)
    sel = (col_off[None] == jnp.arange(2 * NA_COLS - 1)[:, None, None]) & valid[None]
    a = jnp.einsum("hrd,dcx->hrcx", rpb, sel.astype(F32), precision=lax.Precision.HIGHEST)
    a = jnp.where(valid[None, None], a, MASKED)
    t = jnp.stack([a[:, NA_ROWS - 1 - p:2 * NA_ROWS - 1 - p] for p in range(NA_ROWS)], axis=1)
    t = t.transpose(0, 1, 3, 2, 4)
    t = t.reshape(HEAD_PAIRS, 2, NA_ROWS, GRID_W, NA_KEYS)
    return t.transpose(0, 2, 1, 3, 4).reshape(HEAD_PAIRS, NA_ROWS, LANES, NA_KEYS).astype(F32)


def _attn_body(q_ref, k_ref, v_ref, bias_ref, o_ref, s_ref, p_ref, l_ref, *, rows, rows_per_step):
    rb = pl.program_id(2)
    first_head = lax.broadcasted_iota(I32, (GRID_W, LANES), 1) < HEAD_DIM
    zero = jnp.zeros((GRID_W, LANES), BF16)

    def window(rr):
        r = rb * rows_per_step + rr
        r_start = jnp.clip(r - NA_ROWS // 2, 0, rows - NA_ROWS)
        return r - r_start, pl.multiple_of(r_start * GRID_W, GRID_W)

    for rr in range(rows_per_step):
        p, k0 = window(rr)
        q = q_ref[rr * GRID_W:(rr + 1) * GRID_W, :] * jnp.asarray(HEAD_DIM ** -0.5, BF16)
        qbd = jnp.concatenate([jnp.where(first_head, q, zero), jnp.where(first_head, zero, q)], axis=0)
        ks = k_ref[pl.ds(k0, NA_KEYS), :]
        s = lax.dot_general(qbd, ks, (((1,), (1,)), ((), ())), preferred_element_type=F32)
        s_ref[rr] = s + bias_ref[p]
    for rr in range(rows_per_step):
        s = s_ref[rr]
        e = jnp.exp(s - jnp.max(s, axis=-1, keepdims=True))
        l_ref[rr] = jnp.broadcast_to(1.0 / jnp.sum(e, axis=-1, keepdims=True), (LANES, LANES))
        p_ref[rr] = e.astype(BF16)
    for rr in range(rows_per_step):
        _, k0 = window(rr)
        vs = v_ref[pl.ds(k0, NA_KEYS), :]
        o = jnp.dot(p_ref[rr], vs, preferred_element_type=F32) * l_ref[rr]
        out = jnp.where(first_head, o[:GRID_W], o[GRID_W:])
        o_ref[rr * GRID_W:(rr + 1) * GRID_W, :] = out.astype(BF16)


def _attention(qkv, bias):
    B, L, _ = qkv.shape
    rows = L // GRID_W
    rows_per_step = 16
    tq = rows_per_step * GRID_W
    body = functools.partial(_attn_body, rows=rows, rows_per_step=rows_per_step)
    return pl.pallas_call(
        body,
        grid=(HEAD_PAIRS, B, L // tq),
        in_specs=[
            pl.BlockSpec((None, tq, LANES), lambda hp, b, rb: (b, rb, hp)),
            pl.BlockSpec((None, L, LANES), lambda hp, b, rb: (b, 0, HEAD_PAIRS + hp)),
            pl.BlockSpec((None, L, LANES), lambda hp, b, rb: (b, 0, 2 * HEAD_PAIRS + hp)),
            pl.BlockSpec((None, NA_ROWS, LANES, NA_KEYS), lambda hp, b, rb: (hp, 0, 0, 0)),
        ],
        out_specs=pl.BlockSpec((None, tq, LANES), lambda hp, b, rb: (b, rb, hp)),
        out_shape=jax.ShapeDtypeStruct((B, L, D_ATTN), BF16),
        scratch_shapes=[
            pltpu.VMEM((rows_per_step, LANES, NA_KEYS), F32),
            pltpu.VMEM((rows_per_step, LANES, NA_KEYS), BF16),
            pltpu.VMEM((rows_per_step, LANES, LANES), F32),
        ],
        compiler_params=pltpu.CompilerParams(
            dimension_semantics=("arbitrary", "arbitrary", "arbitrary"),
            vmem_limit_bytes=VMEM_LIMIT),
        name="na_attention",
    )(qkv, qkv, qkv, bias)


def _glu(uc_rows):
    a = uc_rows[:, 0:D_CONV].astype(F32)
    gt = uc_rows[:, D_CONV:2 * D_CONV].astype(F32)
    return a * jax.nn.sigmoid(gt)


def _conv_body(uc_ref, w_ref, b_ref, lng_ref, lnb_ref, og_ref, o_ref, zs_ref, cbuf_ref, *, seg):
    sg = pl.program_id(1)
    n_seg = pl.num_programs(1)
    seg0 = pl.multiple_of(sg * seg, seg)
    halo = CONV_HALO
    n_lane_tiles = D_CONV // LANES

    def put_z(u0, n, z):
        for j in range(n_lane_tiles):
            zs_ref[0, j, pl.ds(u0, n), :] = z[:, j * LANES:(j + 1) * LANES]

    glu_rows = 128

    def glu(i, carry):
        u0 = pl.multiple_of(i * glu_rows, glu_rows)
        put_z(halo + u0, glu_rows, _glu(uc_ref[pl.ds(seg0 + u0, glu_rows), :]))
        return carry

    lax.fori_loop(0, seg // glu_rows, glu, 0)

    @pl.when(sg > 0)
    def _():
        put_z(0, halo, _glu(uc_ref[pl.ds(seg0 - halo, halo), :]))

    @pl.when(sg == 0)
    def _():
        put_z(0, halo, jnp.zeros((halo, D_CONV), F32))

    @pl.when(sg < n_seg - 1)
    def _():
        put_z(halo + seg, halo, _glu(uc_ref[pl.ds(seg0 + seg, halo), :]))

    @pl.when(sg == n_seg - 1)
    def _():
        put_z(halo + seg, halo, jnp.zeros((halo, D_CONV), F32))

    def shift_rows(j, u0, n):
        x = zs_ref[0, j, pl.ds(u0, n + 8), :]
        for s in range(1, 8):
            zs_ref[s, j, pl.ds(u0, n), :] = x[s:s + n]

    shift_chunk = 64
    chunks = seg // shift_chunk

    def shift(i, carry):
        shift_rows(i // chunks, pl.multiple_of((i % chunks) * shift_chunk, shift_chunk), shift_chunk)
        return carry

    lax.fori_loop(0, n_lane_tiles * chunks, shift, 0)
    for j in range(n_lane_tiles):
        shift_rows(j, seg, 2 * halo - 8)

    tt = CONV_TILE
    tiles = seg // tt
    first = halo - CONV_WIDTH // 2

    def taps(i, carry):
        j = i // tiles
        t0 = pl.multiple_of((i % tiles) * tt, tt)
        acc = jnp.broadcast_to(b_ref[j], (tt, LANES))
        for k in range(CONV_WIDTH):
            off = k + first
            acc = acc + w_ref[j, k:k + 1, :] * zs_ref[off % 8, j, pl.ds(t0 + 8 * (off // 8), tt), :]
        cbuf_ref[j, pl.ds(t0, tt), :] = acc
        return carry

    lax.fori_loop(0, n_lane_tiles * tiles, taps, 0, unroll=2)

    norm_rows = 128

    def norm(i, carry):
        t0 = pl.multiple_of(i * norm_rows, norm_rows)
        y = jnp.concatenate([cbuf_ref[j, pl.ds(t0, norm_rows), :] for j in range(n_lane_tiles)], axis=1)
        mu = jnp.mean(y, axis=-1, keepdims=True)
        yc = y - mu
        var = jnp.mean(yc * yc, axis=-1, keepdims=True)
        yn = yc * lax.rsqrt(var + EPS) * lng_ref[...] + lnb_ref[...]
        sw = yn * jax.nn.sigmoid(yn)
        o_ref[pl.ds(t0, norm_rows), :] = _rms(sw, og_ref[...]).astype(BF16)
        return carry

    lax.fori_loop(0, seg // norm_rows, norm, 0, unroll=2)


def _conv(uc, w, b, lng, lnb, og):
    B, L, _ = uc.shape
    seg = CONV_SEGMENT
    n_lane_tiles = D_CONV // LANES
    vec = pl.BlockSpec((1, D_CONV), lambda i, s: (0, 0))
    w = w.reshape(CONV_WIDTH + 1, n_lane_tiles, LANES).transpose(1, 0, 2)
    b = b.reshape(n_lane_tiles, 1, LANES)
    return pl.pallas_call(
        functools.partial(_conv_body, seg=seg),
        grid=(B, L // seg),
        in_specs=[
            pl.BlockSpec((None, L, 2 * D_CONV), lambda i, s: (i, 0, 0)),
            pl.BlockSpec((n_lane_tiles, CONV_WIDTH + 1, LANES), lambda i, s: (0, 0, 0)),
            pl.BlockSpec((n_lane_tiles, 1, LANES), lambda i, s: (0, 0, 0)),
            vec, vec, vec,
        ],
        out_specs=pl.BlockSpec((None, seg, D_CONV), lambda i, s: (i, s, 0)),
        out_shape=jax.ShapeDtypeStruct((B, L, D_CONV), BF16),
        scratch_shapes=[
            pltpu.VMEM((8, n_lane_tiles, seg + 2 * CONV_HALO, LANES), F32),
            pltpu.VMEM((n_lane_tiles, seg, LANES), F32),
        ],
        compiler_params=pltpu.CompilerParams(
            dimension_semantics=("arbitrary", "arbitrary"), vmem_limit_bytes=VMEM_LIMIT),
        name="conformer_conv",
    )(uc, w, b, lng, lnb, og)


GROUP_ROWS = 8
TILE_GROUPS = (2 * TOKEN_TILE + (GROUP_ROWS - 1) * N_EXPERTS + GROUP_ROWS - 1) // GROUP_ROWS
TILE_GROUPS = (TILE_GROUPS + 15) // 16 * 16
TILE_SLOTS = TILE_GROUPS * GROUP_ROWS
BLOCK_GROUPS = EXPERT_BLOCK // GROUP_ROWS
ROUTER_ROWS = 128
EXPERT_ROW0 = 8
COL_P0, COL_P1, COL_W0, COL_W1 = range(4)


U32 = jnp.uint32
PACKED = D_MODEL // 2


def _pack_bf16_pairs(x):
    bits = pltpu.bitcast(x, U32)
    return bits[:, PACKED:] | (bits[:, :PACKED] >> 16)


def _unpack_bf16_pairs(words):
    lo = pltpu.bitcast(words << 16, F32)
    hi = pltpu.bitcast(words & jnp.uint32(0xFFFF0000), F32)
    return jnp.concatenate([lo, hi], axis=1).astype(BF16)


def _sorted_out_proj_body(x_ref, a_ref, c_ref, ag_ref, wa_ref, wc_ref, n2g_ref, wrt_ref, brt_ref,
                          x1_ref, hs_ref, route_ref, grp_ref, *, tm):
    an = _rms(a_ref[...].astype(F32), ag_ref[...]).astype(BF16)
    y = jnp.dot(an, wa_ref[...], preferred_element_type=F32)
    y = y + jnp.dot(c_ref[...], wc_ref[...], preferred_element_type=F32)
    x1 = x_ref[...] + y
    x1_ref[...] = x1
    h2 = _rms(x1, n2g_ref[...])

    h_hi = h2.astype(BF16)
    h_lo = (h2 - h_hi.astype(F32)).astype(BF16)
    nt = (((1,), (1,)), ((), ()))
    split = (lax.dot_general(wrt_ref[...], h_hi, nt, preferred_element_type=F32)
             + lax.dot_general(wrt_ref[...], h_lo, nt, preferred_element_type=F32))
    logits = split[:ROUTER_ROWS] + split[ROUTER_ROWS:] + brt_ref[...]

    gtop = logits[0:1]
    gsel = jnp.zeros((1, tm), F32)
    for g in range(1, N_GROUPS):
        cand = logits[g:g + 1]
        better = cand > gtop
        gsel = jnp.where(better, float(g), gsel)
        gtop = jnp.where(better, cand, gtop)
    denom = jnp.zeros((1, tm), F32)
    for g in range(N_GROUPS):
        denom = denom + jnp.exp(logits[g:g + 1] - gtop)
    p_g = 1.0 / denom

    el = logits[EXPERT_ROW0:EXPERT_ROW0 + EPG]
    for g in range(1, N_GROUPS):
        el = jnp.where(gsel == float(g), logits[EXPERT_ROW0 + g * EPG:EXPERT_ROW0 + (g + 1) * EPG], el)
    ninf = jnp.full((1, tm), -jnp.inf, F32)
    v0, v1 = ninf, ninf
    i0 = jnp.zeros((1, tm), F32)
    i1 = jnp.zeros((1, tm), F32)
    for j in range(EPG):
        cand = el[j:j + 1]
        gt0 = cand > v0
        gt1 = cand > v1
        v1 = jnp.where(gt0, v0, jnp.where(gt1, cand, v1))
        i1 = jnp.where(gt0, i0, jnp.where(gt1, float(j), i1))
        v0 = jnp.where(gt0, cand, v0)
        i0 = jnp.where(gt0, float(j), i0)
    t = jnp.exp(v1 - v0)
    w0 = p_g / (1.0 + t)
    w1 = p_g * t / (1.0 + t)
    e0 = gsel * EPG + i0
    e1 = gsel * EPG + i1

    expert = lax.broadcasted_iota(I32, (N_EXPERTS, tm), 0).astype(F32)
    hit0 = expert == e0
    hit1 = expert == e1
    onehot = jnp.where(hit0 | hit1, 1.0, 0.0).astype(BF16)
    ri = lax.broadcasted_iota(I32, (tm, tm), 0)
    ci = lax.broadcasted_iota(I32, (tm, tm), 1)
    upper = jnp.where(ri <= ci, 1.0, 0.0).astype(BF16)
    cum = jnp.dot(onehot, upper, preferred_element_type=F32)
    count = jnp.dot(onehot, jnp.ones((tm, LANES), BF16), preferred_element_type=F32)
    groups = jnp.floor((count + (GROUP_ROWS - 1)) * (1.0 / GROUP_ROWS))
    er = lax.broadcasted_iota(I32, (N_EXPERTS, N_EXPERTS), 0)
    ec = lax.broadcasted_iota(I32, (N_EXPERTS, N_EXPERTS), 1)
    below = jnp.where(ec < er, 1.0, 0.0).astype(BF16)
    run_start = jnp.dot(below, groups.astype(BF16), preferred_element_type=F32) * GROUP_ROWS
    run_start = jnp.concatenate([run_start] * (tm // LANES), axis=1)
    pos = run_start + cum - 1.0
    p0 = jnp.sum(jnp.where(hit0, pos, 0.0), axis=0, keepdims=True)
    p1 = jnp.sum(jnp.where(hit1, pos, 0.0), axis=0, keepdims=True)
    grp_ref[...] = groups

    slot = lax.broadcasted_iota(I32, (TILE_SLOTS, tm), 0).astype(F32)
    perm = jnp.where((slot == p0) | (slot == p1), 1.0, 0.0).astype(BF16)
    hs_ref[...] = _pack_bf16_pairs(jnp.dot(perm, h_hi, preferred_element_type=F32))

    rows = jnp.concatenate([p0, p1, w0, w1, jnp.zeros((ROUTER_ROWS - 4, tm), F32)], axis=0)
    route_ref[...] = rows.T


def _sorted_out_proj(x2, a2, c2, ag, wa, wc, n2g, wrt, brt):
    T = x2.shape[0]
    tm = TOKEN_TILE
    nt = T // tm
    full = lambda shape: pl.BlockSpec(shape, lambda i: (0,) * len(shape))
    return pl.pallas_call(
        functools.partial(_sorted_out_proj_body, tm=tm),
        grid=(nt,),
        in_specs=[
            pl.BlockSpec((tm, D_MODEL), lambda i: (i, 0)),
            pl.BlockSpec((tm, D_ATTN), lambda i: (i, 0)),
            pl.BlockSpec((tm, D_CONV), lambda i: (i, 0)),
            full((1, D_ATTN)),
            full((D_ATTN, D_MODEL)),
            full((D_CONV, D_MODEL)),
            full((1, D_MODEL)),
            full((2 * ROUTER_ROWS, D_MODEL)),
            full((ROUTER_ROWS, 1)),
        ],
        out_specs=[
            pl.BlockSpec((tm, D_MODEL), lambda i: (i, 0)),
            pl.BlockSpec((TILE_SLOTS, PACKED), lambda i: (i, 0)),
            pl.BlockSpec((tm, LANES), lambda i: (i, 0)),
            pl.BlockSpec((None, N_EXPERTS, LANES), lambda i: (i, 0, 0)),
        ],
        out_shape=[
            jax.ShapeDtypeStruct((T, D_MODEL), F32),
            jax.ShapeDtypeStruct((nt * TILE_SLOTS, PACKED), U32),
            jax.ShapeDtypeStruct((T, LANES), F32),
            jax.ShapeDtypeStruct((nt, N_EXPERTS, LANES), F32),
        ],
        compiler_params=pltpu.CompilerParams(
            dimension_semantics=("arbitrary",), vmem_limit_bytes=VMEM_LIMIT),
        name="out_proj_router",
    )(x2, a2, c2, ag, wa, wc, n2g, wrt, brt)


def _group_copies(src_ref, hs_hbm, ys_hbm, xbuf, ybuf, gsem, ssem, block, slot, trash0, to_trash):
    def group_index(i):
        return src_ref[block * BLOCK_GROUPS + i]

    def rows_of(i):
        start = i * GROUP_ROWS
        return pl.ds(start if isinstance(i, int) else pl.multiple_of(start, GROUP_ROWS), GROUP_ROWS)

    def gather(i):
        g = jnp.maximum(group_index(i), 0)
        return pltpu.make_async_copy(hs_hbm.at[g], xbuf.at[slot, rows_of(i)], gsem.at[slot])

    def scatter(i):
        g = group_index(i)
        g = jnp.where((g < 0) | to_trash, trash0 + slot * BLOCK_GROUPS + i, g)
        return pltpu.make_async_copy(ybuf.at[slot, rows_of(i)], ys_hbm.at[g], ssem.at[slot])

    return gather, scatter


def _sorted_expert_body(be_ref, nu_ref, src_ref, tg_ref, hs_hbm, wg_ref, wu_ref, wd_ref, ys_hbm,
                        xbuf, ybuf, zero_ref, gsem, ssem, zsem, *, n_tiles, trash0):
    del be_ref
    b = pl.program_id(0)
    n_used = nu_ref[0]
    slot = b % 2

    @pl.when(b == 0)
    def _():
        zero_ref[...] = jnp.zeros_like(zero_ref)

        def zero_copy(g):
            return pltpu.make_async_copy(zero_ref, ys_hbm.at[g], zsem)

        def over_unused(fn):
            def tile(c, carry):
                def group(g, carry2):
                    fn(c * TILE_GROUPS + g)
                    return carry2
                return lax.fori_loop(tg_ref[c], TILE_GROUPS, group, carry)
            lax.fori_loop(0, n_tiles, tile, 0)

            def trash(i, carry):
                fn(trash0 + i)
                return carry
            lax.fori_loop(0, 2 * BLOCK_GROUPS, trash, 0)

        over_unused(lambda g: zero_copy(g).start())
        over_unused(lambda g: zero_copy(g).wait())

    def for_groups(fn):
        def step(i, carry):
            fn(i)
            return carry
        lax.fori_loop(0, BLOCK_GROUPS, step, 0, unroll=8)

    def copies(block, slot_, to_trash=False):
        return _group_copies(src_ref, hs_hbm, ys_hbm, xbuf, ybuf, gsem, ssem, block, slot_, trash0,
                             to_trash)

    def wait_gathers(slot_):
        gather, _ = copies(0, slot_)
        for_groups(lambda i: gather(i).wait())

    def wait_scatters(slot_):
        _, scatter = copies(0, slot_)
        for_groups(lambda i: scatter(i).wait())

    @pl.when((b == 0) & (n_used > 0))
    def _():
        ybuf[...] = jnp.zeros_like(ybuf)
        gather, _ = copies(0, 0)
        for_groups(lambda i: gather(i).start())

    @pl.when((b >= 1) & (b <= n_used))
    def _():
        wait_scatters(slot)

    @pl.when(b < n_used)
    def _():
        wait_gathers(slot)
        gather, _ = copies(jnp.minimum(b + 1, n_used - 1), 1 - slot)
        _, scatter = copies(jnp.maximum(b - 1, 0), 1 - slot, to_trash=b == 0)
        for i in range(BLOCK_GROUPS):
            gather(i).start()
            scatter(i).start()
        x = _unpack_bf16_pairs(xbuf[slot])
        g = jnp.dot(x, wg_ref[...], preferred_element_type=F32)
        u = jnp.dot(x, wu_ref[...], preferred_element_type=F32)
        hid = (g * jax.nn.sigmoid(g) * u).astype(BF16)
        y = jnp.dot(hid, wd_ref[...], preferred_element_type=F32)
        ybuf[slot] = _pack_bf16_pairs(y.astype(BF16).astype(F32))

    @pl.when((b == n_used) & (n_used > 0))
    def _():
        wait_gathers(slot)
        _, scatter = copies(b - 1, 1 - slot)
        for_groups(lambda i: scatter(i).start())
        wait_scatters(1 - slot)


def _sorted_experts(block_e, n_used, src, tile_groups, hs3, wg, wu, wd):
    n_groups = hs3.shape[0]
    n_tiles = tile_groups.shape[0]
    n_blocks = block_e.shape[0]
    blk = EXPERT_BLOCK
    grid_spec = pltpu.PrefetchScalarGridSpec(
        num_scalar_prefetch=4,
        grid=(n_blocks,),
        in_specs=[
            pl.BlockSpec(memory_space=pl.ANY),
            pl.BlockSpec((None, D_MODEL, D_EXPERT), lambda b, be, nu, src, tg: (be[b], 0, 0)),
            pl.BlockSpec((None, D_MODEL, D_EXPERT), lambda b, be, nu, src, tg: (be[b], 0, 0)),
            pl.BlockSpec((None, D_EXPERT, D_MODEL), lambda b, be, nu, src, tg: (be[b], 0, 0)),
        ],
        out_specs=pl.BlockSpec(memory_space=pl.ANY),
        scratch_shapes=[
            pltpu.VMEM((2, blk, PACKED), U32),
            pltpu.VMEM((2, blk, PACKED), U32),
            pltpu.VMEM((GROUP_ROWS, PACKED), U32),
            pltpu.SemaphoreType.DMA((2,)),
            pltpu.SemaphoreType.DMA((2,)),
            pltpu.SemaphoreType.DMA(()),
        ],
    )
    return pl.pallas_call(
        functools.partial(_sorted_expert_body, n_tiles=n_tiles, trash0=n_groups),
        grid_spec=grid_spec,
        out_shape=jax.ShapeDtypeStruct((n_groups + 2 * BLOCK_GROUPS, GROUP_ROWS, PACKED), U32),
        compiler_params=pltpu.CompilerParams(
            dimension_semantics=("arbitrary",), vmem_limit_bytes=VMEM_LIMIT),
        name="moe_experts",
    )(block_e, n_used, src, tile_groups, hs3, wg, wu, wd)


def _sorted_combine_body(x1_ref, route_ref, fg_ref, ys_ref, o_ref, *, tm):
    ys = _unpack_bf16_pairs(ys_ref[...])
    route = route_ref[...]
    slot = lax.broadcasted_iota(I32, (tm, TILE_SLOTS), 1).astype(F32)
    gate = (jnp.where(slot == route[:, COL_P0:COL_P0 + 1], route[:, COL_W0:COL_W0 + 1], 0.0)
            + jnp.where(slot == route[:, COL_P1:COL_P1 + 1], route[:, COL_W1:COL_W1 + 1], 0.0))
    moe = jnp.dot(gate.astype(BF16), ys, preferred_element_type=F32)
    o_ref[...] = _rms(x1_ref[...] + moe, fg_ref[...])


def _sorted_combine(x1, route, fg, ys2):
    T = x1.shape[0]
    tm = TOKEN_TILE
    return pl.pallas_call(
        functools.partial(_sorted_combine_body, tm=tm),
        grid=(T // tm,),
        in_specs=[
            pl.BlockSpec((tm, D_MODEL), lambda i: (i, 0)),
            pl.BlockSpec((tm, LANES), lambda i: (i, 0)),
            pl.BlockSpec((1, D_MODEL), lambda i: (0, 0)),
            pl.BlockSpec((TILE_SLOTS, PACKED), lambda i: (i, 0)),
        ],
        out_specs=pl.BlockSpec((tm, D_MODEL), lambda i: (i, 0)),
        out_shape=jax.ShapeDtypeStruct((T, D_MODEL), F32),
        compiler_params=pltpu.CompilerParams(
            dimension_semantics=("arbitrary",), vmem_limit_bytes=VMEM_LIMIT),
        name="moe_combine",
    )(x1, route, fg, ys2)


def _moe_plan(groups):
    nt = groups.shape[0]
    run_start = jnp.cumsum(groups, axis=1) - groups
    cum_tiles = jnp.cumsum(groups, axis=0)
    total = cum_tiles[-1]
    blocks = (total + BLOCK_GROUPS - 1) // BLOCK_GROUPS
    blk_end = jnp.cumsum(blocks)
    blk_start = blk_end - blocks
    n_blocks = (2 * TOKEN_TILE + (GROUP_ROWS - 1) * N_EXPERTS) * nt // EXPERT_BLOCK + N_EXPERTS + 1
    b = jnp.arange(n_blocks, dtype=I32)
    block_e = jnp.minimum(jnp.sum(b[:, None] >= blk_end[None, :], axis=1), N_EXPERTS - 1).astype(I32)
    n_used = blk_end[-1:].astype(I32)

    sg = jnp.arange(n_blocks * BLOCK_GROUPS, dtype=I32)
    sb = sg // BLOCK_GROUPS
    sg_e = jnp.broadcast_to(block_e[:, None], (n_blocks, BLOCK_GROUPS)).reshape(-1)
    onehot_e = (sg_e[:, None] == jnp.arange(N_EXPERTS, dtype=I32)[None, :]).astype(F32)

    def pick(table):
        return jnp.dot(onehot_e, table.astype(F32), precision=lax.Precision.HIGHEST).astype(I32)

    j = sg - pick(blk_start[:, None])[:, 0] * BLOCK_GROUPS
    valid = (sb < n_used[0]) & (j < pick(total[:, None])[:, 0])
    cum_e = pick(cum_tiles.T)
    tile = jnp.minimum(jnp.sum(cum_e <= j[:, None], axis=1), nt - 1)
    tile_hot = tile[:, None] == jnp.arange(nt, dtype=I32)[None, :]
    before = jnp.sum(jnp.where(tile_hot, cum_e - pick(groups.T), 0), axis=1)
    start = jnp.sum(jnp.where(tile_hot, pick(run_start.T), 0), axis=1)
    src = jnp.where(valid, tile * TILE_GROUPS + start + (j - before), -1).astype(I32)
    tile_groups = jnp.sum(groups, axis=1).astype(I32)
    return block_e, n_used, src, tile_groups


def _trunk(x, p):
    B, L, _ = x.shape
    T = B * L
    x2 = x.reshape(T, D_MODEL)
    qkv, uc = _in_proj(x2, p["norm1_g"], p["wqkv"], p["wconv"])
    a = _attention(qkv.reshape(B, L, 3 * D_ATTN), p["na_bias"])
    c = _conv(uc.reshape(B, L, 2 * D_CONV), p["conv_w"], p["conv_b"], p["conv_ln_g"],
              p["conv_ln_b"], p["conv_out_g"])
    x1, hs, route, grp = _sorted_out_proj(
        x2, a.reshape(T, D_ATTN), c.reshape(T, D_CONV), p["attn_out_g"], p["wout_a"],
        p["wout_c"], p["norm2_g"], p["w_router_t"], p["b_router_t"])
    block_e, n_used, src, tile_groups = _moe_plan(grp[:, :, 0].astype(I32))
    hs3 = hs.reshape(hs.shape[0] // GROUP_ROWS, GROUP_ROWS, PACKED)
    ys3 = _sorted_experts(block_e, n_used, src, tile_groups, hs3, p["w_gate"], p["w_up"], p["w_down"])
    ys2 = ys3.reshape(ys3.shape[0] * GROUP_ROWS, PACKED)
    out = _sorted_combine(x1, route, p["final_g"], ys2)
    return out.reshape(B, L, D_MODEL)


def kernel(x_prompt, x_sample, norm1_g, w_in, rpb, attn_out_g, conv_w, conv_b, conv_ln_g,
           conv_ln_b, conv_out_g, w_out, norm2_g, w_group, b_group, w_expert, b_expert,
           w_e_gate, w_e_up, w_e_down, final_g):
    l = 0
    w_router_t = jnp.zeros((ROUTER_ROWS, D_MODEL), F32)
    w_router_t = w_router_t.at[:N_GROUPS].set(w_group[l].T)
    w_router_t = w_router_t.at[EXPERT_ROW0:EXPERT_ROW0 + N_EXPERTS].set(
        w_expert[l].transpose(0, 2, 1).reshape(N_EXPERTS, D_MODEL))
    w_router_hi = w_router_t.astype(BF16)
    w_router_lo = (w_router_t - w_router_hi.astype(F32)).astype(BF16)
    w_router_t = jnp.concatenate([w_router_hi, w_router_lo], axis=0)
    b_router_t = jnp.zeros((ROUTER_ROWS,), F32)
    b_router_t = b_router_t.at[:N_GROUPS].set(b_group[l])
    b_router_t = b_router_t.at[EXPERT_ROW0:EXPERT_ROW0 + N_EXPERTS].set(b_expert[l].reshape(N_EXPERTS))
    b_router_t = b_router_t.reshape(ROUTER_ROWS, 1)
    p = {
        "norm1_g": norm1_g[l].reshape(1, D_MODEL),
        "wqkv": w_in[l][:, :3 * D_ATTN].astype(BF16),
        "wconv": w_in[l][:, 3 * D_ATTN:].astype(BF16),
        "na_bias": _na_bias_table(rpb[l]),
        "attn_out_g": attn_out_g[l].reshape(1, D_ATTN),
        "conv_w": jnp.pad(conv_w[l], ((0, 1), (0, 0))),
        "conv_b": conv_b[l].reshape(1, D_CONV),
        "conv_ln_g": conv_ln_g[l].reshape(1, D_CONV),
        "conv_ln_b": conv_ln_b[l].reshape(1, D_CONV),
        "conv_out_g": conv_out_g[l].reshape(1, D_CONV),
        "wout_a": w_out[l][:D_ATTN].astype(BF16),
        "wout_c": w_out[l][D_ATTN:].astype(BF16),
        "norm2_g": norm2_g[l].reshape(1, D_MODEL),
        "w_router_t": w_router_t,
        "b_router_t": b_router_t,
        "w_gate": w_e_gate[l].astype(BF16),
        "w_up": w_e_up[l].astype(BF16),
        "w_down": w_e_down[l].astype(BF16),
        "final_g": final_g.reshape(1, D_MODEL),
    }
    return (_trunk(x_prompt, p), _trunk(x_sample, p))
```

```python
import functools

import jax
import jax.numpy as jnp
from jax import lax
from jax.experimental import pallas as pl
from jax.experimental.pallas import tpu as pltpu

F32 = jnp.float32
BF16 = jnp.bfloat16
I32 = jnp.int32

D_MODEL = 1024
GRID_W = 64
D_ATTN = 512
D_CONV = 512
HEAD_DIM = 64
N_HEADS = 8
NA_ROWS = 8
NA_COLS = 16
CONV_WIDTH = 31
N_GROUPS = 4
EPG = 8
N_EXPERTS = 32
D_EXPERT = 512
EPS = 1e-6

LANES = 128
HEAD_PAIRS = N_HEADS * HEAD_DIM // LANES
NA_KEYS = NA_ROWS * GRID_W
MASKED = -1e30
TOKEN_TILE = 512
EXPERT_BLOCK = 512
CONV_TILE = 64
CONV_SEGMENT = 512
CONV_HALO = 16
VMEM_LIMIT = 56 * 1024 * 1024


def _rms(x, g):
    return x * lax.rsqrt(jnp.mean(x * x, axis=-1, keepdims=True) + EPS) * g


def _in_proj_body(x_ref, g_ref, wqkv_ref, wc_ref, qkv_ref, uc_ref):
    h = _rms(x_ref[...], g_ref[...]).astype(BF16)
    qkv_ref[...] = jnp.dot(h, wqkv_ref[...], preferred_element_type=F32).astype(BF16)
    uc_ref[...] = jnp.dot(h, wc_ref[...], preferred_element_type=F32).astype(BF16)


def _in_proj(x2, g, wqkv, wc):
    T = x2.shape[0]
    tm = TOKEN_TILE
    return pl.pallas_call(
        _in_proj_body,
        grid=(T // tm,),
        in_specs=[
            pl.BlockSpec((tm, D_MODEL), lambda i: (i, 0)),
            pl.BlockSpec((1, D_MODEL), lambda i: (0, 0)),
            pl.BlockSpec((D_MODEL, 3 * D_ATTN), lambda i: (0, 0)),
            pl.BlockSpec((D_MODEL, 2 * D_CONV), lambda i: (0, 0)),
        ],
        out_specs=[
            pl.BlockSpec((tm, 3 * D_ATTN), lambda i: (i, 0)),
            pl.BlockSpec((tm, 2 * D_CONV), lambda i: (i, 0)),
        ],
        out_shape=[
            jax.ShapeDtypeStruct((T, 3 * D_ATTN), BF16),
            jax.ShapeDtypeStruct((T, 2 * D_CONV), BF16),
        ],
        compiler_params=pltpu.CompilerParams(
            dimension_semantics=("arbitrary",), vmem_limit_bytes=VMEM_LIMIT),
        name="in_proj",
    )(x2, g, wqkv, wc)


def _na_bias_table(rpb):
    c = jnp.arange(GRID_W)
    col_start = jnp.clip(c - NA_COLS // 2, 0, GRID_W - NA_COLS)
    cp = jnp.arange(GRID_W)
    valid = (cp[None, :] >= col_start[:, None]) & (cp[None, :] < col_start[:, None] + NA_COLS)
    col_off = cp[None, :] - c[:, None] + (NA_COLS - 1)
    sel = (col_off[None] == jnp.arange(2 * NA_COLS - 1)[:, None, None]) & valid[None]
    a = jnp.einsum("hrd,dcx->hrcx", rpb, sel.astype(F32), precision=lax.Precision.HIGHEST)
    a = jnp.where(valid[None, None], a, MASKED)
    t = jnp.stack([a[:, NA_ROWS - 1 - p:2 * NA_ROWS - 1 - p] for p in range(NA_ROWS)], axis=1)
    t = t.transpose(0, 1, 3, 2, 4)
    t = t.reshape(HEAD_PAIRS, 2, NA_ROWS, GRID_W, NA_KEYS)
    return t.transpose(0, 2, 1, 3, 4).reshape(HEAD_PAIRS, NA_ROWS, LANES, NA_KEYS).astype(F32)


def _attn_body(q_ref, k_ref, v_ref, bias_ref, o_ref, s_ref, p_ref, l_ref, *, rows, rows_per_step):
    rb = pl.program_id(2)
    first_head = lax.broadcasted_iota(I32, (GRID_W, LANES), 1) < HEAD_DIM
    zero = jnp.zeros((GRID_W, LANES), BF16)

    def window(rr):
        r = rb * rows_per_step + rr
        r_start = jnp.clip(r - NA_ROWS // 2, 0, rows - NA_ROWS)
        return r - r_start, pl.multiple_of(r_start * GRID_W, GRID_W)

    for rr in range(rows_per_step):
        p, k0 = window(rr)
        q = q_ref[rr * GRID_W:(rr + 1) * GRID_W, :] * jnp.asarray(HEAD_DIM ** -0.5, BF16)
        qbd = jnp.concatenate([jnp.where(first_head, q, zero), jnp.where(first_head, zero, q)], axis=0)
        ks = k_ref[pl.ds(k0, NA_KEYS), :]
        s = lax.dot_general(qbd, ks, (((1,), (1,)), ((), ())), preferred_element_type=F32)
        s_ref[rr] = s + bias_ref[p]
    for rr in range(rows_per_step):
        s = s_ref[rr]
        e = jnp.exp(s - jnp.max(s, axis=-1, keepdims=True))
        l_ref[rr] = jnp.broadcast_to(1.0 / jnp.sum(e, axis=-1, keepdims=True), (LANES, LANES))
        p_ref[rr] = e.astype(BF16)
    for rr in range(rows_per_step):
        _, k0 = window(rr)
        vs = v_ref[pl.ds(k0, NA_KEYS), :]
        o = jnp.dot(p_ref[rr], vs, preferred_element_type=F32) * l_ref[rr]
        out = jnp.where(first_head, o[:GRID_W], o[GRID_W:])
        o_ref[rr * GRID_W:(rr + 1) * GRID_W, :] = out.astype(BF16)


def _attention(qkv, bias):
    B, L, _ = qkv.shape
    rows = L // GRID_W
    rows_per_step = 16
    tq = rows_per_step * GRID_W
    body = functools.partial(_attn_body, rows=rows, rows_per_step=rows_per_step)
    return pl.pallas_call(
        body,
        grid=(HEAD_PAIRS, B, L // tq),
        in_specs=[
            pl.BlockSpec((None, tq, LANES), lambda hp, b, rb: (b, rb, hp)),
            pl.BlockSpec((None, L, LANES), lambda hp, b, rb: (b, 0, HEAD_PAIRS + hp)),
            pl.BlockSpec((None, L, LANES), lambda hp, b, rb: (b, 0, 2 * HEAD_PAIRS + hp)),
            pl.BlockSpec((None, NA_ROWS, LANES, NA_KEYS), lambda hp, b, rb: (hp, 0, 0, 0)),
        ],
        out_specs=pl.BlockSpec((None, tq, LANES), lambda hp, b, rb: (b, rb, hp)),
        out_shape=jax.ShapeDtypeStruct((B, L, D_ATTN), BF16),
        scratch_shapes=[
            pltpu.VMEM((rows_per_step, LANES, NA_KEYS), F32),
            pltpu.VMEM((rows_per_step, LANES, NA_KEYS), BF16),
            pltpu.VMEM((rows_per_step, LANES, LANES), F32),
        ],
        compiler_params=pltpu.CompilerParams(
            dimension_semantics=("arbitrary", "arbitrary", "arbitrary"),
            vmem_limit_bytes=VMEM_LIMIT),
        name="na_attention",
    )(qkv, qkv, qkv, bias)


def _glu(uc_rows):
    a = uc_rows[:, 0:D_CONV].astype(F32)
    gt = uc_rows[:, D_CONV:2 * D_CONV].astype(F32)
    return a * jax.nn.sigmoid(gt)


def _conv_body(uc_ref, w_ref, b_ref, lng_ref, lnb_ref, og_ref, o_ref, zs_ref, cbuf_ref, *, seg):
    sg = pl.program_id(1)
    n_seg = pl.num_programs(1)
    seg0 = pl.multiple_of(sg * seg, seg)
    halo = CONV_HALO
    n_lane_tiles = D_CONV // LANES

    def put_z(u0, n, z):
        for j in range(n_lane_tiles):
            zs_ref[0, j, pl.ds(u0, n), :] = z[:, j * LANES:(j + 1) * LANES]

    glu_rows = 128

    def glu(i, carry):
        u0 = pl.multiple_of(i * glu_rows, glu_rows)
        put_z(halo + u0, glu_rows, _glu(uc_ref[pl.ds(seg0 + u0, glu_rows), :]))
        return carry

    lax.fori_loop(0, seg // glu_rows, glu, 0)

    @pl.when(sg > 0)
    def _():
        put_z(0, halo, _glu(uc_ref[pl.ds(seg0 - halo, halo), :]))

    @pl.when(sg == 0)
    def _():
        put_z(0, halo, jnp.zeros((halo, D_CONV), F32))

    @pl.when(sg < n_seg - 1)
    def _():
        put_z(halo + seg, halo, _glu(uc_ref[pl.ds(seg0 + seg, halo), :]))

    @pl.when(sg == n_seg - 1)
    def _():
        put_z(halo + seg, halo, jnp.zeros((halo, D_CONV), F32))

    def shift_rows(j, u0, n):
        x = zs_ref[0, j, pl.ds(u0, n + 8), :]
        for s in range(1, 8):
            zs_ref[s, j, pl.ds(u0, n), :] = x[s:s + n]

    shift_chunk = 64
    chunks = seg // shift_chunk

    def shift(i, carry):
        shift_rows(i // chunks, pl.multiple_of((i % chunks) * shift_chunk, shift_chunk), shift_chunk)
        return carry

    lax.fori_loop(0, n_lane_tiles * chunks, shift, 0)
    for j in range(n_lane_tiles):
        shift_rows(j, seg, 2 * halo - 8)

    tt = CONV_TILE
    tiles = seg // tt
    first = halo - CONV_WIDTH // 2

    def taps(i, carry):
        j = i // tiles
        t0 = pl.multiple_of((i % tiles) * tt, tt)
        acc = jnp.broadcast_to(b_ref[j], (tt, LANES))
        for k in range(CONV_WIDTH):
            off = k + first
            acc = acc + w_ref[j, k:k + 1, :] * zs_ref[off % 8, j, pl.ds(t0 + 8 * (off // 8), tt), :]
        cbuf_ref[j, pl.ds(t0, tt), :] = acc
        return carry

    lax.fori_loop(0, n_lane_tiles * tiles, taps, 0, unroll=2)

    norm_rows = 128

    def norm(i, carry):
        t0 = pl.multiple_of(i * norm_rows, norm_rows)
        y = jnp.concatenate([cbuf_ref[j, pl.ds(t0, norm_rows), :] for j in range(n_lane_tiles)], axis=1)
        mu = jnp.mean(y, axis=-1, keepdims=True)
        yc = y - mu
        var = jnp.mean(yc * yc, axis=-1, keepdims=True)
        yn = yc * lax.rsqrt(var + EPS) * lng_ref[...] + lnb_ref[...]
        sw = yn * jax.nn.sigmoid(yn)
        o_ref[pl.ds(t0, norm_rows), :] = _rms(sw, og_ref[...]).astype(BF16)
        return carry

    lax.fori_loop(0, seg // norm_rows, norm, 0, unroll=2)


def _conv(uc, w, b, lng, lnb, og):
    B, L, _ = uc.shape
    seg = CONV_SEGMENT
    n_lane_tiles = D_CONV // LANES
    vec = pl.BlockSpec((1, D_CONV), lambda i, s: (0, 0))
    w = w.reshape(CONV_WIDTH + 1, n_lane_tiles, LANES).transpose(1, 0, 2)
    b = b.reshape(n_lane_tiles, 1, LANES)
    return pl.pallas_call(
        functools.partial(_conv_body, seg=seg),
        grid=(B, L // seg),
        in_specs=[
            pl.BlockSpec((None, L, 2 * D_CONV), lambda i, s: (i, 0, 0)),
            pl.BlockSpec((n_lane_tiles, CONV_WIDTH + 1, LANES), lambda i, s: (0, 0, 0)),
            pl.BlockSpec((n_lane_tiles, 1, LANES), lambda i, s: (0, 0, 0)),
            vec, vec, vec,
        ],
        out_specs=pl.BlockSpec((None, seg, D_CONV), lambda i, s: (i, s, 0)),
        out_shape=jax.ShapeDtypeStruct((B, L, D_CONV), BF16),
        scratch_shapes=[
            pltpu.VMEM((8, n_lane_tiles, seg + 2 * CONV_HALO, LANES), F32),
            pltpu.VMEM((n_lane_tiles, seg, LANES), F32),
        ],
        compiler_params=pltpu.CompilerParams(
            dimension_semantics=("arbitrary", "arbitrary"), vmem_limit_bytes=VMEM_LIMIT),
        name="conformer_conv",
    )(uc, w, b, lng, lnb, og)


GROUP_ROWS = 8
TILE_GROUPS = (2 * TOKEN_TILE + (GROUP_ROWS - 1) * N_EXPERTS + GROUP_ROWS - 1) // GROUP_ROWS
TILE_GROUPS = (TILE_GROUPS + 15) // 16 * 16
TILE_SLOTS = TILE_GROUPS * GROUP_ROWS
BLOCK_GROUPS = EXPERT_BLOCK // GROUP_ROWS
X_SLOTS = 3
ROUTER_ROWS = 128
EXPERT_ROW0 = 8
COL_P0, COL_P1, COL_W0, COL_W1 = range(4)


U32 = jnp.uint32
PACKED = D_MODEL // 2


def _pack_bf16_pairs(x):
    bits = pltpu.bitcast(x, U32)
    return bits[:, PACKED:] | (bits[:, :PACKED] >> 16)


def _unpack_bf16_pairs(words):
    lo = pltpu.bitcast(words << 16, F32)
    hi = pltpu.bitcast(words & jnp.uint32(0xFFFF0000), F32)
    return jnp.concatenate([lo, hi], axis=1).astype(BF16)


def _sorted_out_proj_body(x_ref, a_ref, c_ref, ag_ref, wa_ref, wc_ref, n2g_ref, wrt_ref, brt_ref,
                          x1_ref, hs_ref, route_ref, grp_ref, *, tm):
    an = _rms(a_ref[...].astype(F32), ag_ref[...]).astype(BF16)
    y = jnp.dot(an, wa_ref[...], preferred_element_type=F32)
    y = y + jnp.dot(c_ref[...], wc_ref[...], preferred_element_type=F32)
    x1 = x_ref[...] + y
    x1_ref[...] = x1
    h2 = _rms(x1, n2g_ref[...])

    h_hi = h2.astype(BF16)
    h_lo = (h2 - h_hi.astype(F32)).astype(BF16)
    nt = (((1,), (1,)), ((), ()))
    split = (lax.dot_general(wrt_ref[...], h_hi, nt, preferred_element_type=F32)
             + lax.dot_general(wrt_ref[...], h_lo, nt, preferred_element_type=F32))
    logits = split[:ROUTER_ROWS] + split[ROUTER_ROWS:] + brt_ref[...]

    gtop = logits[0:1]
    gsel = jnp.zeros((1, tm), F32)
    for g in range(1, N_GROUPS):
        cand = logits[g:g + 1]
        better = cand > gtop
        gsel = jnp.where(better, float(g), gsel)
        gtop = jnp.where(better, cand, gtop)
    denom = jnp.zeros((1, tm), F32)
    for g in range(N_GROUPS):
        denom = denom + jnp.exp(logits[g:g + 1] - gtop)
    p_g = 1.0 / denom

    el = logits[EXPERT_ROW0:EXPERT_ROW0 + EPG]
    for g in range(1, N_GROUPS):
        el = jnp.where(gsel == float(g), logits[EXPERT_ROW0 + g * EPG:EXPERT_ROW0 + (g + 1) * EPG], el)
    ninf = jnp.full((1, tm), -jnp.inf, F32)
    v0, v1 = ninf, ninf
    i0 = jnp.zeros((1, tm), F32)
    i1 = jnp.zeros((1, tm), F32)
    for j in range(EPG):
        cand = el[j:j + 1]
        gt0 = cand > v0
        gt1 = cand > v1
        v1 = jnp.where(gt0, v0, jnp.where(gt1, cand, v1))
        i1 = jnp.where(gt0, i0, jnp.where(gt1, float(j), i1))
        v0 = jnp.where(gt0, cand, v0)
        i0 = jnp.where(gt0, float(j), i0)
    t = jnp.exp(v1 - v0)
    w0 = p_g / (1.0 + t)
    w1 = p_g * t / (1.0 + t)
    e0 = gsel * EPG + i0
    e1 = gsel * EPG + i1

    expert = lax.broadcasted_iota(I32, (N_EXPERTS, tm), 0).astype(F32)
    hit0 = expert == e0
    hit1 = expert == e1
    onehot = jnp.where(hit0 | hit1, 1.0, 0.0).astype(BF16)
    ri = lax.broadcasted_iota(I32, (tm, tm), 0)
    ci = lax.broadcasted_iota(I32, (tm, tm), 1)
    upper = jnp.where(ri <= ci, 1.0, 0.0).astype(BF16)
    cum = jnp.dot(onehot, upper, preferred_element_type=F32)
    count = jnp.dot(onehot, jnp.ones((tm, LANES), BF16), preferred_element_type=F32)
    groups = jnp.floor((count + (GROUP_ROWS - 1)) * (1.0 / GROUP_ROWS))
    er = lax.broadcasted_iota(I32, (N_EXPERTS, N_EXPERTS), 0)
    ec = lax.broadcasted_iota(I32, (N_EXPERTS, N_EXPERTS), 1)
    below = jnp.where(ec < er, 1.0, 0.0).astype(BF16)
    run_start = jnp.dot(below, groups.astype(BF16), preferred_element_type=F32) * GROUP_ROWS
    run_start = jnp.concatenate([run_start] * (tm // LANES), axis=1)
    pos = run_start + cum - 1.0
    p0 = jnp.sum(jnp.where(hit0, pos, 0.0), axis=0, keepdims=True)
    p1 = jnp.sum(jnp.where(hit1, pos, 0.0), axis=0, keepdims=True)
    grp_ref[...] = groups

    slot = lax.broadcasted_iota(I32, (TILE_SLOTS, tm), 0).astype(F32)
    perm = jnp.where((slot == p0) | (slot == p1), 1.0, 0.0).astype(BF16)
    hs_ref[...] = _pack_bf16_pairs(jnp.dot(perm, h_hi, preferred_element_type=F32))

    rows = jnp.concatenate([p0, p1, w0, w1, jnp.zeros((ROUTER_ROWS - 4, tm), F32)], axis=0)
    route_ref[...] = rows.T


def _sorted_out_proj(x2, a2, c2, ag, wa, wc, n2g, wrt, brt):
    T = x2.shape[0]
    tm = TOKEN_TILE
    nt = T // tm
    full = lambda shape: pl.BlockSpec(shape, lambda i: (0,) * len(shape))
    return pl.pallas_call(
        functools.partial(_sorted_out_proj_body, tm=tm),
        grid=(nt,),
        in_specs=[
            pl.BlockSpec((tm, D_MODEL), lambda i: (i, 0)),
            pl.BlockSpec((tm, D_ATTN), lambda i: (i, 0)),
            pl.BlockSpec((tm, D_CONV), lambda i: (i, 0)),
            full((1, D_ATTN)),
            full((D_ATTN, D_MODEL)),
            full((D_CONV, D_MODEL)),
            full((1, D_MODEL)),
            full((2 * ROUTER_ROWS, D_MODEL)),
            full((ROUTER_ROWS, 1)),
        ],
        out_specs=[
            pl.BlockSpec((tm, D_MODEL), lambda i: (i, 0)),
            pl.BlockSpec((TILE_SLOTS, PACKED), lambda i: (i, 0)),
            pl.BlockSpec((tm, LANES), lambda i: (i, 0)),
            pl.BlockSpec((None, N_EXPERTS, LANES), lambda i: (i, 0, 0)),
        ],
        out_shape=[
            jax.ShapeDtypeStruct((T, D_MODEL), F32),
            jax.ShapeDtypeStruct((nt * TILE_SLOTS, PACKED), U32),
            jax.ShapeDtypeStruct((T, LANES), F32),
            jax.ShapeDtypeStruct((nt, N_EXPERTS, LANES), F32),
        ],
        compiler_params=pltpu.CompilerParams(
            dimension_semantics=("arbitrary",), vmem_limit_bytes=VMEM_LIMIT),
        name="out_proj_router",
    )(x2, a2, c2, ag, wa, wc, n2g, wrt, brt)


def _group_copies(src_ref, hs_hbm, ys_hbm, xbuf, ybuf, gsem, ssem, block, slot, trash0, to_trash):
    def group_index(i):
        return src_ref[block * BLOCK_GROUPS + i]

    def rows_of(i):
        start = i * GROUP_ROWS
        return pl.ds(start if isinstance(i, int) else pl.multiple_of(start, GROUP_ROWS), GROUP_ROWS)

    def gather(i):
        g = jnp.maximum(group_index(i), 0)
        return pltpu.make_async_copy(hs_hbm.at[g], xbuf.at[slot, rows_of(i)], gsem.at[slot])

    def scatter(i):
        g = group_index(i)
        g = jnp.where((g < 0) | to_trash, trash0 + slot * BLOCK_GROUPS + i, g)
        return pltpu.make_async_copy(ybuf.at[slot, rows_of(i)], ys_hbm.at[g], ssem.at[slot])

    return gather, scatter


def _sorted_expert_body(be_ref, nu_ref, src_ref, tg_ref, hs_hbm, wg_ref, wu_ref, wd_ref, ys_hbm,
                        xbuf, ybuf, zero_ref, gsem, ssem, zsem, *, n_tiles, trash0):
    del be_ref
    b = pl.program_id(0)
    n_used = nu_ref[0]
    slot = b % 2

    @pl.when(b == 0)
    def _():
        zero_ref[...] = jnp.zeros_like(zero_ref)

        def zero_copy(g):
            return pltpu.make_async_copy(zero_ref, ys_hbm.at[g], zsem)

        def over_unused(fn):
            def tile(c, carry):
                def group(g, carry2):
                    fn(c * TILE_GROUPS + g)
                    return carry2
                return lax.fori_loop(tg_ref[c], TILE_GROUPS, group, carry)
            lax.fori_loop(0, n_tiles, tile, 0)

            def trash(i, carry):
                fn(trash0 + i)
                return carry
            lax.fori_loop(0, 2 * BLOCK_GROUPS, trash, 0)

        over_unused(lambda g: zero_copy(g).start())
        over_unused(lambda g: zero_copy(g).wait())

    def for_groups(fn):
        def step(i, carry):
            fn(i)
            return carry
        lax.fori_loop(0, BLOCK_GROUPS, step, 0, unroll=8)

    def copies(block, slot_, to_trash=False):
        return _group_copies(src_ref, hs_hbm, ys_hbm, xbuf, ybuf, gsem, ssem, block, slot_, trash0,
                             to_trash)

    def wait_gathers(slot_):
        gather, _ = copies(0, slot_)
        for_groups(lambda i: gather(i).wait())

    def wait_scatters(slot_):
        _, scatter = copies(0, slot_)
        for_groups(lambda i: scatter(i).wait())

    xslot = b % X_SLOTS

    def last(block):
        return jnp.minimum(block, n_used - 1)

    @pl.when((b == 0) & (n_used > 0))
    def _():
        ybuf[...] = jnp.zeros_like(ybuf)
        for block in range(X_SLOTS - 1):
            gather, _ = copies(last(block), block)
            for_groups(lambda i: gather(i).start())

    @pl.when((b >= 1) & (b <= n_used))
    def _():
        wait_scatters(slot)

    @pl.when(b < n_used)
    def _():
        wait_gathers(xslot)
        ahead = b + (X_SLOTS - 1)
        gather, _ = copies(last(ahead), ahead % X_SLOTS)
        _, scatter = copies(jnp.maximum(b - 1, 0), 1 - slot, to_trash=b == 0)
        for i in range(BLOCK_GROUPS):
            gather(i).start()
            scatter(i).start()
        x = _unpack_bf16_pairs(xbuf[xslot])
        g = jnp.dot(x, wg_ref[...], preferred_element_type=F32)
        u = jnp.dot(x, wu_ref[...], preferred_element_type=F32)
        hid = (g * jax.nn.sigmoid(g) * u).astype(BF16)
        y = jnp.dot(hid, wd_ref[...], preferred_element_type=F32)
        ybuf[slot] = _pack_bf16_pairs(y.astype(BF16).astype(F32))

    @pl.when((b == n_used) & (n_used > 0))
    def _():
        for ahead in range(X_SLOTS - 1):
            wait_gathers((b + ahead) % X_SLOTS)
        _, scatter = copies(b - 1, 1 - slot)
        for_groups(lambda i: scatter(i).start())
        wait_scatters(1 - slot)


def _sorted_experts(block_e, n_used, src, tile_groups, hs3, wg, wu, wd):
    n_groups = hs3.shape[0]
    n_tiles = tile_groups.shape[0]
    n_blocks = block_e.shape[0]
    blk = EXPERT_BLOCK
    grid_spec = pltpu.PrefetchScalarGridSpec(
        num_scalar_prefetch=4,
        grid=(n_blocks,),
        in_specs=[
            pl.BlockSpec(memory_space=pl.ANY),
            pl.BlockSpec((None, D_MODEL, D_EXPERT), lambda b, be, nu, src, tg: (be[b], 0, 0)),
            pl.BlockSpec((None, D_MODEL, D_EXPERT), lambda b, be, nu, src, tg: (be[b], 0, 0)),
            pl.BlockSpec((None, D_EXPERT, D_MODEL), lambda b, be, nu, src, tg: (be[b], 0, 0)),
        ],
        out_specs=pl.BlockSpec(memory_space=pl.ANY),
        scratch_shapes=[
            pltpu.VMEM((X_SLOTS, blk, PACKED), U32),
            pltpu.VMEM((2, blk, PACKED), U32),
            pltpu.VMEM((GROUP_ROWS, PACKED), U32),
            pltpu.SemaphoreType.DMA((X_SLOTS,)),
            pltpu.SemaphoreType.DMA((2,)),
            pltpu.SemaphoreType.DMA(()),
        ],
    )
    return pl.pallas_call(
        functools.partial(_sorted_expert_body, n_tiles=n_tiles, trash0=n_groups),
        grid_spec=grid_spec,
        out_shape=jax.ShapeDtypeStruct((n_groups + 2 * BLOCK_GROUPS, GROUP_ROWS, PACKED), U32),
        compiler_params=pltpu.CompilerParams(
            dimension_semantics=("arbitrary",), vmem_limit_bytes=VMEM_LIMIT),
        name="moe_experts",
    )(block_e, n_used, src, tile_groups, hs3, wg, wu, wd)


def _sorted_combine_body(x1_ref, route_ref, fg_ref, ys_ref, o_ref, *, tm):
    ys = _unpack_bf16_pairs(ys_ref[...])
    route = route_ref[...]
    slot = lax.broadcasted_iota(I32, (tm, TILE_SLOTS), 1).astype(F32)
    gate = (jnp.where(slot == route[:, COL_P0:COL_P0 + 1], route[:, COL_W0:COL_W0 + 1], 0.0)
            + jnp.where(slot == route[:, COL_P1:COL_P1 + 1], route[:, COL_W1:COL_W1 + 1], 0.0))
    moe = jnp.dot(gate.astype(BF16), ys, preferred_element_type=F32)
    o_ref[...] = _rms(x1_ref[...] + moe, fg_ref[...])


def _sorted_combine(x1, route, fg, ys2):
    T = x1.shape[0]
    tm = TOKEN_TILE
    return pl.pallas_call(
        functools.partial(_sorted_combine_body, tm=tm),
        grid=(T // tm,),
        in_specs=[
            pl.BlockSpec((tm, D_MODEL), lambda i: (i, 0)),
            pl.BlockSpec((tm, LANES), lambda i: (i, 0)),
            pl.BlockSpec((1, D_MODEL), lambda i: (0, 0)),
            pl.BlockSpec((TILE_SLOTS, PACKED), lambda i: (i, 0)),
        ],
        out_specs=pl.BlockSpec((tm, D_MODEL), lambda i: (i, 0)),
        out_shape=jax.ShapeDtypeStruct((T, D_MODEL), F32),
        compiler_params=pltpu.CompilerParams(
            dimension_semantics=("arbitrary",), vmem_limit_bytes=VMEM_LIMIT),
        name="moe_combine",
    )(x1, route, fg, ys2)


def _moe_plan(groups):
    nt = groups.shape[0]
    run_start = jnp.cumsum(groups, axis=1) - groups
    cum_tiles = jnp.cumsum(groups, axis=0)
    total = cum_tiles[-1]
    blocks = (total + BLOCK_GROUPS - 1) // BLOCK_GROUPS
    blk_end = jnp.cumsum(blocks)
    blk_start = blk_end - blocks
    n_blocks = (2 * TOKEN_TILE + (GROUP_ROWS - 1) * N_EXPERTS) * nt // EXPERT_BLOCK + N_EXPERTS + 1
    b = jnp.arange(n_blocks, dtype=I32)
    block_e = jnp.minimum(jnp.sum(b[:, None] >= blk_end[None, :], axis=1), N_EXPERTS - 1).astype(I32)
    n_used = blk_end[-1:].astype(I32)

    sg = jnp.arange(n_blocks * BLOCK_GROUPS, dtype=I32)
    sb = sg // BLOCK_GROUPS
    sg_e = jnp.broadcast_to(block_e[:, None], (n_blocks, BLOCK_GROUPS)).reshape(-1)
    onehot_e = (sg_e[:, None] == jnp.arange(N_EXPERTS, dtype=I32)[None, :]).astype(F32)

    def pick(table):
        return jnp.dot(onehot_e, table.astype(F32), precision=lax.Precision.HIGHEST).astype(I32)

    j = sg - pick(blk_start[:, None])[:, 0] * BLOCK_GROUPS
    valid = (sb < n_used[0]) & (j < pick(total[:, None])[:, 0])
    cum_e = pick(cum_tiles.T)
    tile = jnp.minimum(jnp.sum(cum_e <= j[:, None], axis=1), nt - 1)
    tile_hot = tile[:, None] == jnp.arange(nt, dtype=I32)[None, :]
    before = jnp.sum(jnp.where(tile_hot, cum_e - pick(groups.T), 0), axis=1)
    start = jnp.sum(jnp.where(tile_hot, pick(run_start.T), 0), axis=1)
    src = jnp.where(valid, tile * TILE_GROUPS + start + (j - before), -1).astype(I32)
    tile_groups = jnp.sum(groups, axis=1).astype(I32)
    return block_e, n_used, src, tile_groups


def _trunk(x, p):
    B, L, _ = x.shape
    T = B * L
    x2 = x.reshape(T, D_MODEL)
    qkv, uc = _in_proj(x2, p["norm1_g"], p["wqkv"], p["wconv"])
    a = _attention(qkv.reshape(B, L, 3 * D_ATTN), p["na_bias"])
    c = _conv(uc.reshape(B, L, 2 * D_CONV), p["conv_w"], p["conv_b"], p["conv_ln_g"],
              p["conv_ln_b"], p["conv_out_g"])
    x1, hs, route, grp = _sorted_out_proj(
        x2, a.reshape(T, D_ATTN), c.reshape(T, D_CONV), p["attn_out_g"], p["wout_a"],
        p["wout_c"], p["norm2_g"], p["w_router_t"], p["b_router_t"])
    block_e, n_used, src, tile_groups = _moe_plan(grp[:, :, 0].astype(I32))
    hs3 = hs.reshape(hs.shape[0] // GROUP_ROWS, GROUP_ROWS, PACKED)
    ys3 = _sorted_experts(block_e, n_used, src, tile_groups, hs3, p["w_gate"], p["w_up"], p["w_down"])
    ys2 = ys3.reshape(ys3.shape[0] * GROUP_ROWS, PACKED)
    out = _sorted_combine(x1, route, p["final_g"], ys2)
    return out.reshape(B, L, D_MODEL)


def kernel(x_prompt, x_sample, norm1_g, w_in, rpb, attn_out_g, conv_w, conv_b, conv_ln_g,
           conv_ln_b, conv_out_g, w_out, norm2_g, w_group, b_group, w_expert, b_expert,
           w_e_gate, w_e_up, w_e_down, final_g):
    l = 0
    w_router_t = jnp.zeros((ROUTER_ROWS, D_MODEL), F32)
    w_router_t = w_router_t.at[:N_GROUPS].set(w_group[l].T)
    w_router_t = w_router_t.at[EXPERT_ROW0:EXPERT_ROW0 + N_EXPERTS].set(
        w_expert[l].transpose(0, 2, 1).reshape(N_EXPERTS, D_MODEL))
    w_router_hi = w_router_t.astype(BF16)
    w_router_lo = (w_router_t - w_router_hi.astype(F32)).astype(BF16)
    w_router_t = jnp.concatenate([w_router_hi, w_router_lo], axis=0)
    b_router_t = jnp.zeros((ROUTER_ROWS,), F32)
    b_router_t = b_router_t.at[:N_GROUPS].set(b_group[l])
    b_router_t = b_router_t.at[EXPERT_ROW0:EXPERT_ROW0 + N_EXPERTS].set(b_expert[l].reshape(N_EXPERTS))
    b_router_t = b_router_t.reshape(ROUTER_ROWS, 1)
    p = {
        "norm1_g": norm1_g[l].reshape(1, D_MODEL),
        "wqkv": w_in[l][:, :3 * D_ATTN].astype(BF16),
        "wconv": w_in[l][:, 3 * D_ATTN:].astype(BF16),
        "na_bias": _na_bias_table(rpb[l]),
        "attn_out_g": attn_out_g[l].reshape(1, D_ATTN),
        "conv_w": jnp.pad(conv_w[l], ((0, 1), (0, 0))),
        "conv_b": conv_b[l].reshape(1, D_CONV),
        "conv_ln_g": conv_ln_g[l].reshape(1, D_CONV),
        "conv_ln_b": conv_ln_b[l].reshape(1, D_CONV),
        "conv_out_g": conv_out_g[l].reshape(1, D_CONV),
        "wout_a": w_out[l][:D_ATTN].astype(BF16),
        "wout_c": w_out[l][D_ATTN:].astype(BF16),
        "norm2_g": norm2_g[l].reshape(1, D_MODEL),
        "w_router_t": w_router_t,
        "b_router_t": b_router_t,
        "w_gate": w_e_gate[l].astype(BF16),
        "w_up": w_e_up[l].astype(BF16),
        "w_down": w_e_down[l].astype(BF16),
        "final_g": final_g.reshape(1, D_MODEL),
    }
    return (_trunk(x_prompt, p), _trunk(x_sample, p))
```

```python
import functools

import jax
import jax.numpy as jnp
from jax import lax
from jax.experimental import pallas as pl
from jax.experimental.pallas import tpu as pltpu

F32 = jnp.float32
BF16 = jnp.bfloat16
I32 = jnp.int32

D_MODEL = 1024
GRID_W = 64
D_ATTN = 512
D_CONV = 512
HEAD_DIM = 64
N_HEADS = 8
NA_ROWS = 8
NA_COLS = 16
CONV_WIDTH = 31
N_GROUPS = 4
EPG = 8
N_EXPERTS = 32
D_EXPERT = 512
EPS = 1e-6

LANES = 128
HEAD_PAIRS = N_HEADS * HEAD_DIM // LANES
NA_KEYS = NA_ROWS * GRID_W
MASKED = -1e30
TOKEN_TILE = 512
EXPERT_BLOCK = 512
CONV_TILE = 64
CONV_SEGMENT = 512
CONV_HALO = 16
VMEM_LIMIT = 56 * 1024 * 1024


def _rms(x, g):
    return x * lax.rsqrt(jnp.mean(x * x, axis=-1, keepdims=True) + EPS) * g


def _in_proj_body(x_ref, g_ref, wqkv_ref, wc_ref, qkv_ref, uc_ref):
    h = _rms(x_ref[...], g_ref[...]).astype(BF16)
    qkv_ref[...] = jnp.dot(h, wqkv_ref[...], preferred_element_type=F32).astype(BF16)
    uc_ref[...] = jnp.dot(h, wc_ref[...], preferred_element_type=F32).astype(BF16)


def _in_proj(x2, g, wqkv, wc):
    T = x2.shape[0]
    tm = TOKEN_TILE
    return pl.pallas_call(
        _in_proj_body,
        grid=(T // tm,),
        in_specs=[
            pl.BlockSpec((tm, D_MODEL), lambda i: (i, 0)),
            pl.BlockSpec((1, D_MODEL), lambda i: (0, 0)),
            pl.BlockSpec((D_MODEL, 3 * D_ATTN), lambda i: (0, 0)),
            pl.BlockSpec((D_MODEL, 2 * D_CONV), lambda i: (0, 0)),
        ],
        out_specs=[
            pl.BlockSpec((tm, 3 * D_ATTN), lambda i: (i, 0)),
            pl.BlockSpec((tm, 2 * D_CONV), lambda i: (i, 0)),
        ],
        out_shape=[
            jax.ShapeDtypeStruct((T, 3 * D_ATTN), BF16),
            jax.ShapeDtypeStruct((T, 2 * D_CONV), BF16),
        ],
        compiler_params=pltpu.CompilerParams(
            dimension_semantics=("arbitrary",), vmem_limit_bytes=VMEM_LIMIT),
        name="in_proj",
    )(x2, g, wqkv, wc)


def _na_bias_table(rpb):
    c = jnp.arange(GRID_W)
    col_start = jnp.clip(c - NA_COLS // 2, 0, GRID_W - NA_COLS)
    cp = jnp.arange(GRID_W)
    valid = (cp[None, :] >= col_start[:, None]) & (cp[None, :] < col_start[:, None] + NA_COLS)
    col_off = cp[None, :] - c[:, None] + (NA_COLS - 1)
    sel = (col_off[None] == jnp.arange(2 * NA_COLS - 1)[:, None, None]) & valid[None]
    a = jnp.einsum("hrd,dcx->hrcx", rpb, sel.astype(F32), precision=lax.Precision.HIGHEST)
    a = jnp.where(valid[None, None], a, MASKED)
    t = jnp.stack([a[:, NA_ROWS - 1 - p:2 * NA_ROWS - 1 - p] for p in range(NA_ROWS)], axis=1)
    t = t.transpose(0, 1, 3, 2, 4)
    t = t.reshape(HEAD_PAIRS, 2, NA_ROWS, GRID_W, NA_KEYS)
    return t.transpose(0, 2, 1, 3, 4).reshape(HEAD_PAIRS, NA_ROWS, LANES, NA_KEYS).astype(F32)


def _attn_body(q_ref, k_ref, v_ref, bias_ref, o_ref, s_ref, p_ref, l_ref, *, rows, rows_per_step):
    rb = pl.program_id(2)
    first_head = lax.broadcasted_iota(I32, (GRID_W, LANES), 1) < HEAD_DIM
    zero = jnp.zeros((GRID_W, LANES), BF16)

    def window(rr):
        r = rb * rows_per_step + rr
        r_start = jnp.clip(r - NA_ROWS // 2, 0, rows - NA_ROWS)
        return r - r_start, pl.multiple_of(r_start * GRID_W, GRID_W)

    for rr in range(rows_per_step):
        p, k0 = window(rr)
        q = q_ref[rr * GRID_W:(rr + 1) * GRID_W, :] * jnp.asarray(HEAD_DIM ** -0.5, BF16)
        qbd = jnp.concatenate([jnp.where(first_head, q, zero), jnp.where(first_head, zero, q)], axis=0)
        ks = k_ref[pl.ds(k0, NA_KEYS), :]
        s = lax.dot_general(qbd, ks, (((1,), (1,)), ((), ())), preferred_element_type=F32)
        s_ref[rr] = s + bias_ref[p]
    for rr in range(rows_per_step):
        s = s_ref[rr]
        e = jnp.exp(s - jnp.max(s, axis=-1, keepdims=True))
        l_ref[rr] = jnp.broadcast_to(1.0 / jnp.sum(e, axis=-1, keepdims=True), (LANES, LANES))
        p_ref[rr] = e.astype(BF16)
    for rr in range(rows_per_step):
        _, k0 = window(rr)
        vs = v_ref[pl.ds(k0, NA_KEYS), :]
        o = jnp.dot(p_ref[rr], vs, preferred_element_type=F32) * l_ref[rr]
        out = jnp.where(first_head, o[:GRID_W], o[GRID_W:])
        o_ref[rr * GRID_W:(rr + 1) * GRID_W, :] = out.astype(BF16)


def _attention(qkv, bias):
    B, L, _ = qkv.shape
    rows = L // GRID_W
    rows_per_step = 16
    tq = rows_per_step * GRID_W
    body = functools.partial(_attn_body, rows=rows, rows_per_step=rows_per_step)
    return pl.pallas_call(
        body,
        grid=(HEAD_PAIRS, B, L // tq),
        in_specs=[
            pl.BlockSpec((None, tq, LANES), lambda hp, b, rb: (b, rb, hp)),
            pl.BlockSpec((None, L, LANES), lambda hp, b, rb: (b, 0, HEAD_PAIRS + hp)),
            pl.BlockSpec((None, L, LANES), lambda hp, b, rb: (b, 0, 2 * HEAD_PAIRS + hp)),
            pl.BlockSpec((None, NA_ROWS, LANES, NA_KEYS), lambda hp, b, rb: (hp, 0, 0, 0)),
        ],
        out_specs=pl.BlockSpec((None, tq, LANES), lambda hp, b, rb: (b, rb, hp)),
        out_shape=jax.ShapeDtypeStruct((B, L, D_ATTN), BF16),
        scratch_shapes=[
            pltpu.VMEM((rows_per_step, LANES, NA_KEYS), F32),
            pltpu.VMEM((rows_per_step, LANES, NA_KEYS), BF16),
            pltpu.VMEM((rows_per_step, LANES, LANES), F32),
        ],
        compiler_params=pltpu.CompilerParams(
            dimension_semantics=("arbitrary", "arbitrary", "arbitrary"),
            vmem_limit_bytes=VMEM_LIMIT),
        name="na_attention",
    )(qkv, qkv, qkv, bias)


def _glu(uc_rows):
    a = uc_rows[:, 0:D_CONV].astype(F32)
    gt = uc_rows[:, D_CONV:2 * D_CONV].astype(F32)
    return a * jax.nn.sigmoid(gt)


def _conv_body(uc_ref, w_ref, b_ref, lng_ref, lnb_ref, og_ref, o_ref, zs_ref, cbuf_ref, *, seg):
    sg = pl.program_id(1)
    n_seg = pl.num_programs(1)
    seg0 = pl.multiple_of(sg * seg, seg)
    halo = CONV_HALO
    n_lane_tiles = D_CONV // LANES

    def put_z(u0, n, z):
        for j in range(n_lane_tiles):
            zs_ref[0, j, pl.ds(u0, n), :] = z[:, j * LANES:(j + 1) * LANES]

    glu_rows = 128

    def glu(i, carry):
        u0 = pl.multiple_of(i * glu_rows, glu_rows)
        put_z(halo + u0, glu_rows, _glu(uc_ref[pl.ds(seg0 + u0, glu_rows), :]))
        return carry

    lax.fori_loop(0, seg // glu_rows, glu, 0)

    @pl.when(sg > 0)
    def _():
        put_z(0, halo, _glu(uc_ref[pl.ds(seg0 - halo, halo), :]))

    @pl.when(sg == 0)
    def _():
        put_z(0, halo, jnp.zeros((halo, D_CONV), F32))

    @pl.when(sg < n_seg - 1)
    def _():
        put_z(halo + seg, halo, _glu(uc_ref[pl.ds(seg0 + seg, halo), :]))

    @pl.when(sg == n_seg - 1)
    def _():
        put_z(halo + seg, halo, jnp.zeros((halo, D_CONV), F32))

    def shift_rows(j, u0, n):
        x = zs_ref[0, j, pl.ds(u0, n + 8), :]
        for s in range(1, 8):
            zs_ref[s, j, pl.ds(u0, n), :] = x[s:s + n]

    shift_chunk = 64
    chunks = seg // shift_chunk

    def shift(i, carry):
        shift_rows(i // chunks, pl.multiple_of((i % chunks) * shift_chunk, shift_chunk), shift_chunk)
        return carry

    lax.fori_loop(0, n_lane_tiles * chunks, shift, 0)
    for j in range(n_lane_tiles):
        shift_rows(j, seg, 2 * halo - 8)

    tt = CONV_TILE
    tiles = seg // tt
    first = halo - CONV_WIDTH // 2

    def taps(i, carry):
        j = i // tiles
        t0 = pl.multiple_of((i % tiles) * tt, tt)
        acc = jnp.broadcast_to(b_ref[j], (tt, LANES))
        for k in range(CONV_WIDTH):
            off = k + first
            acc = acc + w_ref[j, k:k + 1, :] * zs_ref[off % 8, j, pl.ds(t0 + 8 * (off // 8), tt), :]
        cbuf_ref[j, pl.ds(t0, tt), :] = acc
        return carry

    lax.fori_loop(0, n_lane_tiles * tiles, taps, 0, unroll=2)

    norm_rows = 128

    def norm(i, carry):
        t0 = pl.multiple_of(i * norm_rows, norm_rows)
        y = jnp.concatenate([cbuf_ref[j, pl.ds(t0, norm_rows), :] for j in range(n_lane_tiles)], axis=1)
        mu = jnp.mean(y, axis=-1, keepdims=True)
        yc = y - mu
        var = jnp.mean(yc * yc, axis=-1, keepdims=True)
        yn = yc * lax.rsqrt(var + EPS) * lng_ref[...] + lnb_ref[...]
        sw = yn * jax.nn.sigmoid(yn)
        o_ref[pl.ds(t0, norm_rows), :] = _rms(sw, og_ref[...]).astype(BF16)
        return carry

    lax.fori_loop(0, seg // norm_rows, norm, 0, unroll=2)


def _conv(uc, w, b, lng, lnb, og):
    B, L, _ = uc.shape
    seg = CONV_SEGMENT
    n_lane_tiles = D_CONV // LANES
    vec = pl.BlockSpec((1, D_CONV), lambda i, s: (0, 0))
    w = w.reshape(CONV_WIDTH + 1, n_lane_tiles, LANES).transpose(1, 0, 2)
    b = b.reshape(n_lane_tiles, 1, LANES)
    return pl.pallas_call(
        functools.partial(_conv_body, seg=seg),
        grid=(B, L // seg),
        in_specs=[
            pl.BlockSpec((None, L, 2 * D_CONV), lambda i, s: (i, 0, 0)),
            pl.BlockSpec((n_lane_tiles, CONV_WIDTH + 1, LANES), lambda i, s: (0, 0, 0)),
            pl.BlockSpec((n_lane_tiles, 1, LANES), lambda i, s: (0, 0, 0)),
            vec, vec, vec,
        ],
        out_specs=pl.BlockSpec((None, seg, D_CONV), lambda i, s: (i, s, 0)),
        out_shape=jax.ShapeDtypeStruct((B, L, D_CONV), BF16),
        scratch_shapes=[
            pltpu.VMEM((8, n_lane_tiles, seg + 2 * CONV_HALO, LANES), F32),
            pltpu.VMEM((n_lane_tiles, seg, LANES), F32),
        ],
        compiler_params=pltpu.CompilerParams(
            dimension_semantics=("arbitrary", "arbitrary"), vmem_limit_bytes=VMEM_LIMIT),
        name="conformer_conv",
    )(uc, w, b, lng, lnb, og)


GROUP_ROWS = 8
TILE_GROUPS = (2 * TOKEN_TILE + (GROUP_ROWS - 1) * N_EXPERTS + GROUP_ROWS - 1) // GROUP_ROWS
TILE_GROUPS = (TILE_GROUPS + 15) // 16 * 16
TILE_SLOTS = TILE_GROUPS * GROUP_ROWS
BLOCK_GROUPS = EXPERT_BLOCK // GROUP_ROWS
X_SLOTS = 3
ROUTER_ROWS = 128
EXPERT_ROW0 = 8
COL_P0, COL_P1, COL_W0, COL_W1 = range(4)


U32 = jnp.uint32
PACKED = D_MODEL // 2


def _pack_bf16_pairs(x):
    bits = pltpu.bitcast(x, U32)
    return bits[:, PACKED:] | (bits[:, :PACKED] >> 16)


def _unpack_bf16_pairs(words):
    lo = pltpu.bitcast(words << 16, F32)
    hi = pltpu.bitcast(words & jnp.uint32(0xFFFF0000), F32)
    return jnp.concatenate([lo, hi], axis=1).astype(BF16)


def _sorted_out_proj_body(x_ref, a_ref, c_ref, ag_ref, wa_ref, wc_ref, n2g_ref, wrt_ref, brt_ref,
                          x1_ref, hs_ref, route_ref, grp_ref, *, tm):
    an = _rms(a_ref[...].astype(F32), ag_ref[...]).astype(BF16)
    y = jnp.dot(an, wa_ref[...], preferred_element_type=F32)
    y = y + jnp.dot(c_ref[...], wc_ref[...], preferred_element_type=F32)
    x1 = x_ref[...] + y
    x1_ref[...] = x1
    h2 = _rms(x1, n2g_ref[...])

    h_hi = h2.astype(BF16)
    h_lo = (h2 - h_hi.astype(F32)).astype(BF16)
    nt = (((1,), (1,)), ((), ()))
    split = (lax.dot_general(wrt_ref[...], h_hi, nt, preferred_element_type=F32)
             + lax.dot_general(wrt_ref[...], h_lo, nt, preferred_element_type=F32))
    logits = split[:ROUTER_ROWS] + split[ROUTER_ROWS:] + brt_ref[...]

    gtop = logits[0:1]
    gsel = jnp.zeros((1, tm), F32)
    for g in range(1, N_GROUPS):
        cand = logits[g:g + 1]
        better = cand > gtop
        gsel = jnp.where(better, float(g), gsel)
        gtop = jnp.where(better, cand, gtop)
    denom = jnp.zeros((1, tm), F32)
    for g in range(N_GROUPS):
        denom = denom + jnp.exp(logits[g:g + 1] - gtop)
    p_g = 1.0 / denom

    el = logits[EXPERT_ROW0:EXPERT_ROW0 + EPG]
    for g in range(1, N_GROUPS):
        el = jnp.where(gsel == float(g), logits[EXPERT_ROW0 + g * EPG:EXPERT_ROW0 + (g + 1) * EPG], el)
    ninf = jnp.full((1, tm), -jnp.inf, F32)
    v0, v1 = ninf, ninf
    i0 = jnp.zeros((1, tm), F32)
    i1 = jnp.zeros((1, tm), F32)
    for j in range(EPG):
        cand = el[j:j + 1]
        gt0 = cand > v0
        gt1 = cand > v1
        v1 = jnp.where(gt0, v0, jnp.where(gt1, cand, v1))
        i1 = jnp.where(gt0, i0, jnp.where(gt1, float(j), i1))
        v0 = jnp.where(gt0, cand, v0)
        i0 = jnp.where(gt0, float(j), i0)
    t = jnp.exp(v1 - v0)
    w0 = p_g / (1.0 + t)
    w1 = p_g * t / (1.0 + t)
    e0 = gsel * EPG + i0
    e1 = gsel * EPG + i1

    expert = lax.broadcasted_iota(I32, (N_EXPERTS, tm), 0).astype(F32)
    hit0 = expert == e0
    hit1 = expert == e1
    onehot = jnp.where(hit0 | hit1, 1.0, 0.0).astype(BF16)
    ri = lax.broadcasted_iota(I32, (tm, tm), 0)
    ci = lax.broadcasted_iota(I32, (tm, tm), 1)
    upper = jnp.where(ri <= ci, 1.0, 0.0).astype(BF16)
    cum = jnp.dot(onehot, upper, preferred_element_type=F32)
    count = jnp.dot(onehot, jnp.ones((tm, LANES), BF16), preferred_element_type=F32)
    groups = jnp.floor((count + (GROUP_ROWS - 1)) * (1.0 / GROUP_ROWS))
    er = lax.broadcasted_iota(I32, (N_EXPERTS, N_EXPERTS), 0)
    ec = lax.broadcasted_iota(I32, (N_EXPERTS, N_EXPERTS), 1)
    below = jnp.where(ec < er, 1.0, 0.0).astype(BF16)
    run_start = jnp.dot(below, groups.astype(BF16), preferred_element_type=F32) * GROUP_ROWS
    run_start = jnp.concatenate([run_start] * (tm // LANES), axis=1)
    pos = run_start + cum - 1.0
    p0 = jnp.sum(jnp.where(hit0, pos, 0.0), axis=0, keepdims=True)
    p1 = jnp.sum(jnp.where(hit1, pos, 0.0), axis=0, keepdims=True)
    grp_ref[...] = groups

    slot = lax.broadcasted_iota(I32, (TILE_SLOTS, tm), 0).astype(F32)
    perm = jnp.where((slot == p0) | (slot == p1), 1.0, 0.0).astype(BF16)
    hs_ref[...] = _pack_bf16_pairs(jnp.dot(perm, h_hi, preferred_element_type=F32))

    rows = jnp.concatenate([p0, p1, w0, w1, jnp.zeros((ROUTER_ROWS - 4, tm), F32)], axis=0)
    route_ref[...] = rows.T


def _sorted_out_proj(x2, a2, c2, ag, wa, wc, n2g, wrt, brt):
    T = x2.shape[0]
    tm = TOKEN_TILE
    nt = T // tm
    full = lambda shape: pl.BlockSpec(shape, lambda i: (0,) * len(shape))
    return pl.pallas_call(
        functools.partial(_sorted_out_proj_body, tm=tm),
        grid=(nt,),
        in_specs=[
            pl.BlockSpec((tm, D_MODEL), lambda i: (i, 0)),
            pl.BlockSpec((tm, D_ATTN), lambda i: (i, 0)),
            pl.BlockSpec((tm, D_CONV), lambda i: (i, 0)),
            full((1, D_ATTN)),
            full((D_ATTN, D_MODEL)),
            full((D_CONV, D_MODEL)),
            full((1, D_MODEL)),
            full((2 * ROUTER_ROWS, D_MODEL)),
            full((ROUTER_ROWS, 1)),
        ],
        out_specs=[
            pl.BlockSpec((tm, D_MODEL), lambda i: (i, 0)),
            pl.BlockSpec((TILE_SLOTS, PACKED), lambda i: (i, 0)),
            pl.BlockSpec((tm, LANES), lambda i: (i, 0)),
            pl.BlockSpec((None, N_EXPERTS, LANES), lambda i: (i, 0, 0)),
        ],
        out_shape=[
            jax.ShapeDtypeStruct((T, D_MODEL), F32),
            jax.ShapeDtypeStruct((nt * TILE_SLOTS, PACKED), U32),
            jax.ShapeDtypeStruct((T, LANES), F32),
            jax.ShapeDtypeStruct((nt, N_EXPERTS, LANES), F32),
        ],
        compiler_params=pltpu.CompilerParams(
            dimension_semantics=("arbitrary",), vmem_limit_bytes=VMEM_LIMIT),
        name="out_proj_router",
    )(x2, a2, c2, ag, wa, wc, n2g, wrt, brt)


def _group_copies(src_ref, hs_hbm, ys_hbm, xbuf, ybuf, gsem, ssem, block, slot, trash0, to_trash):
    def group_index(i):
        return src_ref[block * BLOCK_GROUPS + i]

    def rows_of(i):
        start = i * GROUP_ROWS
        return pl.ds(start if isinstance(i, int) else pl.multiple_of(start, GROUP_ROWS), GROUP_ROWS)

    def gather(i):
        g = jnp.maximum(group_index(i), 0)
        return pltpu.make_async_copy(hs_hbm.at[g], xbuf.at[slot, rows_of(i)], gsem.at[slot])

    def scatter(i):
        g = group_index(i)
        g = jnp.where((g < 0) | to_trash, trash0 + slot * BLOCK_GROUPS + i, g)
        return pltpu.make_async_copy(ybuf.at[slot, rows_of(i)], ys_hbm.at[g], ssem.at[slot])

    return gather, scatter


def _sorted_expert_body(be_ref, nu_ref, src_ref, tg_ref, hs_hbm, wg_ref, wu_ref, wd_ref, ys_hbm,
                        xbuf, ybuf, zero_ref, gsem, ssem, zsem, *, n_tiles, trash0):
    del be_ref
    b = pl.program_id(0)
    n_used = nu_ref[0]
    slot = b % 2

    @pl.when(b == 0)
    def _():
        zero_ref[...] = jnp.zeros_like(zero_ref)

        def zero_copy(g):
            return pltpu.make_async_copy(zero_ref, ys_hbm.at[g], zsem)

        def over_unused(fn):
            def tile(c, carry):
                def group(g, carry2):
                    fn(c * TILE_GROUPS + g)
                    return carry2
                return lax.fori_loop(tg_ref[c], TILE_GROUPS, group, carry)
            lax.fori_loop(0, n_tiles, tile, 0)

            def trash(i, carry):
                fn(trash0 + i)
                return carry
            lax.fori_loop(0, 2 * BLOCK_GROUPS, trash, 0)

        over_unused(lambda g: zero_copy(g).start())
        over_unused(lambda g: zero_copy(g).wait())

    def for_groups(fn):
        def step(i, carry):
            fn(i)
            return carry
        lax.fori_loop(0, BLOCK_GROUPS, step, 0, unroll=8)

    def copies(block, slot_, to_trash=False):
        return _group_copies(src_ref, hs_hbm, ys_hbm, xbuf, ybuf, gsem, ssem, block, slot_, trash0,
                             to_trash)

    def wait_gathers(slot_):
        gather, _ = copies(0, slot_)
        for_groups(lambda i: gather(i).wait())

    def wait_scatters(slot_):
        _, scatter = copies(0, slot_)
        for_groups(lambda i: scatter(i).wait())

    xslot = b % X_SLOTS

    def last(block):
        return jnp.minimum(block, n_used - 1)

    @pl.when((b == 0) & (n_used > 0))
    def _():
        ybuf[...] = jnp.zeros_like(ybuf)
        for block in range(X_SLOTS - 1):
            gather, _ = copies(last(block), block)
            for_groups(lambda i: gather(i).start())
        _, scatter = copies(0, 0, to_trash=True)
        for_groups(lambda i: scatter(i).start())

    @pl.when(b < n_used)
    def _():
        wait_gathers(xslot)
        ahead = b + (X_SLOTS - 1)
        gather, _ = copies(last(ahead), ahead % X_SLOTS)
        _, scatter = copies(jnp.maximum(b - 1, 0), 1 - slot, to_trash=b == 0)
        for i in range(BLOCK_GROUPS):
            gather(i).start()
            scatter(i).start()
        x = _unpack_bf16_pairs(xbuf[xslot])
        g = jnp.dot(x, wg_ref[...], preferred_element_type=F32)
        u = jnp.dot(x, wu_ref[...], preferred_element_type=F32)
        hid = (g * jax.nn.sigmoid(g) * u).astype(BF16)
        y = jnp.dot(hid, wd_ref[...], preferred_element_type=F32)
        packed = _pack_bf16_pairs(y.astype(BF16).astype(F32))
        _, landed = copies(0, slot)
        for i in range(BLOCK_GROUPS):
            landed(i).wait()
        ybuf[slot] = packed

    @pl.when((b == n_used) & (n_used > 0))
    def _():
        for ahead in range(X_SLOTS - 1):
            wait_gathers((b + ahead) % X_SLOTS)
        wait_scatters(slot)
        _, scatter = copies(b - 1, 1 - slot)
        for_groups(lambda i: scatter(i).start())
        wait_scatters(1 - slot)


def _sorted_experts(block_e, n_used, src, tile_groups, hs3, wg, wu, wd):
    n_groups = hs3.shape[0]
    n_tiles = tile_groups.shape[0]
    n_blocks = block_e.shape[0]
    blk = EXPERT_BLOCK
    grid_spec = pltpu.PrefetchScalarGridSpec(
        num_scalar_prefetch=4,
        grid=(n_blocks,),
        in_specs=[
            pl.BlockSpec(memory_space=pl.ANY),
            pl.BlockSpec((None, D_MODEL, D_EXPERT), lambda b, be, nu, src, tg: (be[b], 0, 0)),
            pl.BlockSpec((None, D_MODEL, D_EXPERT), lambda b, be, nu, src, tg: (be[b], 0, 0)),
            pl.BlockSpec((None, D_EXPERT, D_MODEL), lambda b, be, nu, src, tg: (be[b], 0, 0)),
        ],
        out_specs=pl.BlockSpec(memory_space=pl.ANY),
        scratch_shapes=[
            pltpu.VMEM((X_SLOTS, blk, PACKED), U32),
            pltpu.VMEM((2, blk, PACKED), U32),
            pltpu.VMEM((GROUP_ROWS, PACKED), U32),
            pltpu.SemaphoreType.DMA((X_SLOTS,)),
            pltpu.SemaphoreType.DMA((2,)),
            pltpu.SemaphoreType.DMA(()),
        ],
    )
    return pl.pallas_call(
        functools.partial(_sorted_expert_body, n_tiles=n_tiles, trash0=n_groups),
        grid_spec=grid_spec,
        out_shape=jax.ShapeDtypeStruct((n_groups + 2 * BLOCK_GROUPS, GROUP_ROWS, PACKED), U32),
        compiler_params=pltpu.CompilerParams(
            dimension_semantics=("arbitrary",), vmem_limit_bytes=VMEM_LIMIT),
        name="moe_experts",
    )(block_e, n_used, src, tile_groups, hs3, wg, wu, wd)


def _sorted_combine_body(x1_ref, route_ref, fg_ref, ys_ref, o_ref, *, tm):
    ys = _unpack_bf16_pairs(ys_ref[...])
    route = route_ref[...]
    slot = lax.broadcasted_iota(I32, (tm, TILE_SLOTS), 1).astype(F32)
    gate = (jnp.where(slot == route[:, COL_P0:COL_P0 + 1], route[:, COL_W0:COL_W0 + 1], 0.0)
            + jnp.where(slot == route[:, COL_P1:COL_P1 + 1], route[:, COL_W1:COL_W1 + 1], 0.0))
    moe = jnp.dot(gate.astype(BF16), ys, preferred_element_type=F32)
    o_ref[...] = _rms(x1_ref[...] + moe, fg_ref[...])


def _sorted_combine(x1, route, fg, ys2):
    T = x1.shape[0]
    tm = TOKEN_TILE
    return pl.pallas_call(
        functools.partial(_sorted_combine_body, tm=tm),
        grid=(T // tm,),
        in_specs=[
            pl.BlockSpec((tm, D_MODEL), lambda i: (i, 0)),
            pl.BlockSpec((tm, LANES), lambda i: (i, 0)),
            pl.BlockSpec((1, D_MODEL), lambda i: (0, 0)),
            pl.BlockSpec((TILE_SLOTS, PACKED), lambda i: (i, 0)),
        ],
        out_specs=pl.BlockSpec((tm, D_MODEL), lambda i: (i, 0)),
        out_shape=jax.ShapeDtypeStruct((T, D_MODEL), F32),
        compiler_params=pltpu.CompilerParams(
            dimension_semantics=("arbitrary",), vmem_limit_bytes=VMEM_LIMIT),
        name="moe_combine",
    )(x1, route, fg, ys2)


def _moe_plan(groups):
    nt = groups.shape[0]
    run_start = jnp.cumsum(groups, axis=1) - groups
    cum_tiles = jnp.cumsum(groups, axis=0)
    total = cum_tiles[-1]
    blocks = (total + BLOCK_GROUPS - 1) // BLOCK_GROUPS
    blk_end = jnp.cumsum(blocks)
    blk_start = blk_end - blocks
    n_blocks = (2 * TOKEN_TILE + (GROUP_ROWS - 1) * N_EXPERTS) * nt // EXPERT_BLOCK + N_EXPERTS + 1
    b = jnp.arange(n_blocks, dtype=I32)
    block_e = jnp.minimum(jnp.sum(b[:, None] >= blk_end[None, :], axis=1), N_EXPERTS - 1).astype(I32)
    n_used = blk_end[-1:].astype(I32)

    sg = jnp.arange(n_blocks * BLOCK_GROUPS, dtype=I32)
    sb = sg // BLOCK_GROUPS
    sg_e = jnp.broadcast_to(block_e[:, None], (n_blocks, BLOCK_GROUPS)).reshape(-1)
    onehot_e = (sg_e[:, None] == jnp.arange(N_EXPERTS, dtype=I32)[None, :]).astype(F32)

    def pick(table):
        return jnp.dot(onehot_e, table.astype(F32), precision=lax.Precision.HIGHEST).astype(I32)

    j = sg - pick(blk_start[:, None])[:, 0] * BLOCK_GROUPS
    valid = (sb < n_used[0]) & (j < pick(total[:, None])[:, 0])
    cum_e = pick(cum_tiles.T)
    tile = jnp.minimum(jnp.sum(cum_e <= j[:, None], axis=1), nt - 1)
    tile_hot = tile[:, None] == jnp.arange(nt, dtype=I32)[None, :]
    before = jnp.sum(jnp.where(tile_hot, cum_e - pick(groups.T), 0), axis=1)
    start = jnp.sum(jnp.where(tile_hot, pick(run_start.T), 0), axis=1)
    src = jnp.where(valid, tile * TILE_GROUPS + start + (j - before), -1).astype(I32)
    tile_groups = jnp.sum(groups, axis=1).astype(I32)
    return block_e, n_used, src, tile_groups


def _trunk(x, p):
    B, L, _ = x.shape
    T = B * L
    x2 = x.reshape(T, D_MODEL)
    qkv, uc = _in_proj(x2, p["norm1_g"], p["wqkv"], p["wconv"])
    a = _attention(qkv.reshape(B, L, 3 * D_ATTN), p["na_bias"])
    c = _conv(uc.reshape(B, L, 2 * D_CONV), p["conv_w"], p["conv_b"], p["conv_ln_g"],
              p["conv_ln_b"], p["conv_out_g"])
    x1, hs, route, grp = _sorted_out_proj(
        x2, a.reshape(T, D_ATTN), c.reshape(T, D_CONV), p["attn_out_g"], p["wout_a"],
        p["wout_c"], p["norm2_g"], p["w_router_t"], p["b_router_t"])
    block_e, n_used, src, tile_groups = _moe_plan(grp[:, :, 0].astype(I32))
    hs3 = hs.reshape(hs.shape[0] // GROUP_ROWS, GROUP_ROWS, PACKED)
    ys3 = _sorted_experts(block_e, n_used, src, tile_groups, hs3, p["w_gate"], p["w_up"], p["w_down"])
    ys2 = ys3.reshape(ys3.shape[0] * GROUP_ROWS, PACKED)
    out = _sorted_combine(x1, route, p["final_g"], ys2)
    return out.reshape(B, L, D_MODEL)


def kernel(x_prompt, x_sample, norm1_g, w_in, rpb, attn_out_g, conv_w, conv_b, conv_ln_g,
           conv_ln_b, conv_out_g, w_out, norm2_g, w_group, b_group, w_expert, b_expert,
           w_e_gate, w_e_up, w_e_down, final_g):
    l = 0
    w_router_t = jnp.zeros((ROUTER_ROWS, D_MODEL), F32)
    w_router_t = w_router_t.at[:N_GROUPS].set(w_group[l].T)
    w_router_t = w_router_t.at[EXPERT_ROW0:EXPERT_ROW0 + N_EXPERTS].set(
        w_expert[l].transpose(0, 2, 1).reshape(N_EXPERTS, D_MODEL))
    w_router_hi = w_router_t.astype(BF16)
    w_router_lo = (w_router_t - w_router_hi.astype(F32)).astype(BF16)
    w_router_t = jnp.concatenate([w_router_hi, w_router_lo], axis=0)
    b_router_t = jnp.zeros((ROUTER_ROWS,), F32)
    b_router_t = b_router_t.at[:N_GROUPS].set(b_group[l])
    b_router_t = b_router_t.at[EXPERT_ROW0:EXPERT_ROW0 + N_EXPERTS].set(b_expert[l].reshape(N_EXPERTS))
    b_router_t = b_router_t.reshape(ROUTER_ROWS, 1)
    p = {
        "norm1_g": norm1_g[l].reshape(1, D_MODEL),
        "wqkv": w_in[l][:, :3 * D_ATTN].astype(BF16),
        "wconv": w_in[l][:, 3 * D_ATTN:].astype(BF16),
        "na_bias": _na_bias_table(rpb[l]),
        "attn_out_g": attn_out_g[l].reshape(1, D_ATTN),
        "conv_w": jnp.pad(conv_w[l], ((0, 1), (0, 0))),
        "conv_b": conv_b[l].reshape(1, D_CONV),
        "conv_ln_g": conv_ln_g[l].reshape(1, D_CONV),
        "conv_ln_b": conv_ln_b[l].reshape(1, D_CONV),
        "conv_out_g": conv_out_g[l].reshape(1, D_CONV),
        "wout_a": w_out[l][:D_ATTN].astype(BF16),
        "wout_c": w_out[l][D_ATTN:].astype(BF16),
        "norm2_g": norm2_g[l].reshape(1, D_MODEL),
        "w_router_t": w_router_t,
        "b_router_t": b_router_t,
        "w_gate": w_e_gate[l].astype(BF16),
        "w_up": w_e_up[l].astype(BF16),
        "w_down": w_e_down[l].astype(BF16),
        "final_g": final_g.reshape(1, D_MODEL),
    }
    return (_trunk(x_prompt, p), _trunk(x_sample, p))
```

```python
import functools

import jax
import jax.numpy as jnp
from jax import lax
from jax.experimental import pallas as pl
from jax.experimental.pallas import tpu as pltpu

F32 = jnp.float32
BF16 = jnp.bfloat16
I32 = jnp.int32

D_MODEL = 1024
GRID_W = 64
D_ATTN = 512
D_CONV = 512
HEAD_DIM = 64
N_HEADS = 8
NA_ROWS = 8
NA_COLS = 16
CONV_WIDTH = 31
N_GROUPS = 4
EPG = 8
N_EXPERTS = 32
D_EXPERT = 512
EPS = 1e-6

LANES = 128
HEAD_PAIRS = N_HEADS * HEAD_DIM // LANES
NA_KEYS = NA_ROWS * GRID_W
MASKED = -1e30
TOKEN_TILE = 512
EXPERT_BLOCK = 512
CONV_TILE = 64
CONV_SEGMENT = 512
CONV_HALO = 16
VMEM_LIMIT = 56 * 1024 * 1024


def _rms(x, g):
    return x * lax.rsqrt(jnp.mean(x * x, axis=-1, keepdims=True) + EPS) * g


def _in_proj_body(x_ref, g_ref, wqkv_ref, wc_ref, qkv_ref, uc_ref):
    h = _rms(x_ref[...], g_ref[...]).astype(BF16)
    qkv_ref[...] = jnp.dot(h, wqkv_ref[...], preferred_element_type=F32).astype(BF16)
    uc_ref[...] = jnp.dot(h, wc_ref[...], preferred_element_type=F32).astype(BF16)


def _in_proj(x2, g, wqkv, wc):
    T = x2.shape[0]
    tm = TOKEN_TILE
    return pl.pallas_call(
        _in_proj_body,
        grid=(T // tm,),
        in_specs=[
            pl.BlockSpec((tm, D_MODEL), lambda i: (i, 0)),
            pl.BlockSpec((1, D_MODEL), lambda i: (0, 0)),
            pl.BlockSpec((D_MODEL, 3 * D_ATTN), lambda i: (0, 0)),
            pl.BlockSpec((D_MODEL, 2 * D_CONV), lambda i: (0, 0)),
        ],
        out_specs=[
            pl.BlockSpec((tm, 3 * D_ATTN), lambda i: (i, 0)),
            pl.BlockSpec((tm, 2 * D_CONV), lambda i: (i, 0)),
        ],
        out_shape=[
            jax.ShapeDtypeStruct((T, 3 * D_ATTN), BF16),
            jax.ShapeDtypeStruct((T, 2 * D_CONV), BF16),
        ],
        compiler_params=pltpu.CompilerParams(
            dimension_semantics=("arbitrary",), vmem_limit_bytes=VMEM_LIMIT),
        name="in_proj",
    )(x2, g, wqkv, wc)


def _na_bias_table(rpb):
    c = jnp.arange(GRID_W)
    col_start = jnp.clip(c - NA_COLS // 2, 0, GRID_W - NA_COLS)
    cp = jnp.arange(GRID_W)
    valid = (cp[None, :] >= col_start[:, None]) & (cp[None, :] < col_start[:, None] + NA_COLS)
    col_off = cp[None, :] - c[:, None] + (NA_COLS - 1)
    sel = (col_off[None] == jnp.arange(2 * NA_COLS - 1)[:, None, None]) & valid[None]
    a = jnp.einsum("hrd,dcx->hrcx", rpb, sel.astype(F32), precision=lax.Precision.HIGHEST)
    a = jnp.where(valid[None, None], a, MASKED)
    t = jnp.stack([a[:, NA_ROWS - 1 - p:2 * NA_ROWS - 1 - p] for p in range(NA_ROWS)], axis=1)
    t = t.transpose(0, 1, 3, 2, 4)
    t = t.reshape(HEAD_PAIRS, 2, NA_ROWS, GRID_W, NA_KEYS)
    return t.transpose(0, 2, 1, 3, 4).reshape(HEAD_PAIRS, NA_ROWS, LANES, NA_KEYS).astype(F32)


def _attn_body(q_ref, k_ref, v_ref, bias_ref, o_ref, s_ref, p_ref, l_ref, *, rows, rows_per_step):
    rb = pl.program_id(2)
    first_head = lax.broadcasted_iota(I32, (GRID_W, LANES), 1) < HEAD_DIM
    zero = jnp.zeros((GRID_W, LANES), BF16)

    def window(rr):
        r = rb * rows_per_step + rr
        r_start = jnp.clip(r - NA_ROWS // 2, 0, rows - NA_ROWS)
        return r - r_start, pl.multiple_of(r_start * GRID_W, GRID_W)

    for rr in range(rows_per_step):
        p, k0 = window(rr)
        q = q_ref[rr * GRID_W:(rr + 1) * GRID_W, :] * jnp.asarray(HEAD_DIM ** -0.5, BF16)
        qbd = jnp.concatenate([jnp.where(first_head, q, zero), jnp.where(first_head, zero, q)], axis=0)
        ks = k_ref[pl.ds(k0, NA_KEYS), :]
        s = lax.dot_general(qbd, ks, (((1,), (1,)), ((), ())), preferred_element_type=F32)
        s_ref[rr] = s + bias_ref[p]
    for rr in range(rows_per_step):
        s = s_ref[rr]
        e = jnp.exp(s - jnp.max(s, axis=-1, keepdims=True))
        l_ref[rr] = jnp.broadcast_to(1.0 / jnp.sum(e, axis=-1, keepdims=True), (LANES, LANES))
        p_ref[rr] = e.astype(BF16)
    for rr in range(rows_per_step):
        _, k0 = window(rr)
        vs = v_ref[pl.ds(k0, NA_KEYS), :]
        o = jnp.dot(p_ref[rr], vs, preferred_element_type=F32) * l_ref[rr]
        out = jnp.where(first_head, o[:GRID_W], o[GRID_W:])
        o_ref[rr * GRID_W:(rr + 1) * GRID_W, :] = out.astype(BF16)


def _attention(qkv, bias):
    B, L, _ = qkv.shape
    rows = L // GRID_W
    rows_per_step = 16
    tq = rows_per_step * GRID_W
    body = functools.partial(_attn_body, rows=rows, rows_per_step=rows_per_step)
    return pl.pallas_call(
        body,
        grid=(HEAD_PAIRS, B, L // tq),
        in_specs=[
            pl.BlockSpec((None, tq, LANES), lambda hp, b, rb: (b, rb, hp)),
            pl.BlockSpec((None, L, LANES), lambda hp, b, rb: (b, 0, HEAD_PAIRS + hp)),
            pl.BlockSpec((None, L, LANES), lambda hp, b, rb: (b, 0, 2 * HEAD_PAIRS + hp)),
            pl.BlockSpec((None, NA_ROWS, LANES, NA_KEYS), lambda hp, b, rb: (hp, 0, 0, 0)),
        ],
        out_specs=pl.BlockSpec((None, tq, LANES), lambda hp, b, rb: (b, rb, hp)),
        out_shape=jax.ShapeDtypeStruct((B, L, D_ATTN), BF16),
        scratch_shapes=[
            pltpu.VMEM((rows_per_step, LANES, NA_KEYS), F32),
            pltpu.VMEM((rows_per_step, LANES, NA_KEYS), BF16),
            pltpu.VMEM((rows_per_step, LANES, LANES), F32),
        ],
        compiler_params=pltpu.CompilerParams(
            dimension_semantics=("arbitrary", "arbitrary", "arbitrary"),
            vmem_limit_bytes=VMEM_LIMIT),
        name="na_attention",
    )(qkv, qkv, qkv, bias)


def _glu(uc_rows):
    a = uc_rows[:, 0:D_CONV].astype(F32)
    gt = uc_rows[:, D_CONV:2 * D_CONV].astype(F32)
    return a * jax.nn.sigmoid(gt)


def _conv_body(uc_ref, w_ref, b_ref, lng_ref, lnb_ref, og_ref, o_ref, zs_ref, cbuf_ref, *, seg):
    sg = pl.program_id(1)
    n_seg = pl.num_programs(1)
    seg0 = pl.multiple_of(sg * seg, seg)
    halo = CONV_HALO
    n_lane_tiles = D_CONV // LANES

    def put_z(u0, n, z):
        for j in range(n_lane_tiles):
            zs_ref[0, j, pl.ds(u0, n), :] = z[:, j * LANES:(j + 1) * LANES]

    glu_rows = 128

    def glu(i, carry):
        u0 = pl.multiple_of(i * glu_rows, glu_rows)
        put_z(halo + u0, glu_rows, _glu(uc_ref[pl.ds(seg0 + u0, glu_rows), :]))
        return carry

    lax.fori_loop(0, seg // glu_rows, glu, 0)

    @pl.when(sg > 0)
    def _():
        put_z(0, halo, _glu(uc_ref[pl.ds(seg0 - halo, halo), :]))

    @pl.when(sg == 0)
    def _():
        put_z(0, halo, jnp.zeros((halo, D_CONV), F32))

    @pl.when(sg < n_seg - 1)
    def _():
        put_z(halo + seg, halo, _glu(uc_ref[pl.ds(seg0 + seg, halo), :]))

    @pl.when(sg == n_seg - 1)
    def _():
        put_z(halo + seg, halo, jnp.zeros((halo, D_CONV), F32))

    def shift_rows(j, u0, n):
        x = zs_ref[0, j, pl.ds(u0, n + 8), :]
        for s in range(1, 8):
            zs_ref[s, j, pl.ds(u0, n), :] = x[s:s + n]

    shift_chunk = 64
    chunks = seg // shift_chunk

    def shift(i, carry):
        shift_rows(i // chunks, pl.multiple_of((i % chunks) * shift_chunk, shift_chunk), shift_chunk)
        return carry

    lax.fori_loop(0, n_lane_tiles * chunks, shift, 0)
    for j in range(n_lane_tiles):
        shift_rows(j, seg, 2 * halo - 8)

    tt = CONV_TILE
    tiles = seg // tt
    first = halo - CONV_WIDTH // 2

    def taps(i, carry):
        j = i // tiles
        t0 = pl.multiple_of((i % tiles) * tt, tt)
        acc = jnp.broadcast_to(b_ref[j], (tt, LANES))
        for k in range(CONV_WIDTH):
            off = k + first
            acc = acc + w_ref[j, k:k + 1, :] * zs_ref[off % 8, j, pl.ds(t0 + 8 * (off // 8), tt), :]
        cbuf_ref[j, pl.ds(t0, tt), :] = acc
        return carry

    lax.fori_loop(0, n_lane_tiles * tiles, taps, 0, unroll=2)

    norm_rows = 128

    def norm(i, carry):
        t0 = pl.multiple_of(i * norm_rows, norm_rows)
        y = jnp.concatenate([cbuf_ref[j, pl.ds(t0, norm_rows), :] for j in range(n_lane_tiles)], axis=1)
        mu = jnp.mean(y, axis=-1, keepdims=True)
        yc = y - mu
        var = jnp.mean(yc * yc, axis=-1, keepdims=True)
        yn = yc * lax.rsqrt(var + EPS) * lng_ref[...] + lnb_ref[...]
        sw = yn * jax.nn.sigmoid(yn)
        o_ref[pl.ds(t0, norm_rows), :] = _rms(sw, og_ref[...]).astype(BF16)
        return carry

    lax.fori_loop(0, seg // norm_rows, norm, 0, unroll=2)


def _conv(uc, w, b, lng, lnb, og):
    B, L, _ = uc.shape
    seg = CONV_SEGMENT
    n_lane_tiles = D_CONV // LANES
    vec = pl.BlockSpec((1, D_CONV), lambda i, s: (0, 0))
    w = w.reshape(CONV_WIDTH + 1, n_lane_tiles, LANES).transpose(1, 0, 2)
    b = b.reshape(n_lane_tiles, 1, LANES)
    return pl.pallas_call(
        functools.partial(_conv_body, seg=seg),
        grid=(B, L // seg),
        in_specs=[
            pl.BlockSpec((None, L, 2 * D_CONV), lambda i, s: (i, 0, 0)),
            pl.BlockSpec((n_lane_tiles, CONV_WIDTH + 1, LANES), lambda i, s: (0, 0, 0)),
            pl.BlockSpec((n_lane_tiles, 1, LANES), lambda i, s: (0, 0, 0)),
            vec, vec, vec,
        ],
        out_specs=pl.BlockSpec((None, seg, D_CONV), lambda i, s: (i, s, 0)),
        out_shape=jax.ShapeDtypeStruct((B, L, D_CONV), BF16),
        scratch_shapes=[
            pltpu.VMEM((8, n_lane_tiles, seg + 2 * CONV_HALO, LANES), F32),
            pltpu.VMEM((n_lane_tiles, seg, LANES), F32),
        ],
        compiler_params=pltpu.CompilerParams(
            dimension_semantics=("arbitrary", "arbitrary"), vmem_limit_bytes=VMEM_LIMIT),
        name="conformer_conv",
    )(uc, w, b, lng, lnb, og)


UC_RING = 4
COL_TILE = 256


def _in_proj_conv_body(x_ref, g_ref, wqkv_ref, wc_ref, w_ref, b_ref, lng_ref, lnb_ref, og_ref,
                       qkv_ref, o_ref, ring_ref, zs_ref, cbuf_ref, *, seg, n_seg):
    i = pl.program_id(0)
    halo = CONV_HALO
    n_lane_tiles = D_CONV // LANES

    @pl.when(i == 0)
    def _():
        ring_ref[...] = jnp.zeros_like(ring_ref)

    sg = jnp.maximum(i - 2, 0) % n_seg
    prev_slot = (i + UC_RING - 3) % UC_RING
    cur_slot = (i + UC_RING - 2) % UC_RING
    next_slot = (i + UC_RING - 1) % UC_RING
    write_slot = i % UC_RING

    def put_z(u0, n, z):
        for j in range(n_lane_tiles):
            zs_ref[0, j, pl.ds(u0, n), :] = z[:, j * LANES:(j + 1) * LANES]

    glu_rows = 128

    def glu(r, carry):
        u0 = pl.multiple_of(r * glu_rows, glu_rows)
        put_z(halo + u0, glu_rows, _glu(ring_ref[cur_slot, pl.ds(u0, glu_rows), :]))
        return carry

    lax.fori_loop(0, seg // glu_rows, glu, 0)

    @pl.when(sg > 0)
    def _():
        put_z(0, halo, _glu(ring_ref[prev_slot, seg - halo:seg, :]))

    @pl.when(sg == 0)
    def _():
        put_z(0, halo, jnp.zeros((halo, D_CONV), F32))

    @pl.when(sg < n_seg - 1)
    def _():
        put_z(halo + seg, halo, _glu(ring_ref[next_slot, 0:halo, :]))

    @pl.when(sg == n_seg - 1)
    def _():
        put_z(halo + seg, halo, jnp.zeros((halo, D_CONV), F32))

    def shift_rows(j, u0, n):
        x = zs_ref[0, j, pl.ds(u0, n + 8), :]
        for s in range(1, 8):
            zs_ref[s, j, pl.ds(u0, n), :] = x[s:s + n]

    shift_chunk = 64
    chunks = seg // shift_chunk

    def shift(r, carry):
        shift_rows(r // chunks, pl.multiple_of((r % chunks) * shift_chunk, shift_chunk), shift_chunk)
        return carry

    lax.fori_loop(0, n_lane_tiles * chunks, shift, 0)
    for j in range(n_lane_tiles):
        shift_rows(j, seg, 2 * halo - 8)

    tt = CONV_TILE
    first = halo - CONV_WIDTH // 2

    def taps(j, t0):
        acc = jnp.broadcast_to(b_ref[j], (tt, LANES))
        for k in range(CONV_WIDTH):
            off = k + first
            acc = acc + w_ref[j, k:k + 1, :] * zs_ref[off % 8, j, t0 + 8 * (off // 8):t0 + 8 * (off // 8) + tt, :]
        cbuf_ref[j, t0:t0 + tt, :] = acc

    tap_tiles = [(j, t0) for j in range(n_lane_tiles) for t0 in range(0, seg, tt)]

    h = _rms(x_ref[...], g_ref[...]).astype(BF16)
    qkv_chunks = 3 * D_ATTN // COL_TILE
    uc_chunks = 2 * D_CONV // COL_TILE
    n_chunks = qkv_chunks + uc_chunks
    for c in range(n_chunks):
        if c < qkv_chunks:
            cols = slice(c * COL_TILE, (c + 1) * COL_TILE)
            qkv_ref[:, cols] = jnp.dot(h, wqkv_ref[:, cols], preferred_element_type=F32).astype(BF16)
        else:
            cols = slice((c - qkv_chunks) * COL_TILE, (c - qkv_chunks + 1) * COL_TILE)
            ring_ref[write_slot, :, cols] = jnp.dot(
                h, wc_ref[:, cols], preferred_element_type=F32).astype(BF16)
        lo = len(tap_tiles) * c // n_chunks
        hi = len(tap_tiles) * (c + 1) // n_chunks
        for j, t0 in tap_tiles[lo:hi]:
            taps(j, t0)

    norm_rows = 128

    def norm(r, carry):
        t0 = pl.multiple_of(r * norm_rows, norm_rows)
        y = jnp.concatenate([cbuf_ref[j, pl.ds(t0, norm_rows), :] for j in range(n_lane_tiles)], axis=1)
        mu = jnp.mean(y, axis=-1, keepdims=True)
        yc = y - mu
        var = jnp.mean(yc * yc, axis=-1, keepdims=True)
        yn = yc * lax.rsqrt(var + EPS) * lng_ref[...] + lnb_ref[...]
        sw = yn * jax.nn.sigmoid(yn)
        o_ref[pl.ds(t0, norm_rows), :] = _rms(sw, og_ref[...]).astype(BF16)
        return carry

    lax.fori_loop(0, seg // norm_rows, norm, 0, unroll=2)


def _in_proj_conv(x2, L, g, wqkv, wc, w, b, lng, lnb, og):
    T = x2.shape[0]
    seg = CONV_SEGMENT
    assert seg == TOKEN_TILE and L % seg == 0
    nt = T // seg
    n_lane_tiles = D_CONV // LANES
    w = w.reshape(CONV_WIDTH + 1, n_lane_tiles, LANES).transpose(1, 0, 2)
    b = b.reshape(n_lane_tiles, 1, LANES)
    vec = pl.BlockSpec((1, D_CONV), lambda i: (0, 0))
    tile = lambda i: (jnp.minimum(i, nt - 1), 0)
    return pl.pallas_call(
        functools.partial(_in_proj_conv_body, seg=seg, n_seg=L // seg),
        grid=(nt + 2,),
        in_specs=[
            pl.BlockSpec((seg, D_MODEL), tile),
            pl.BlockSpec((1, D_MODEL), lambda i: (0, 0)),
            pl.BlockSpec((D_MODEL, 3 * D_ATTN), lambda i: (0, 0)),
            pl.BlockSpec((D_MODEL, 2 * D_CONV), lambda i: (0, 0)),
            pl.BlockSpec((n_lane_tiles, CONV_WIDTH + 1, LANES), lambda i: (0, 0, 0)),
            pl.BlockSpec((n_lane_tiles, 1, LANES), lambda i: (0, 0, 0)),
            vec, vec, vec,
        ],
        out_specs=[
            pl.BlockSpec((seg, 3 * D_ATTN), tile),
            pl.BlockSpec((seg, D_CONV), lambda i: (jnp.maximum(i - 2, 0), 0)),
        ],
        out_shape=[
            jax.ShapeDtypeStruct((T, 3 * D_ATTN), BF16),
            jax.ShapeDtypeStruct((T, D_CONV), BF16),
        ],
        scratch_shapes=[
            pltpu.VMEM((UC_RING, seg, 2 * D_CONV), BF16),
            pltpu.VMEM((8, n_lane_tiles, seg + 2 * CONV_HALO, LANES), F32),
            pltpu.VMEM((n_lane_tiles, seg, LANES), F32),
        ],
        compiler_params=pltpu.CompilerParams(
            dimension_semantics=("arbitrary",), vmem_limit_bytes=VMEM_LIMIT),
        name="in_proj_conv",
    )(x2, g, wqkv, wc, w, b, lng, lnb, og)


GROUP_ROWS = 8
TILE_GROUPS = (2 * TOKEN_TILE + (GROUP_ROWS - 1) * N_EXPERTS + GROUP_ROWS - 1) // GROUP_ROWS
TILE_GROUPS = (TILE_GROUPS + 15) // 16 * 16
TILE_SLOTS = TILE_GROUPS * GROUP_ROWS
BLOCK_GROUPS = EXPERT_BLOCK // GROUP_ROWS
X_SLOTS = 3
ROUTER_ROWS = 128
EXPERT_ROW0 = 8
COL_P0, COL_P1, COL_W0, COL_W1 = range(4)


U32 = jnp.uint32
PACKED = D_MODEL // 2


def _pack_bf16_pairs(x):
    bits = pltpu.bitcast(x, U32)
    return bits[:, PACKED:] | (bits[:, :PACKED] >> 16)


def _unpack_bf16_pairs(words):
    lo = pltpu.bitcast(words << 16, F32)
    hi = pltpu.bitcast(words & jnp.uint32(0xFFFF0000), F32)
    return jnp.concatenate([lo, hi], axis=1).astype(BF16)


def _sorted_out_proj_body(x_ref, a_ref, c_ref, ag_ref, wa_ref, wc_ref, n2g_ref, wrt_ref, brt_ref,
                          x1_ref, hs_ref, route_ref, grp_ref, *, tm):
    an = _rms(a_ref[...].astype(F32), ag_ref[...]).astype(BF16)
    y = jnp.dot(an, wa_ref[...], preferred_element_type=F32)
    y = y + jnp.dot(c_ref[...], wc_ref[...], preferred_element_type=F32)
    x1 = x_ref[...] + y
    x1_ref[...] = x1
    h2 = _rms(x1, n2g_ref[...])

    h_hi = h2.astype(BF16)
    h_lo = (h2 - h_hi.astype(F32)).astype(BF16)
    nt = (((1,), (1,)), ((), ()))
    split = (lax.dot_general(wrt_ref[...], h_hi, nt, preferred_element_type=F32)
             + lax.dot_general(wrt_ref[...], h_lo, nt, preferred_element_type=F32))
    logits = split[:ROUTER_ROWS] + split[ROUTER_ROWS:] + brt_ref[...]

    gtop = logits[0:1]
    gsel = jnp.zeros((1, tm), F32)
    for g in range(1, N_GROUPS):
        cand = logits[g:g + 1]
        better = cand > gtop
        gsel = jnp.where(better, float(g), gsel)
        gtop = jnp.where(better, cand, gtop)
    denom = jnp.zeros((1, tm), F32)
    for g in range(N_GROUPS):
        denom = denom + jnp.exp(logits[g:g + 1] - gtop)
    p_g = 1.0 / denom

    el = logits[EXPERT_ROW0:EXPERT_ROW0 + EPG]
    for g in range(1, N_GROUPS):
        el = jnp.where(gsel == float(g), logits[EXPERT_ROW0 + g * EPG:EXPERT_ROW0 + (g + 1) * EPG], el)
    ninf = jnp.full((1, tm), -jnp.inf, F32)
    v0, v1 = ninf, ninf
    i0 = jnp.zeros((1, tm), F32)
    i1 = jnp.zeros((1, tm), F32)
    for j in range(EPG):
        cand = el[j:j + 1]
        gt0 = cand > v0
        gt1 = cand > v1
        v1 = jnp.where(gt0, v0, jnp.where(gt1, cand, v1))
        i1 = jnp.where(gt0, i0, jnp.where(gt1, float(j), i1))
        v0 = jnp.where(gt0, cand, v0)
        i0 = jnp.where(gt0, float(j), i0)
    t = jnp.exp(v1 - v0)
    w0 = p_g / (1.0 + t)
    w1 = p_g * t / (1.0 + t)
    e0 = gsel * EPG + i0
    e1 = gsel * EPG + i1

    expert = lax.broadcasted_iota(I32, (N_EXPERTS, tm), 0).astype(F32)
    hit0 = expert == e0
    hit1 = expert == e1
    onehot = jnp.where(hit0 | hit1, 1.0, 0.0).astype(BF16)
    ri = lax.broadcasted_iota(I32, (tm, tm), 0)
    ci = lax.broadcasted_iota(I32, (tm, tm), 1)
    upper = jnp.where(ri <= ci, 1.0, 0.0).astype(BF16)
    cum = jnp.dot(onehot, upper, preferred_element_type=F32)
    count = jnp.dot(onehot, jnp.ones((tm, LANES), BF16), preferred_element_type=F32)
    groups = jnp.floor((count + (GROUP_ROWS - 1)) * (1.0 / GROUP_ROWS))
    er = lax.broadcasted_iota(I32, (N_EXPERTS, N_EXPERTS), 0)
    ec = lax.broadcasted_iota(I32, (N_EXPERTS, N_EXPERTS), 1)
    below = jnp.where(ec < er, 1.0, 0.0).astype(BF16)
    run_start = jnp.dot(below, groups.astype(BF16), preferred_element_type=F32) * GROUP_ROWS
    run_start = jnp.concatenate([run_start] * (tm // LANES), axis=1)
    pos = run_start + cum - 1.0
    p0 = jnp.sum(jnp.where(hit0, pos, 0.0), axis=0, keepdims=True)
    p1 = jnp.sum(jnp.where(hit1, pos, 0.0), axis=0, keepdims=True)
    grp_ref[...] = groups

    slot = lax.broadcasted_iota(I32, (TILE_SLOTS, tm), 0).astype(F32)
    perm = jnp.where((slot == p0) | (slot == p1), 1.0, 0.0).astype(BF16)
    hs_ref[...] = _pack_bf16_pairs(jnp.dot(perm, h_hi, preferred_element_type=F32))

    rows = jnp.concatenate([p0, p1, w0, w1, jnp.zeros((ROUTER_ROWS - 4, tm), F32)], axis=0)
    route_ref[...] = rows.T


def _sorted_out_proj(x2, a2, c2, ag, wa, wc, n2g, wrt, brt):
    T = x2.shape[0]
    tm = TOKEN_TILE
    nt = T // tm
    full = lambda shape: pl.BlockSpec(shape, lambda i: (0,) * len(shape))
    return pl.pallas_call(
        functools.partial(_sorted_out_proj_body, tm=tm),
        grid=(nt,),
        in_specs=[
            pl.BlockSpec((tm, D_MODEL), lambda i: (i, 0)),
            pl.BlockSpec((tm, D_ATTN), lambda i: (i, 0)),
            pl.BlockSpec((tm, D_CONV), lambda i: (i, 0)),
            full((1, D_ATTN)),
            full((D_ATTN, D_MODEL)),
            full((D_CONV, D_MODEL)),
            full((1, D_MODEL)),
            full((2 * ROUTER_ROWS, D_MODEL)),
            full((ROUTER_ROWS, 1)),
        ],
        out_specs=[
            pl.BlockSpec((tm, D_MODEL), lambda i: (i, 0)),
            pl.BlockSpec((TILE_SLOTS, PACKED), lambda i: (i, 0)),
            pl.BlockSpec((tm, LANES), lambda i: (i, 0)),
            pl.BlockSpec((None, N_EXPERTS, LANES), lambda i: (i, 0, 0)),
        ],
        out_shape=[
            jax.ShapeDtypeStruct((T, D_MODEL), F32),
            jax.ShapeDtypeStruct((nt * TILE_SLOTS, PACKED), U32),
            jax.ShapeDtypeStruct((T, LANES), F32),
            jax.ShapeDtypeStruct((nt, N_EXPERTS, LANES), F32),
        ],
        compiler_params=pltpu.CompilerParams(
            dimension_semantics=("arbitrary",), vmem_limit_bytes=VMEM_LIMIT),
        name="out_proj_router",
    )(x2, a2, c2, ag, wa, wc, n2g, wrt, brt)


def _group_copies(src_ref, hs_hbm, ys_hbm, xbuf, ybuf, gsem, ssem, block, slot, trash0, to_trash):
    def group_index(i):
        return src_ref[block * BLOCK_GROUPS + i]

    def rows_of(i):
        start = i * GROUP_ROWS
        return pl.ds(start if isinstance(i, int) else pl.multiple_of(start, GROUP_ROWS), GROUP_ROWS)

    def gather(i):
        g = jnp.maximum(group_index(i), 0)
        return pltpu.make_async_copy(hs_hbm.at[g], xbuf.at[slot, rows_of(i)], gsem.at[slot])

    def scatter(i):
        g = group_index(i)
        g = jnp.where((g < 0) | to_trash, trash0 + slot * BLOCK_GROUPS + i, g)
        return pltpu.make_async_copy(ybuf.at[slot, rows_of(i)], ys_hbm.at[g], ssem.at[slot])

    return gather, scatter


def _sorted_expert_body(be_ref, nu_ref, src_ref, tg_ref, hs_hbm, wg_ref, wu_ref, wd_ref, ys_hbm,
                        xbuf, ybuf, zero_ref, gsem, ssem, zsem, *, n_tiles, trash0):
    del be_ref
    b = pl.program_id(0)
    n_used = nu_ref[0]
    slot = b % 2

    @pl.when(b == 0)
    def _():
        zero_ref[...] = jnp.zeros_like(zero_ref)

        def zero_copy(g):
            return pltpu.make_async_copy(zero_ref, ys_hbm.at[g], zsem)

        def over_unused(fn):
            def tile(c, carry):
                def group(g, carry2):
                    fn(c * TILE_GROUPS + g)
                    return carry2
                return lax.fori_loop(tg_ref[c], TILE_GROUPS, group, carry)
            lax.fori_loop(0, n_tiles, tile, 0)

            def trash(i, carry):
                fn(trash0 + i)
                return carry
            lax.fori_loop(0, 2 * BLOCK_GROUPS, trash, 0)

        over_unused(lambda g: zero_copy(g).start())
        over_unused(lambda g: zero_copy(g).wait())

    def for_groups(fn):
        def step(i, carry):
            fn(i)
            return carry
        lax.fori_loop(0, BLOCK_GROUPS, step, 0, unroll=8)

    def copies(block, slot_, to_trash=False):
        return _group_copies(src_ref, hs_hbm, ys_hbm, xbuf, ybuf, gsem, ssem, block, slot_, trash0,
                             to_trash)

    def wait_gathers(slot_):
        gather, _ = copies(0, slot_)
        for_groups(lambda i: gather(i).wait())

    def wait_scatters(slot_):
        _, scatter = copies(0, slot_)
        for_groups(lambda i: scatter(i).wait())

    xslot = b % X_SLOTS

    def last(block):
        return jnp.minimum(block, n_used - 1)

    @pl.when((b == 0) & (n_used > 0))
    def _():
        ybuf[...] = jnp.zeros_like(ybuf)
        for block in range(X_SLOTS - 1):
            gather, _ = copies(last(block), block)
            for_groups(lambda i: gather(i).start())
        _, scatter = copies(0, 0, to_trash=True)
        for_groups(lambda i: scatter(i).start())

    @pl.when(b < n_used)
    def _():
        wait_gathers(xslot)
        ahead = b + (X_SLOTS - 1)
        gather, _ = copies(last(ahead), ahead % X_SLOTS)
        _, scatter = copies(jnp.maximum(b - 1, 0), 1 - slot, to_trash=b == 0)
        for i in range(BLOCK_GROUPS):
            gather(i).start()
            scatter(i).start()
        x = _unpack_bf16_pairs(xbuf[xslot])
        g = jnp.dot(x, wg_ref[...], preferred_element_type=F32)
        u = jnp.dot(x, wu_ref[...], preferred_element_type=F32)
        hid = (g * jax.nn.sigmoid(g) * u).astype(BF16)
        y = jnp.dot(hid, wd_ref[...], preferred_element_type=F32)
        packed = _pack_bf16_pairs(y.astype(BF16).astype(F32))
        _, landed = copies(0, slot)
        for i in range(BLOCK_GROUPS):
            landed(i).wait()
        ybuf[slot] = packed

    @pl.when((b == n_used) & (n_used > 0))
    def _():
        for ahead in range(X_SLOTS - 1):
            wait_gathers((b + ahead) % X_SLOTS)
        wait_scatters(slot)
        _, scatter = copies(b - 1, 1 - slot)
        for_groups(lambda i: scatter(i).start())
        wait_scatters(1 - slot)


def _sorted_experts(block_e, n_used, src, tile_groups, hs3, wg, wu, wd):
    n_groups = hs3.shape[0]
    n_tiles = tile_groups.shape[0]
    n_blocks = block_e.shape[0]
    blk = EXPERT_BLOCK
    grid_spec = pltpu.PrefetchScalarGridSpec(
        num_scalar_prefetch=4,
        grid=(n_blocks,),
        in_specs=[
            pl.BlockSpec(memory_space=pl.ANY),
            pl.BlockSpec((None, D_MODEL, D_EXPERT), lambda b, be, nu, src, tg: (be[b], 0, 0)),
            pl.BlockSpec((None, D_MODEL, D_EXPERT), lambda b, be, nu, src, tg: (be[b], 0, 0)),
            pl.BlockSpec((None, D_EXPERT, D_MODEL), lambda b, be, nu, src, tg: (be[b], 0, 0)),
        ],
        out_specs=pl.BlockSpec(memory_space=pl.ANY),
        scratch_shapes=[
            pltpu.VMEM((X_SLOTS, blk, PACKED), U32),
            pltpu.VMEM((2, blk, PACKED), U32),
            pltpu.VMEM((GROUP_ROWS, PACKED), U32),
            pltpu.SemaphoreType.DMA((X_SLOTS,)),
            pltpu.SemaphoreType.DMA((2,)),
            pltpu.SemaphoreType.DMA(()),
        ],
    )
    return pl.pallas_call(
        functools.partial(_sorted_expert_body, n_tiles=n_tiles, trash0=n_groups),
        grid_spec=grid_spec,
        out_shape=jax.ShapeDtypeStruct((n_groups + 2 * BLOCK_GROUPS, GROUP_ROWS, PACKED), U32),
        compiler_params=pltpu.CompilerParams(
            dimension_semantics=("arbitrary",), vmem_limit_bytes=VMEM_LIMIT),
        name="moe_experts",
    )(block_e, n_used, src, tile_groups, hs3, wg, wu, wd)


def _sorted_combine_body(x1_ref, route_ref, fg_ref, ys_ref, o_ref, *, tm):
    ys = _unpack_bf16_pairs(ys_ref[...])
    route = route_ref[...]
    slot = lax.broadcasted_iota(I32, (tm, TILE_SLOTS), 1).astype(F32)
    gate = (jnp.where(slot == route[:, COL_P0:COL_P0 + 1], route[:, COL_W0:COL_W0 + 1], 0.0)
            + jnp.where(slot == route[:, COL_P1:COL_P1 + 1], route[:, COL_W1:COL_W1 + 1], 0.0))
    moe = jnp.dot(gate.astype(BF16), ys, preferred_element_type=F32)
    o_ref[...] = _rms(x1_ref[...] + moe, fg_ref[...])


def _sorted_combine(x1, route, fg, ys2):
    T = x1.shape[0]
    tm = TOKEN_TILE
    return pl.pallas_call(
        functools.partial(_sorted_combine_body, tm=tm),
        grid=(T // tm,),
        in_specs=[
            pl.BlockSpec((tm, D_MODEL), lambda i: (i, 0)),
            pl.BlockSpec((tm, LANES), lambda i: (i, 0)),
            pl.BlockSpec((1, D_MODEL), lambda i: (0, 0)),
            pl.BlockSpec((TILE_SLOTS, PACKED), lambda i: (i, 0)),
        ],
        out_specs=pl.BlockSpec((tm, D_MODEL), lambda i: (i, 0)),
        out_shape=jax.ShapeDtypeStruct((T, D_MODEL), F32),
        compiler_params=pltpu.CompilerParams(
            dimension_semantics=("arbitrary",), vmem_limit_bytes=VMEM_LIMIT),
        name="moe_combine",
    )(x1, route, fg, ys2)


def _moe_plan(groups):
    nt = groups.shape[0]
    run_start = jnp.cumsum(groups, axis=1) - groups
    cum_tiles = jnp.cumsum(groups, axis=0)
    total = cum_tiles[-1]
    blocks = (total + BLOCK_GROUPS - 1) // BLOCK_GROUPS
    blk_end = jnp.cumsum(blocks)
    blk_start = blk_end - blocks
    n_blocks = (2 * TOKEN_TILE + (GROUP_ROWS - 1) * N_EXPERTS) * nt // EXPERT_BLOCK + N_EXPERTS + 1
    b = jnp.arange(n_blocks, dtype=I32)
    block_e = jnp.minimum(jnp.sum(b[:, None] >= blk_end[None, :], axis=1), N_EXPERTS - 1).astype(I32)
    n_used = blk_end[-1:].astype(I32)

    sg = jnp.arange(n_blocks * BLOCK_GROUPS, dtype=I32)
    sb = sg // BLOCK_GROUPS
    sg_e = jnp.broadcast_to(block_e[:, None], (n_blocks, BLOCK_GROUPS)).reshape(-1)
    onehot_e = (sg_e[:, None] == jnp.arange(N_EXPERTS, dtype=I32)[None, :]).astype(F32)

    def pick(table):
        return jnp.dot(onehot_e, table.astype(F32), precision=lax.Precision.HIGHEST).astype(I32)

    j = sg - pick(blk_start[:, None])[:, 0] * BLOCK_GROUPS
    valid = (sb < n_used[0]) & (j < pick(total[:, None])[:, 0])
    cum_e = pick(cum_tiles.T)
    tile = jnp.minimum(jnp.sum(cum_e <= j[:, None], axis=1), nt - 1)
    tile_hot = tile[:, None] == jnp.arange(nt, dtype=I32)[None, :]
    before = jnp.sum(jnp.where(tile_hot, cum_e - pick(groups.T), 0), axis=1)
    start = jnp.sum(jnp.where(tile_hot, pick(run_start.T), 0), axis=1)
    src = jnp.where(valid, tile * TILE_GROUPS + start + (j - before), -1).astype(I32)
    tile_groups = jnp.sum(groups, axis=1).astype(I32)
    return block_e, n_used, src, tile_groups


def _trunk(x, p):
    B, L, _ = x.shape
    T = B * L
    x2 = x.reshape(T, D_MODEL)
    qkv, c = _in_proj_conv(x2, L, p["norm1_g"], p["wqkv"], p["wconv"], p["conv_w"], p["conv_b"],
                           p["conv_ln_g"], p["conv_ln_b"], p["conv_out_g"])
    a = _attention(qkv.reshape(B, L, 3 * D_ATTN), p["na_bias"])
    x1, hs, route, grp = _sorted_out_proj(
        x2, a.reshape(T, D_ATTN), c.reshape(T, D_CONV), p["attn_out_g"], p["wout_a"],
        p["wout_c"], p["norm2_g"], p["w_router_t"], p["b_router_t"])
    block_e, n_used, src, tile_groups = _moe_plan(grp[:, :, 0].astype(I32))
    hs3 = hs.reshape(hs.shape[0] // GROUP_ROWS, GROUP_ROWS, PACKED)
    ys3 = _sorted_experts(block_e, n_used, src, tile_groups, hs3, p["w_gate"], p["w_up"], p["w_down"])
    ys2 = ys3.reshape(ys3.shape[0] * GROUP_ROWS, PACKED)
    out = _sorted_combine(x1, route, p["final_g"], ys2)
    return out.reshape(B, L, D_MODEL)


def kernel(x_prompt, x_sample, norm1_g, w_in, rpb, attn_out_g, conv_w, conv_b, conv_ln_g,
           conv_ln_b, conv_out_g, w_out, norm2_g, w_group, b_group, w_expert, b_expert,
           w_e_gate, w_e_up, w_e_down, final_g):
    l = 0
    w_router_t = jnp.zeros((ROUTER_ROWS, D_MODEL), F32)
    w_router_t = w_router_t.at[:N_GROUPS].set(w_group[l].T)
    w_router_t = w_router_t.at[EXPERT_ROW0:EXPERT_ROW0 + N_EXPERTS].set(
        w_expert[l].transpose(0, 2, 1).reshape(N_EXPERTS, D_MODEL))
    w_router_hi = w_router_t.astype(BF16)
    w_router_lo = (w_router_t - w_router_hi.astype(F32)).astype(BF16)
    w_router_t = jnp.concatenate([w_router_hi, w_router_lo], axis=0)
    b_router_t = jnp.zeros((ROUTER_ROWS,), F32)
    b_router_t = b_router_t.at[:N_GROUPS].set(b_group[l])
    b_router_t = b_router_t.at[EXPERT_ROW0:EXPERT_ROW0 + N_EXPERTS].set(b_expert[l].reshape(N_EXPERTS))
    b_router_t = b_router_t.reshape(ROUTER_ROWS, 1)
    p = {
        "norm1_g": norm1_g[l].reshape(1, D_MODEL),
        "wqkv": w_in[l][:, :3 * D_ATTN].astype(BF16),
        "wconv": w_in[l][:, 3 * D_ATTN:].astype(BF16),
        "na_bias": _na_bias_table(rpb[l]),
        "attn_out_g": attn_out_g[l].reshape(1, D_ATTN),
        "conv_w": jnp.pad(conv_w[l], ((0, 1), (0, 0))),
        "conv_b": conv_b[l].reshape(1, D_CONV),
        "conv_ln_g": conv_ln_g[l].reshape(1, D_CONV),
        "conv_ln_b": conv_ln_b[l].reshape(1, D_CONV),
        "conv_out_g": conv_out_g[l].reshape(1, D_CONV),
        "wout_a": w_out[l][:D_ATTN].astype(BF16),
        "wout_c": w_out[l][D_ATTN:].astype(BF16),
        "norm2_g": norm2_g[l].reshape(1, D_MODEL),
        "w_router_t": w_router_t,
        "b_router_t": b_router_t,
        "w_gate": w_e_gate[l].astype(BF16),
        "w_up": w_e_up[l].astype(BF16),
        "w_down": w_e_down[l].astype(BF16),
        "final_g": final_g.reshape(1, D_MODEL),
    }
    return (_trunk(x_prompt, p), _trunk(x_sample, p))
```

```python
import functools

import jax
import jax.numpy as jnp
from jax import lax
from jax.experimental import pallas as pl
from jax.experimental.pallas import tpu as pltpu

F32 = jnp.float32
BF16 = jnp.bfloat16
I32 = jnp.int32

D_MODEL = 1024
GRID_W = 64
D_ATTN = 512
D_CONV = 512
HEAD_DIM = 64
N_HEADS = 8
NA_ROWS = 8
NA_COLS = 16
CONV_WIDTH = 31
N_GROUPS = 4
EPG = 8
N_EXPERTS = 32
D_EXPERT = 512
EPS = 1e-6

LANES = 128
HEAD_PAIRS = N_HEADS * HEAD_DIM // LANES
NA_KEYS = NA_ROWS * GRID_W
MASKED = -1e30
TOKEN_TILE = 512
EXPERT_BLOCK = 512
CONV_TILE = 32
CONV_SEGMENT = 512
CONV_HALO = 16
VMEM_LIMIT = 56 * 1024 * 1024


def _rms(x, g):
    return x * lax.rsqrt(jnp.mean(x * x, axis=-1, keepdims=True) + EPS) * g


def _na_bias_table(rpb):
    c = jnp.arange(GRID_W)
    col_start = jnp.clip(c - NA_COLS // 2, 0, GRID_W - NA_COLS)
    cp = jnp.arange(GRID_W)
    valid = (cp[None, :] >= col_start[:, None]) & (cp[None, :] < col_start[:, None] + NA_COLS)
    col_off = cp[None, :] - c[:, None] + (NA_COLS - 1)
    sel = (col_off[None] == jnp.arange(2 * NA_COLS - 1)[:, None, None]) & valid[None]
    a = jnp.einsum("hrd,dcx->hrcx", rpb, sel.astype(F32), precision=lax.Precision.HIGHEST)
    a = jnp.where(valid[None, None], a, MASKED)
    t = jnp.stack([a[:, NA_ROWS - 1 - p:2 * NA_ROWS - 1 - p] for p in range(NA_ROWS)], axis=1)
    t = t.transpose(0, 1, 3, 2, 4)
    t = t.reshape(HEAD_PAIRS, 2, NA_ROWS, GRID_W, NA_KEYS)
    return t.transpose(0, 2, 1, 3, 4).reshape(HEAD_PAIRS, NA_ROWS, LANES, NA_KEYS).astype(F32)


def _attn_body(q_ref, k_ref, v_ref, bias_ref, o_ref, s_ref, p_ref, l_ref, *, rows, rows_per_step):
    rb = pl.program_id(2)
    first_head = lax.broadcasted_iota(I32, (GRID_W, LANES), 1) < HEAD_DIM
    zero = jnp.zeros((GRID_W, LANES), BF16)

    def window(rr):
        r = rb * rows_per_step + rr
        r_start = jnp.clip(r - NA_ROWS // 2, 0, rows - NA_ROWS)
        return r - r_start, pl.multiple_of(r_start * GRID_W, GRID_W)

    for rr in range(rows_per_step):
        p, k0 = window(rr)
        q = q_ref[rr * GRID_W:(rr + 1) * GRID_W, :] * jnp.asarray(HEAD_DIM ** -0.5, BF16)
        qbd = jnp.concatenate([jnp.where(first_head, q, zero), jnp.where(first_head, zero, q)], axis=0)
        ks = k_ref[pl.ds(k0, NA_KEYS), :]
        s = lax.dot_general(qbd, ks, (((1,), (1,)), ((), ())), preferred_element_type=F32)
        s_ref[rr] = s + bias_ref[p]
    for rr in range(rows_per_step):
        s = s_ref[rr]
        e = jnp.exp(s - jnp.max(s, axis=-1, keepdims=True))
        l_ref[rr] = jnp.broadcast_to(1.0 / jnp.sum(e, axis=-1, keepdims=True), (LANES, LANES))
        p_ref[rr] = e.astype(BF16)
    for rr in range(rows_per_step):
        _, k0 = window(rr)
        vs = v_ref[pl.ds(k0, NA_KEYS), :]
        o = jnp.dot(p_ref[rr], vs, preferred_element_type=F32) * l_ref[rr]
        out = jnp.where(first_head, o[:GRID_W], o[GRID_W:])
        o_ref[rr * GRID_W:(rr + 1) * GRID_W, :] = out.astype(BF16)


def _attention(qkv, bias):
    B, L, _ = qkv.shape
    rows = L // GRID_W
    rows_per_step = 16
    tq = rows_per_step * GRID_W
    body = functools.partial(_attn_body, rows=rows, rows_per_step=rows_per_step)
    return pl.pallas_call(
        body,
        grid=(HEAD_PAIRS, B, L // tq),
        in_specs=[
            pl.BlockSpec((None, tq, LANES), lambda hp, b, rb: (b, rb, hp)),
            pl.BlockSpec((None, L, LANES), lambda hp, b, rb: (b, 0, HEAD_PAIRS + hp)),
            pl.BlockSpec((None, L, LANES), lambda hp, b, rb: (b, 0, 2 * HEAD_PAIRS + hp)),
            pl.BlockSpec((None, NA_ROWS, LANES, NA_KEYS), lambda hp, b, rb: (hp, 0, 0, 0)),
        ],
        out_specs=pl.BlockSpec((None, tq, LANES), lambda hp, b, rb: (b, rb, hp)),
        out_shape=jax.ShapeDtypeStruct((B, L, D_ATTN), BF16),
        scratch_shapes=[
            pltpu.VMEM((rows_per_step, LANES, NA_KEYS), F32),
            pltpu.VMEM((rows_per_step, LANES, NA_KEYS), BF16),
            pltpu.VMEM((rows_per_step, LANES, LANES), F32),
        ],
        compiler_params=pltpu.CompilerParams(
            dimension_semantics=("arbitrary", "arbitrary", "arbitrary"),
            vmem_limit_bytes=VMEM_LIMIT),
        name="na_attention",
    )(qkv, qkv, qkv, bias)


def _glu(uc_rows):
    a = uc_rows[:, 0:D_CONV].astype(F32)
    gt = uc_rows[:, D_CONV:2 * D_CONV].astype(F32)
    return a * jax.nn.sigmoid(gt)


UC_RING = 4
COL_TILE = 256


def _in_proj_conv_body(x_ref, g_ref, wqkv_ref, wc_ref, w_ref, b_ref, lng_ref, lnb_ref, og_ref,
                       qkv_ref, o_ref, ring_ref, zs_ref, cbuf_ref, *, seg, n_seg):
    i = pl.program_id(0)
    halo = CONV_HALO
    n_lane_tiles = D_CONV // LANES

    @pl.when(i == 0)
    def _():
        ring_ref[...] = jnp.zeros_like(ring_ref)

    sg = jnp.maximum(i - 2, 0) % n_seg
    prev_slot = (i + UC_RING - 3) % UC_RING
    cur_slot = (i + UC_RING - 2) % UC_RING
    next_slot = (i + UC_RING - 1) % UC_RING
    write_slot = i % UC_RING

    def put_z(u0, n, z):
        for j in range(n_lane_tiles):
            zs_ref[0, j, pl.ds(u0, n), :] = z[:, j * LANES:(j + 1) * LANES]

    @pl.when(sg > 0)
    def _():
        put_z(0, halo, _glu(ring_ref[prev_slot, seg - halo:seg, :]))

    @pl.when(sg == 0)
    def _():
        put_z(0, halo, jnp.zeros((halo, D_CONV), F32))

    @pl.when(sg < n_seg - 1)
    def _():
        put_z(halo + seg, halo, _glu(ring_ref[next_slot, 0:halo, :]))

    @pl.when(sg == n_seg - 1)
    def _():
        put_z(halo + seg, halo, jnp.zeros((halo, D_CONV), F32))

    glu_rows = shift_chunk = tt = norm_rows = CONV_TILE
    first = halo - CONV_WIDTH // 2

    def glu(u0):
        put_z(halo + u0, glu_rows, _glu(ring_ref[cur_slot, u0:u0 + glu_rows, :]))

    def shift_rows(j, u0, n):
        x = zs_ref[0, j, u0:u0 + n + 8, :]
        for s in range(1, 8):
            zs_ref[s, j, u0:u0 + n, :] = x[s:s + n]

    def taps(j, t0):
        acc = jnp.broadcast_to(b_ref[j], (tt, LANES))
        for k in range(CONV_WIDTH):
            off = k + first
            lo = t0 + 8 * (off // 8)
            acc = acc + w_ref[j, k:k + 1, :] * zs_ref[off % 8, j, lo:lo + tt, :]
        cbuf_ref[j, t0:t0 + tt, :] = acc

    def norm(t0):
        y = jnp.concatenate([cbuf_ref[j, t0:t0 + norm_rows, :] for j in range(n_lane_tiles)], axis=1)
        mu = jnp.mean(y, axis=-1, keepdims=True)
        yc = y - mu
        var = jnp.mean(yc * yc, axis=-1, keepdims=True)
        yn = yc * lax.rsqrt(var + EPS) * lng_ref[...] + lnb_ref[...]
        sw = yn * jax.nn.sigmoid(yn)
        o_ref[t0:t0 + norm_rows, :] = _rms(sw, og_ref[...]).astype(BF16)

    pieces = [functools.partial(glu, u0) for u0 in range(0, seg, glu_rows)]
    for j in range(n_lane_tiles):
        pieces += [functools.partial(shift_rows, j, u0, shift_chunk) for u0 in range(0, seg, shift_chunk)]
        pieces.append(functools.partial(shift_rows, j, seg, 2 * halo - 8))
    pieces += [functools.partial(taps, j, t0) for j in range(n_lane_tiles) for t0 in range(0, seg, tt)]
    pieces += [functools.partial(norm, t0) for t0 in range(0, seg, norm_rows)]

    h = _rms(x_ref[...], g_ref[...]).astype(BF16)
    qkv_chunks = 3 * D_ATTN // COL_TILE
    uc_chunks = 2 * D_CONV // COL_TILE
    n_chunks = qkv_chunks + uc_chunks
    for c in range(n_chunks):
        if c < qkv_chunks:
            cols = slice(c * COL_TILE, (c + 1) * COL_TILE)
            qkv_ref[:, cols] = jnp.dot(h, wqkv_ref[:, cols], preferred_element_type=F32).astype(BF16)
        else:
            cols = slice((c - qkv_chunks) * COL_TILE, (c - qkv_chunks + 1) * COL_TILE)
            ring_ref[write_slot, :, cols] = jnp.dot(
                h, wc_ref[:, cols], preferred_element_type=F32).astype(BF16)
        for piece in pieces[len(pieces) * c // n_chunks:len(pieces) * (c + 1) // n_chunks]:
            piece()


def _in_proj_conv(x2, L, g, wqkv, wc, w, b, lng, lnb, og):
    T = x2.shape[0]
    seg = CONV_SEGMENT
    assert seg == TOKEN_TILE and L % seg == 0
    nt = T // seg
    n_lane_tiles = D_CONV // LANES
    w = w.reshape(CONV_WIDTH + 1, n_lane_tiles, LANES).transpose(1, 0, 2)
    b = b.reshape(n_lane_tiles, 1, LANES)
    vec = pl.BlockSpec((1, D_CONV), lambda i: (0, 0))
    tile = lambda i: (jnp.minimum(i, nt - 1), 0)
    return pl.pallas_call(
        functools.partial(_in_proj_conv_body, seg=seg, n_seg=L // seg),
        grid=(nt + 2,),
        in_specs=[
            pl.BlockSpec((seg, D_MODEL), tile),
            pl.BlockSpec((1, D_MODEL), lambda i: (0, 0)),
            pl.BlockSpec((D_MODEL, 3 * D_ATTN), lambda i: (0, 0)),
            pl.BlockSpec((D_MODEL, 2 * D_CONV), lambda i: (0, 0)),
            pl.BlockSpec((n_lane_tiles, CONV_WIDTH + 1, LANES), lambda i: (0, 0, 0)),
            pl.BlockSpec((n_lane_tiles, 1, LANES), lambda i: (0, 0, 0)),
            vec, vec, vec,
        ],
        out_specs=[
            pl.BlockSpec((seg, 3 * D_ATTN), tile),
            pl.BlockSpec((seg, D_CONV), lambda i: (jnp.maximum(i - 2, 0), 0)),
        ],
        out_shape=[
            jax.ShapeDtypeStruct((T, 3 * D_ATTN), BF16),
            jax.ShapeDtypeStruct((T, D_CONV), BF16),
        ],
        scratch_shapes=[
            pltpu.VMEM((UC_RING, seg, 2 * D_CONV), BF16),
            pltpu.VMEM((8, n_lane_tiles, seg + 2 * CONV_HALO, LANES), F32),
            pltpu.VMEM((n_lane_tiles, seg, LANES), F32),
        ],
        compiler_params=pltpu.CompilerParams(
            dimension_semantics=("arbitrary",), vmem_limit_bytes=VMEM_LIMIT),
        name="in_proj_conv",
    )(x2, g, wqkv, wc, w, b, lng, lnb, og)


GROUP_ROWS = 8
TILE_GROUPS = (2 * TOKEN_TILE + (GROUP_ROWS - 1) * N_EXPERTS + GROUP_ROWS - 1) // GROUP_ROWS
TILE_GROUPS = (TILE_GROUPS + 15) // 16 * 16
TILE_SLOTS = TILE_GROUPS * GROUP_ROWS
BLOCK_GROUPS = EXPERT_BLOCK // GROUP_ROWS
X_SLOTS = 3
ROUTER_ROWS = 128
EXPERT_ROW0 = 8
COL_P0, COL_P1, COL_W0, COL_W1 = range(4)


U32 = jnp.uint32
PACKED = D_MODEL // 2


def _pack_bf16_pairs(x):
    bits = pltpu.bitcast(x, U32)
    return bits[:, PACKED:] | (bits[:, :PACKED] >> 16)


def _unpack_bf16_pairs(words):
    lo = pltpu.bitcast(words << 16, F32)
    hi = pltpu.bitcast(words & jnp.uint32(0xFFFF0000), F32)
    return jnp.concatenate([lo, hi], axis=1).astype(BF16)


def _sorted_out_proj_body(x_ref, a_ref, c_ref, ag_ref, wa_ref, wc_ref, n2g_ref, wrt_ref, brt_ref,
                          x1_ref, hs_ref, route_ref, grp_ref, *, tm):
    an = _rms(a_ref[...].astype(F32), ag_ref[...]).astype(BF16)
    y = jnp.dot(an, wa_ref[...], preferred_element_type=F32)
    y = y + jnp.dot(c_ref[...], wc_ref[...], preferred_element_type=F32)
    x1 = x_ref[...] + y
    x1_ref[...] = x1
    h2 = _rms(x1, n2g_ref[...])

    h_hi = h2.astype(BF16)
    h_lo = (h2 - h_hi.astype(F32)).astype(BF16)
    nt = (((1,), (1,)), ((), ()))
    split = (lax.dot_general(wrt_ref[...], h_hi, nt, preferred_element_type=F32)
             + lax.dot_general(wrt_ref[...], h_lo, nt, preferred_element_type=F32))
    logits = split[:ROUTER_ROWS] + split[ROUTER_ROWS:] + brt_ref[...]

    gtop = logits[0:1]
    gsel = jnp.zeros((1, tm), F32)
    for g in range(1, N_GROUPS):
        cand = logits[g:g + 1]
        better = cand > gtop
        gsel = jnp.where(better, float(g), gsel)
        gtop = jnp.where(better, cand, gtop)
    denom = jnp.zeros((1, tm), F32)
    for g in range(N_GROUPS):
        denom = denom + jnp.exp(logits[g:g + 1] - gtop)
    p_g = 1.0 / denom

    el = logits[EXPERT_ROW0:EXPERT_ROW0 + EPG]
    for g in range(1, N_GROUPS):
        el = jnp.where(gsel == float(g), logits[EXPERT_ROW0 + g * EPG:EXPERT_ROW0 + (g + 1) * EPG], el)
    ninf = jnp.full((1, tm), -jnp.inf, F32)
    v0, v1 = ninf, ninf
    i0 = jnp.zeros((1, tm), F32)
    i1 = jnp.zeros((1, tm), F32)
    for j in range(EPG):
        cand = el[j:j + 1]
        gt0 = cand > v0
        gt1 = cand > v1
        v1 = jnp.where(gt0, v0, jnp.where(gt1, cand, v1))
        i1 = jnp.where(gt0, i0, jnp.where(gt1, float(j), i1))
        v0 = jnp.where(gt0, cand, v0)
        i0 = jnp.where(gt0, float(j), i0)
    t = jnp.exp(v1 - v0)
    w0 = p_g / (1.0 + t)
    w1 = p_g * t / (1.0 + t)
    e0 = gsel * EPG + i0
    e1 = gsel * EPG + i1

    expert = lax.broadcasted_iota(I32, (N_EXPERTS, tm), 0).astype(F32)
    hit0 = expert == e0
    hit1 = expert == e1
    onehot = jnp.where(hit0 | hit1, 1.0, 0.0).astype(BF16)
    ri = lax.broadcasted_iota(I32, (tm, tm), 0)
    ci = lax.broadcasted_iota(I32, (tm, tm), 1)
    upper = jnp.where(ri <= ci, 1.0, 0.0).astype(BF16)
    cum = jnp.dot(onehot, upper, preferred_element_type=F32)
    count = jnp.dot(onehot, jnp.ones((tm, LANES), BF16), preferred_element_type=F32)
    groups = jnp.floor((count + (GROUP_ROWS - 1)) * (1.0 / GROUP_ROWS))
    er = lax.broadcasted_iota(I32, (N_EXPERTS, N_EXPERTS), 0)
    ec = lax.broadcasted_iota(I32, (N_EXPERTS, N_EXPERTS), 1)
    below = jnp.where(ec < er, 1.0, 0.0).astype(BF16)
    run_start = jnp.dot(below, groups.astype(BF16), preferred_element_type=F32) * GROUP_ROWS
    run_start = jnp.concatenate([run_start] * (tm // LANES), axis=1)
    pos = run_start + cum - 1.0
    p0 = jnp.sum(jnp.where(hit0, pos, 0.0), axis=0, keepdims=True)
    p1 = jnp.sum(jnp.where(hit1, pos, 0.0), axis=0, keepdims=True)
    grp_ref[...] = groups

    slot = lax.broadcasted_iota(I32, (TILE_SLOTS, tm), 0).astype(F32)
    perm = jnp.where((slot == p0) | (slot == p1), 1.0, 0.0).astype(BF16)
    hs_ref[...] = _pack_bf16_pairs(jnp.dot(perm, h_hi, preferred_element_type=F32))

    rows = jnp.concatenate([p0, p1, w0, w1, jnp.zeros((ROUTER_ROWS - 4, tm), F32)], axis=0)
    route_ref[...] = rows.T


def _sorted_out_proj(x2, a2, c2, ag, wa, wc, n2g, wrt, brt):
    T = x2.shape[0]
    tm = TOKEN_TILE
    nt = T // tm
    full = lambda shape: pl.BlockSpec(shape, lambda i: (0,) * len(shape))
    return pl.pallas_call(
        functools.partial(_sorted_out_proj_body, tm=tm),
        grid=(nt,),
        in_specs=[
            pl.BlockSpec((tm, D_MODEL), lambda i: (i, 0)),
            pl.BlockSpec((tm, D_ATTN), lambda i: (i, 0)),
            pl.BlockSpec((tm, D_CONV), lambda i: (i, 0)),
            full((1, D_ATTN)),
            full((D_ATTN, D_MODEL)),
            full((D_CONV, D_MODEL)),
            full((1, D_MODEL)),
            full((2 * ROUTER_ROWS, D_MODEL)),
            full((ROUTER_ROWS, 1)),
        ],
        out_specs=[
            pl.BlockSpec((tm, D_MODEL), lambda i: (i, 0)),
            pl.BlockSpec((TILE_SLOTS, PACKED), lambda i: (i, 0)),
            pl.BlockSpec((tm, LANES), lambda i: (i, 0)),
            pl.BlockSpec((None, N_EXPERTS, LANES), lambda i: (i, 0, 0)),
        ],
        out_shape=[
            jax.ShapeDtypeStruct((T, D_MODEL), F32),
            jax.ShapeDtypeStruct((nt * TILE_SLOTS, PACKED), U32),
            jax.ShapeDtypeStruct((T, LANES), F32),
            jax.ShapeDtypeStruct((nt, N_EXPERTS, LANES), F32),
        ],
        compiler_params=pltpu.CompilerParams(
            dimension_semantics=("arbitrary",), vmem_limit_bytes=VMEM_LIMIT),
        name="out_proj_router",
    )(x2, a2, c2, ag, wa, wc, n2g, wrt, brt)


def _group_copies(src_ref, hs_hbm, ys_hbm, xbuf, ybuf, gsem, ssem, block, slot, trash0, to_trash):
    def group_index(i):
        return src_ref[block * BLOCK_GROUPS + i]

    def rows_of(i):
        start = i * GROUP_ROWS
        return pl.ds(start if isinstance(i, int) else pl.multiple_of(start, GROUP_ROWS), GROUP_ROWS)

    def gather(i):
        g = jnp.maximum(group_index(i), 0)
        return pltpu.make_async_copy(hs_hbm.at[g], xbuf.at[slot, rows_of(i)], gsem.at[slot])

    def scatter(i):
        g = group_index(i)
        g = jnp.where((g < 0) | to_trash, trash0 + slot * BLOCK_GROUPS + i, g)
        return pltpu.make_async_copy(ybuf.at[slot, rows_of(i)], ys_hbm.at[g], ssem.at[slot])

    return gather, scatter


def _sorted_expert_body(be_ref, nu_ref, src_ref, tg_ref, hs_hbm, wg_ref, wu_ref, wd_ref, ys_hbm,
                        xbuf, ybuf, zero_ref, gsem, ssem, zsem, *, n_tiles, trash0):
    del be_ref
    b = pl.program_id(0)
    n_used = nu_ref[0]
    slot = b % 2

    @pl.when(b == 0)
    def _():
        zero_ref[...] = jnp.zeros_like(zero_ref)

        def zero_copy(g):
            return pltpu.make_async_copy(zero_ref, ys_hbm.at[g], zsem)

        def over_unused(fn):
            def tile(c, carry):
                def group(g, carry2):
                    fn(c * TILE_GROUPS + g)
                    return carry2
                return lax.fori_loop(tg_ref[c], TILE_GROUPS, group, carry)
            lax.fori_loop(0, n_tiles, tile, 0)

            def trash(i, carry):
                fn(trash0 + i)
                return carry
            lax.fori_loop(0, 2 * BLOCK_GROUPS, trash, 0)

        over_unused(lambda g: zero_copy(g).start())
        over_unused(lambda g: zero_copy(g).wait())

    def for_groups(fn):
        def step(i, carry):
            fn(i)
            return carry
        lax.fori_loop(0, BLOCK_GROUPS, step, 0, unroll=8)

    def copies(block, slot_, to_trash=False):
        return _group_copies(src_ref, hs_hbm, ys_hbm, xbuf, ybuf, gsem, ssem, block, slot_, trash0,
                             to_trash)

    def wait_gathers(slot_):
        gather, _ = copies(0, slot_)
        for_groups(lambda i: gather(i).wait())

    def wait_scatters(slot_):
        _, scatter = copies(0, slot_)
        for_groups(lambda i: scatter(i).wait())

    xslot = b % X_SLOTS

    def last(block):
        return jnp.minimum(block, n_used - 1)

    @pl.when((b == 0) & (n_used > 0))
    def _():
        ybuf[...] = jnp.zeros_like(ybuf)
        for block in range(X_SLOTS - 1):
            gather, _ = copies(last(block), block)
            for_groups(lambda i: gather(i).start())
        _, scatter = copies(0, 0, to_trash=True)
        for_groups(lambda i: scatter(i).start())

    @pl.when(b < n_used)
    def _():
        wait_gathers(xslot)
        ahead = b + (X_SLOTS - 1)
        gather, _ = copies(last(ahead), ahead % X_SLOTS)
        _, scatter = copies(jnp.maximum(b - 1, 0), 1 - slot, to_trash=b == 0)
        for i in range(BLOCK_GROUPS):
            gather(i).start()
            scatter(i).start()
        x = _unpack_bf16_pairs(xbuf[xslot])
        g = jnp.dot(x, wg_ref[...], preferred_element_type=F32)
        u = jnp.dot(x, wu_ref[...], preferred_element_type=F32)
        hid = (g * jax.nn.sigmoid(g) * u).astype(BF16)
        y = jnp.dot(hid, wd_ref[...], preferred_element_type=F32)
        packed = _pack_bf16_pairs(y.astype(BF16).astype(F32))
        _, landed = copies(0, slot)
        for i in range(BLOCK_GROUPS):
            landed(i).wait()
        ybuf[slot] = packed

    @pl.when((b == n_used) & (n_used > 0))
    def _():
        for ahead in range(X_SLOTS - 1):
            wait_gathers((b + ahead) % X_SLOTS)
        wait_scatters(slot)
        _, scatter = copies(b - 1, 1 - slot)
        for_groups(lambda i: scatter(i).start())
        wait_scatters(1 - slot)


def _sorted_experts(block_e, n_used, src, tile_groups, hs3, wg, wu, wd):
    n_groups = hs3.shape[0]
    n_tiles = tile_groups.shape[0]
    n_blocks = block_e.shape[0]
    blk = EXPERT_BLOCK
    grid_spec = pltpu.PrefetchScalarGridSpec(
        num_scalar_prefetch=4,
        grid=(n_blocks,),
        in_specs=[
            pl.BlockSpec(memory_space=pl.ANY),
            pl.BlockSpec((None, D_MODEL, D_EXPERT), lambda b, be, nu, src, tg: (be[b], 0, 0)),
            pl.BlockSpec((None, D_MODEL, D_EXPERT), lambda b, be, nu, src, tg: (be[b], 0, 0)),
            pl.BlockSpec((None, D_EXPERT, D_MODEL), lambda b, be, nu, src, tg: (be[b], 0, 0)),
        ],
        out_specs=pl.BlockSpec(memory_space=pl.ANY),
        scratch_shapes=[
            pltpu.VMEM((X_SLOTS, blk, PACKED), U32),
            pltpu.VMEM((2, blk, PACKED), U32),
            pltpu.VMEM((GROUP_ROWS, PACKED), U32),
            pltpu.SemaphoreType.DMA((X_SLOTS,)),
            pltpu.SemaphoreType.DMA((2,)),
            pltpu.SemaphoreType.DMA(()),
        ],
    )
    return pl.pallas_call(
        functools.partial(_sorted_expert_body, n_tiles=n_tiles, trash0=n_groups),
        grid_spec=grid_spec,
        out_shape=jax.ShapeDtypeStruct((n_groups + 2 * BLOCK_GROUPS, GROUP_ROWS, PACKED), U32),
        compiler_params=pltpu.CompilerParams(
            dimension_semantics=("arbitrary",), vmem_limit_bytes=VMEM_LIMIT),
        name="moe_experts",
    )(block_e, n_used, src, tile_groups, hs3, wg, wu, wd)


def _sorted_combine_body(x1_ref, route_ref, fg_ref, ys_ref, o_ref, *, tm):
    ys = _unpack_bf16_pairs(ys_ref[...])
    route = route_ref[...]
    slot = lax.broadcasted_iota(I32, (tm, TILE_SLOTS), 1).astype(F32)
    gate = (jnp.where(slot == route[:, COL_P0:COL_P0 + 1], route[:, COL_W0:COL_W0 + 1], 0.0)
            + jnp.where(slot == route[:, COL_P1:COL_P1 + 1], route[:, COL_W1:COL_W1 + 1], 0.0))
    moe = jnp.dot(gate.astype(BF16), ys, preferred_element_type=F32)
    o_ref[...] = _rms(x1_ref[...] + moe, fg_ref[...])


def _sorted_combine(x1, route, fg, ys2):
    T = x1.shape[0]
    tm = TOKEN_TILE
    return pl.pallas_call(
        functools.partial(_sorted_combine_body, tm=tm),
        grid=(T // tm,),
        in_specs=[
            pl.BlockSpec((tm, D_MODEL), lambda i: (i, 0)),
            pl.BlockSpec((tm, LANES), lambda i: (i, 0)),
            pl.BlockSpec((1, D_MODEL), lambda i: (0, 0)),
            pl.BlockSpec((TILE_SLOTS, PACKED), lambda i: (i, 0)),
        ],
        out_specs=pl.BlockSpec((tm, D_MODEL), lambda i: (i, 0)),
        out_shape=jax.ShapeDtypeStruct((T, D_MODEL), F32),
        compiler_params=pltpu.CompilerParams(
            dimension_semantics=("arbitrary",), vmem_limit_bytes=VMEM_LIMIT),
        name="moe_combine",
    )(x1, route, fg, ys2)


def _moe_plan(groups):
    nt = groups.shape[0]
    run_start = jnp.cumsum(groups, axis=1) - groups
    cum_tiles = jnp.cumsum(groups, axis=0)
    total = cum_tiles[-1]
    blocks = (total + BLOCK_GROUPS - 1) // BLOCK_GROUPS
    blk_end = jnp.cumsum(blocks)
    blk_start = blk_end - blocks
    n_blocks = (2 * TOKEN_TILE + (GROUP_ROWS - 1) * N_EXPERTS) * nt // EXPERT_BLOCK + N_EXPERTS + 1
    b = jnp.arange(n_blocks, dtype=I32)
    block_e = jnp.minimum(jnp.sum(b[:, None] >= blk_end[None, :], axis=1), N_EXPERTS - 1).astype(I32)
    n_used = blk_end[-1:].astype(I32)

    sg = jnp.arange(n_blocks * BLOCK_GROUPS, dtype=I32)
    sb = sg // BLOCK_GROUPS
    sg_e = jnp.broadcast_to(block_e[:, None], (n_blocks, BLOCK_GROUPS)).reshape(-1)
    onehot_e = (sg_e[:, None] == jnp.arange(N_EXPERTS, dtype=I32)[None, :]).astype(F32)

    def pick(table):
        return jnp.dot(onehot_e, table.astype(F32), precision=lax.Precision.HIGHEST).astype(I32)

    j = sg - pick(blk_start[:, None])[:, 0] * BLOCK_GROUPS
    valid = (sb < n_used[0]) & (j < pick(total[:, None])[:, 0])
    cum_e = pick(cum_tiles.T)
    tile = jnp.minimum(jnp.sum(cum_e <= j[:, None], axis=1), nt - 1)
    tile_hot = tile[:, None] == jnp.arange(nt, dtype=I32)[None, :]
    before = jnp.sum(jnp.where(tile_hot, cum_e - pick(groups.T), 0), axis=1)
    start = jnp.sum(jnp.where(tile_hot, pick(run_start.T), 0), axis=1)
    src = jnp.where(valid, tile * TILE_GROUPS + start + (j - before), -1).astype(I32)
    tile_groups = jnp.sum(groups, axis=1).astype(I32)
    return block_e, n_used, src, tile_groups


def _trunk(x, p):
    B, L, _ = x.shape
    T = B * L
    x2 = x.reshape(T, D_MODEL)
    qkv, c = _in_proj_conv(x2, L, p["norm1_g"], p["wqkv"], p["wconv"], p["conv_w"], p["conv_b"],
                           p["conv_ln_g"], p["conv_ln_b"], p["conv_out_g"])
    a = _attention(qkv.reshape(B, L, 3 * D_ATTN), p["na_bias"])
    x1, hs, route, grp = _sorted_out_proj(
        x2, a.reshape(T, D_ATTN), c.reshape(T, D_CONV), p["attn_out_g"], p["wout_a"],
        p["wout_c"], p["norm2_g"], p["w_router_t"], p["b_router_t"])
    block_e, n_used, src, tile_groups = _moe_plan(grp[:, :, 0].astype(I32))
    hs3 = hs.reshape(hs.shape[0] // GROUP_ROWS, GROUP_ROWS, PACKED)
    ys3 = _sorted_experts(block_e, n_used, src, tile_groups, hs3, p["w_gate"], p["w_up"], p["w_down"])
    ys2 = ys3.reshape(ys3.shape[0] * GROUP_ROWS, PACKED)
    out = _sorted_combine(x1, route, p["final_g"], ys2)
    return out.reshape(B, L, D_MODEL)


def kernel(x_prompt, x_sample, norm1_g, w_in, rpb, attn_out_g, conv_w, conv_b, conv_ln_g,
           conv_ln_b, conv_out_g, w_out, norm2_g, w_group, b_group, w_expert, b_expert,
           w_e_gate, w_e_up, w_e_down, final_g):
    l = 0
    w_router_t = jnp.zeros((ROUTER_ROWS, D_MODEL), F32)
    w_router_t = w_router_t.at[:N_GROUPS].set(w_group[l].T)
    w_router_t = w_router_t.at[EXPERT_ROW0:EXPERT_ROW0 + N_EXPERTS].set(
        w_expert[l].transpose(0, 2, 1).reshape(N_EXPERTS, D_MODEL))
    w_router_hi = w_router_t.astype(BF16)
    w_router_lo = (w_router_t - w_router_hi.astype(F32)).astype(BF16)
    w_router_t = jnp.concatenate([w_router_hi, w_router_lo], axis=0)
    b_router_t = jnp.zeros((ROUTER_ROWS,), F32)
    b_router_t = b_router_t.at[:N_GROUPS].set(b_group[l])
    b_router_t = b_router_t.at[EXPERT_ROW0:EXPERT_ROW0 + N_EXPERTS].set(b_expert[l].reshape(N_EXPERTS))
    b_router_t = b_router_t.reshape(ROUTER_ROWS, 1)
    p = {
        "norm1_g": norm1_g[l].reshape(1, D_MODEL),
        "wqkv": w_in[l][:, :3 * D_ATTN].astype(BF16),
        "wconv": w_in[l][:, 3 * D_ATTN:].astype(BF16),
        "na_bias": _na_bias_table(rpb[l]),
        "attn_out_g": attn_out_g[l].reshape(1, D_ATTN),
        "conv_w": jnp.pad(conv_w[l], ((0, 1), (0, 0))),
        "conv_b": conv_b[l].reshape(1, D_CONV),
        "conv_ln_g": conv_ln_g[l].reshape(1, D_CONV),
        "conv_ln_b": conv_ln_b[l].reshape(1, D_CONV),
        "conv_out_g": conv_out_g[l].reshape(1, D_CONV),
        "wout_a": w_out[l][:D_ATTN].astype(BF16),
        "wout_c": w_out[l][D_ATTN:].astype(BF16),
        "norm2_g": norm2_g[l].reshape(1, D_MODEL),
        "w_router_t": w_router_t,
        "b_router_t": b_router_t,
        "w_gate": w_e_gate[l].astype(BF16),
        "w_up": w_e_up[l].astype(BF16),
        "w_down": w_e_down[l].astype(BF16),
        "final_g": final_g.reshape(1, D_MODEL),
    }
    return (_trunk(x_prompt, p), _trunk(x_sample, p))
```

```python
import functools

import jax
import jax.numpy as jnp
from jax import lax
from jax.experimental import pallas as pl
from jax.experimental.pallas import tpu as pltpu

F32 = jnp.float32
BF16 = jnp.bfloat16
I32 = jnp.int32

D_MODEL = 1024
GRID_W = 64
D_ATTN = 512
D_CONV = 512
HEAD_DIM = 64
N_HEADS = 8
NA_ROWS = 8
NA_COLS = 16
CONV_WIDTH = 31
N_GROUPS = 4
EPG = 8
N_EXPERTS = 32
D_EXPERT = 512
EPS = 1e-6

LANES = 128
HEAD_PAIRS = N_HEADS * HEAD_DIM // LANES
NA_KEYS = NA_ROWS * GRID_W
MASKED = -1e30
TOKEN_TILE = 512
EXPERT_BLOCK = 512
CONV_TILE = 32
CONV_SEGMENT = 512
CONV_HALO = 16
VMEM_LIMIT = 56 * 1024 * 1024


def _rms(x, g):
    return x * lax.rsqrt(jnp.mean(x * x, axis=-1, keepdims=True) + EPS) * g


def _na_bias_table(rpb):
    c = jnp.arange(GRID_W)
    col_start = jnp.clip(c - NA_COLS // 2, 0, GRID_W - NA_COLS)
    cp = jnp.arange(GRID_W)
    valid = (cp[None, :] >= col_start[:, None]) & (cp[None, :] < col_start[:, None] + NA_COLS)
    col_off = cp[None, :] - c[:, None] + (NA_COLS - 1)
    sel = (col_off[None] == jnp.arange(2 * NA_COLS - 1)[:, None, None]) & valid[None]
    a = jnp.einsum("hrd,dcx->hrcx", rpb, sel.astype(F32), precision=lax.Precision.HIGHEST)
    a = jnp.where(valid[None, None], a, MASKED)
    t = jnp.stack([a[:, NA_ROWS - 1 - p:2 * NA_ROWS - 1 - p] for p in range(NA_ROWS)], axis=1)
    t = t.transpose(0, 1, 3, 2, 4)
    t = t.reshape(HEAD_PAIRS, 2, NA_ROWS, GRID_W, NA_KEYS)
    return t.transpose(0, 2, 1, 3, 4).reshape(HEAD_PAIRS, NA_ROWS, LANES, NA_KEYS).astype(F32)


def _attn_body(q_ref, k_ref, v_ref, bias_ref, o_ref, s_ref, p_ref, l_ref, *, rows, rows_per_step):
    rb = pl.program_id(2)
    first_head = lax.broadcasted_iota(I32, (GRID_W, LANES), 1) < HEAD_DIM
    zero = jnp.zeros((GRID_W, LANES), BF16)

    def window(rr):
        r = rb * rows_per_step + rr
        r_start = jnp.clip(r - NA_ROWS // 2, 0, rows - NA_ROWS)
        return r - r_start, pl.multiple_of(r_start * GRID_W, GRID_W)

    for rr in range(rows_per_step):
        p, k0 = window(rr)
        q = q_ref[rr * GRID_W:(rr + 1) * GRID_W, :] * jnp.asarray(HEAD_DIM ** -0.5, BF16)
        qbd = jnp.concatenate([jnp.where(first_head, q, zero), jnp.where(first_head, zero, q)], axis=0)
        ks = k_ref[pl.ds(k0, NA_KEYS), :]
        s = lax.dot_general(qbd, ks, (((1,), (1,)), ((), ())), preferred_element_type=F32)
        s_ref[rr] = s + bias_ref[p]
    for rr in range(rows_per_step):
        s = s_ref[rr]
        e = jnp.exp(s - jnp.max(s, axis=-1, keepdims=True))
        l_ref[rr] = jnp.broadcast_to(1.0 / jnp.sum(e, axis=-1, keepdims=True), (LANES, LANES))
        p_ref[rr] = e.astype(BF16)
    for rr in range(rows_per_step):
        _, k0 = window(rr)
        vs = v_ref[pl.ds(k0, NA_KEYS), :]
        o = jnp.dot(p_ref[rr], vs, preferred_element_type=F32) * l_ref[rr]
        out = jnp.where(first_head, o[:GRID_W], o[GRID_W:])
        o_ref[rr * GRID_W:(rr + 1) * GRID_W, :] = out.astype(BF16)


def _attention(qkv, bias):
    B, L, _ = qkv.shape
    rows = L // GRID_W
    rows_per_step = 32
    tq = rows_per_step * GRID_W
    body = functools.partial(_attn_body, rows=rows, rows_per_step=rows_per_step)
    return pl.pallas_call(
        body,
        grid=(HEAD_PAIRS, B, L // tq),
        in_specs=[
            pl.BlockSpec((None, tq, LANES), lambda hp, b, rb: (b, rb, hp)),
            pl.BlockSpec((None, L, LANES), lambda hp, b, rb: (b, 0, HEAD_PAIRS + hp)),
            pl.BlockSpec((None, L, LANES), lambda hp, b, rb: (b, 0, 2 * HEAD_PAIRS + hp)),
            pl.BlockSpec((None, NA_ROWS, LANES, NA_KEYS), lambda hp, b, rb: (hp, 0, 0, 0)),
        ],
        out_specs=pl.BlockSpec((None, tq, LANES), lambda hp, b, rb: (b, rb, hp)),
        out_shape=jax.ShapeDtypeStruct((B, L, D_ATTN), BF16),
        scratch_shapes=[
            pltpu.VMEM((rows_per_step, LANES, NA_KEYS), F32),
            pltpu.VMEM((rows_per_step, LANES, NA_KEYS), BF16),
            pltpu.VMEM((rows_per_step, LANES, LANES), F32),
        ],
        compiler_params=pltpu.CompilerParams(
            dimension_semantics=("arbitrary", "arbitrary", "arbitrary"),
            vmem_limit_bytes=VMEM_LIMIT),
        name="na_attention",
    )(qkv, qkv, qkv, bias)


def _glu(uc_rows):
    a = uc_rows[:, 0:D_CONV].astype(F32)
    gt = uc_rows[:, D_CONV:2 * D_CONV].astype(F32)
    return a * jax.nn.sigmoid(gt)


UC_RING = 4
COL_TILE = 256


def _in_proj_conv_body(x_ref, g_ref, wqkv_ref, wc_ref, w_ref, b_ref, lng_ref, lnb_ref, og_ref,
                       qkv_ref, o_ref, ring_ref, zs_ref, cbuf_ref, *, seg, n_seg):
    i = pl.program_id(0)
    halo = CONV_HALO
    n_lane_tiles = D_CONV // LANES

    @pl.when(i == 0)
    def _():
        ring_ref[...] = jnp.zeros_like(ring_ref)

    sg = jnp.maximum(i - 2, 0) % n_seg
    prev_slot = (i + UC_RING - 3) % UC_RING
    cur_slot = (i + UC_RING - 2) % UC_RING
    next_slot = (i + UC_RING - 1) % UC_RING
    write_slot = i % UC_RING

    def put_z(u0, n, z):
        for j in range(n_lane_tiles):
            zs_ref[0, j, pl.ds(u0, n), :] = z[:, j * LANES:(j + 1) * LANES]

    @pl.when(sg > 0)
    def _():
        put_z(0, halo, _glu(ring_ref[prev_slot, seg - halo:seg, :]))

    @pl.when(sg == 0)
    def _():
        put_z(0, halo, jnp.zeros((halo, D_CONV), F32))

    @pl.when(sg < n_seg - 1)
    def _():
        put_z(halo + seg, halo, _glu(ring_ref[next_slot, 0:halo, :]))

    @pl.when(sg == n_seg - 1)
    def _():
        put_z(halo + seg, halo, jnp.zeros((halo, D_CONV), F32))

    glu_rows = shift_chunk = tt = norm_rows = CONV_TILE
    first = halo - CONV_WIDTH // 2

    def glu(u0):
        put_z(halo + u0, glu_rows, _glu(ring_ref[cur_slot, u0:u0 + glu_rows, :]))

    def shift_rows(j, u0, n):
        x = zs_ref[0, j, u0:u0 + n + 8, :]
        for s in range(1, 8):
            zs_ref[s, j, u0:u0 + n, :] = x[s:s + n]

    def taps(j, t0):
        acc = jnp.broadcast_to(b_ref[j], (tt, LANES))
        for k in range(CONV_WIDTH):
            off = k + first
            lo = t0 + 8 * (off // 8)
            acc = acc + w_ref[j, k:k + 1, :] * zs_ref[off % 8, j, lo:lo + tt, :]
        cbuf_ref[j, t0:t0 + tt, :] = acc

    def norm(t0):
        y = jnp.concatenate([cbuf_ref[j, t0:t0 + norm_rows, :] for j in range(n_lane_tiles)], axis=1)
        mu = jnp.mean(y, axis=-1, keepdims=True)
        yc = y - mu
        var = jnp.mean(yc * yc, axis=-1, keepdims=True)
        yn = yc * lax.rsqrt(var + EPS) * lng_ref[...] + lnb_ref[...]
        sw = yn * jax.nn.sigmoid(yn)
        o_ref[t0:t0 + norm_rows, :] = _rms(sw, og_ref[...]).astype(BF16)

    pieces = [functools.partial(glu, u0) for u0 in range(0, seg, glu_rows)]
    for j in range(n_lane_tiles):
        pieces += [functools.partial(shift_rows, j, u0, shift_chunk) for u0 in range(0, seg, shift_chunk)]
        pieces.append(functools.partial(shift_rows, j, seg, 2 * halo - 8))
    pieces += [functools.partial(taps, j, t0) for j in range(n_lane_tiles) for t0 in range(0, seg, tt)]
    pieces += [functools.partial(norm, t0) for t0 in range(0, seg, norm_rows)]

    h = _rms(x_ref[...], g_ref[...]).astype(BF16)
    qkv_chunks = 3 * D_ATTN // COL_TILE
    uc_chunks = 2 * D_CONV // COL_TILE
    n_chunks = qkv_chunks + uc_chunks
    for c in range(n_chunks):
        if c < qkv_chunks:
            cols = slice(c * COL_TILE, (c + 1) * COL_TILE)
            qkv_ref[:, cols] = jnp.dot(h, wqkv_ref[:, cols], preferred_element_type=F32).astype(BF16)
        else:
            cols = slice((c - qkv_chunks) * COL_TILE, (c - qkv_chunks + 1) * COL_TILE)
            ring_ref[write_slot, :, cols] = jnp.dot(
                h, wc_ref[:, cols], preferred_element_type=F32).astype(BF16)
        for piece in pieces[len(pieces) * c // n_chunks:len(pieces) * (c + 1) // n_chunks]:
            piece()


def _in_proj_conv(x2, L, g, wqkv, wc, w, b, lng, lnb, og):
    T = x2.shape[0]
    seg = CONV_SEGMENT
    assert seg == TOKEN_TILE and L % seg == 0
    nt = T // seg
    n_lane_tiles = D_CONV // LANES
    w = w.reshape(CONV_WIDTH + 1, n_lane_tiles, LANES).transpose(1, 0, 2)
    b = b.reshape(n_lane_tiles, 1, LANES)
    vec = pl.BlockSpec((1, D_CONV), lambda i: (0, 0))
    tile = lambda i: (jnp.minimum(i, nt - 1), 0)
    return pl.pallas_call(
        functools.partial(_in_proj_conv_body, seg=seg, n_seg=L // seg),
        grid=(nt + 2,),
        in_specs=[
            pl.BlockSpec((seg, D_MODEL), tile),
            pl.BlockSpec((1, D_MODEL), lambda i: (0, 0)),
            pl.BlockSpec((D_MODEL, 3 * D_ATTN), lambda i: (0, 0)),
            pl.BlockSpec((D_MODEL, 2 * D_CONV), lambda i: (0, 0)),
            pl.BlockSpec((n_lane_tiles, CONV_WIDTH + 1, LANES), lambda i: (0, 0, 0)),
            pl.BlockSpec((n_lane_tiles, 1, LANES), lambda i: (0, 0, 0)),
            vec, vec, vec,
        ],
        out_specs=[
            pl.BlockSpec((seg, 3 * D_ATTN), tile),
            pl.BlockSpec((seg, D_CONV), lambda i: (jnp.maximum(i - 2, 0), 0)),
        ],
        out_shape=[
            jax.ShapeDtypeStruct((T, 3 * D_ATTN), BF16),
            jax.ShapeDtypeStruct((T, D_CONV), BF16),
        ],
        scratch_shapes=[
            pltpu.VMEM((UC_RING, seg, 2 * D_CONV), BF16),
            pltpu.VMEM((8, n_lane_tiles, seg + 2 * CONV_HALO, LANES), F32),
            pltpu.VMEM((n_lane_tiles, seg, LANES), F32),
        ],
        compiler_params=pltpu.CompilerParams(
            dimension_semantics=("arbitrary",), vmem_limit_bytes=VMEM_LIMIT),
        name="in_proj_conv",
    )(x2, g, wqkv, wc, w, b, lng, lnb, og)


GROUP_ROWS = 8
TILE_GROUPS = (2 * TOKEN_TILE + (GROUP_ROWS - 1) * N_EXPERTS + GROUP_ROWS - 1) // GROUP_ROWS
TILE_GROUPS = (TILE_GROUPS + 15) // 16 * 16
TILE_SLOTS = TILE_GROUPS * GROUP_ROWS
BLOCK_GROUPS = EXPERT_BLOCK // GROUP_ROWS
X_SLOTS = 3
ROUTER_ROWS = 128
EXPERT_ROW0 = 8
COL_P0, COL_P1, COL_W0, COL_W1 = range(4)


U32 = jnp.uint32
PACKED = D_MODEL // 2


def _pack_bf16_pairs(x):
    bits = pltpu.bitcast(x, U32)
    return bits[:, PACKED:] | (bits[:, :PACKED] >> 16)


def _unpack_bf16_pairs(words):
    lo = pltpu.bitcast(words << 16, F32)
    hi = pltpu.bitcast(words & jnp.uint32(0xFFFF0000), F32)
    return jnp.concatenate([lo, hi], axis=1).astype(BF16)


def _sorted_out_proj_body(x_ref, a_ref, c_ref, ag_ref, wa_ref, wc_ref, n2g_ref, wrt_ref, brt_ref,
                          x1_ref, hs_ref, route_ref, grp_ref, *, tm):
    an = _rms(a_ref[...].astype(F32), ag_ref[...]).astype(BF16)
    y = jnp.dot(an, wa_ref[...], preferred_element_type=F32)
    y = y + jnp.dot(c_ref[...], wc_ref[...], preferred_element_type=F32)
    x1 = x_ref[...] + y
    x1_ref[...] = x1
    h2 = _rms(x1, n2g_ref[...])

    h_hi = h2.astype(BF16)
    h_lo = (h2 - h_hi.astype(F32)).astype(BF16)
    nt = (((1,), (1,)), ((), ()))
    split = (lax.dot_general(wrt_ref[...], h_hi, nt, preferred_element_type=F32)
             + lax.dot_general(wrt_ref[...], h_lo, nt, preferred_element_type=F32))
    logits = split[:ROUTER_ROWS] + split[ROUTER_ROWS:] + brt_ref[...]

    gtop = logits[0:1]
    gsel = jnp.zeros((1, tm), F32)
    for g in range(1, N_GROUPS):
        cand = logits[g:g + 1]
        better = cand > gtop
        gsel = jnp.where(better, float(g), gsel)
        gtop = jnp.where(better, cand, gtop)
    denom = jnp.zeros((1, tm), F32)
    for g in range(N_GROUPS):
        denom = denom + jnp.exp(logits[g:g + 1] - gtop)
    p_g = 1.0 / denom

    el = logits[EXPERT_ROW0:EXPERT_ROW0 + EPG]
    for g in range(1, N_GROUPS):
        el = jnp.where(gsel == float(g), logits[EXPERT_ROW0 + g * EPG:EXPERT_ROW0 + (g + 1) * EPG], el)
    ninf = jnp.full((1, tm), -jnp.inf, F32)
    v0, v1 = ninf, ninf
    i0 = jnp.zeros((1, tm), F32)
    i1 = jnp.zeros((1, tm), F32)
    for j in range(EPG):
        cand = el[j:j + 1]
        gt0 = cand > v0
        gt1 = cand > v1
        v1 = jnp.where(gt0, v0, jnp.where(gt1, cand, v1))
        i1 = jnp.where(gt0, i0, jnp.where(gt1, float(j), i1))
        v0 = jnp.where(gt0, cand, v0)
        i0 = jnp.where(gt0, float(j), i0)
    t = jnp.exp(v1 - v0)
    w0 = p_g / (1.0 + t)
    w1 = p_g * t / (1.0 + t)
    e0 = gsel * EPG + i0
    e1 = gsel * EPG + i1

    expert = lax.broadcasted_iota(I32, (N_EXPERTS, tm), 0).astype(F32)
    hit0 = expert == e0
    hit1 = expert == e1
    onehot = jnp.where(hit0 | hit1, 1.0, 0.0).astype(BF16)
    ri = lax.broadcasted_iota(I32, (tm, tm), 0)
    ci = lax.broadcasted_iota(I32, (tm, tm), 1)
    upper = jnp.where(ri <= ci, 1.0, 0.0).astype(BF16)
    cum = jnp.dot(onehot, upper, preferred_element_type=F32)
    count = jnp.dot(onehot, jnp.ones((tm, LANES), BF16), preferred_element_type=F32)
    groups = jnp.floor((count + (GROUP_ROWS - 1)) * (1.0 / GROUP_ROWS))
    er = lax.broadcasted_iota(I32, (N_EXPERTS, N_EXPERTS), 0)
    ec = lax.broadcasted_iota(I32, (N_EXPERTS, N_EXPERTS), 1)
    below = jnp.where(ec < er, 1.0, 0.0).astype(BF16)
    run_start = jnp.dot(below, groups.astype(BF16), preferred_element_type=F32) * GROUP_ROWS
    run_start = jnp.concatenate([run_start] * (tm // LANES), axis=1)
    pos = run_start + cum - 1.0
    p0 = jnp.sum(jnp.where(hit0, pos, 0.0), axis=0, keepdims=True)
    p1 = jnp.sum(jnp.where(hit1, pos, 0.0), axis=0, keepdims=True)
    grp_ref[...] = groups

    slot = lax.broadcasted_iota(I32, (TILE_SLOTS, tm), 0).astype(F32)
    perm = jnp.where((slot == p0) | (slot == p1), 1.0, 0.0).astype(BF16)
    hs_ref[...] = _pack_bf16_pairs(jnp.dot(perm, h_hi, preferred_element_type=F32))

    rows = jnp.concatenate([p0, p1, w0, w1, jnp.zeros((ROUTER_ROWS - 4, tm), F32)], axis=0)
    route_ref[...] = rows.T


def _sorted_out_proj(x2, a2, c2, ag, wa, wc, n2g, wrt, brt):
    T = x2.shape[0]
    tm = TOKEN_TILE
    nt = T // tm
    full = lambda shape: pl.BlockSpec(shape, lambda i: (0,) * len(shape))
    return pl.pallas_call(
        functools.partial(_sorted_out_proj_body, tm=tm),
        grid=(nt,),
        in_specs=[
            pl.BlockSpec((tm, D_MODEL), lambda i: (i, 0)),
            pl.BlockSpec((tm, D_ATTN), lambda i: (i, 0)),
            pl.BlockSpec((tm, D_CONV), lambda i: (i, 0)),
            full((1, D_ATTN)),
            full((D_ATTN, D_MODEL)),
            full((D_CONV, D_MODEL)),
            full((1, D_MODEL)),
            full((2 * ROUTER_ROWS, D_MODEL)),
            full((ROUTER_ROWS, 1)),
        ],
        out_specs=[
            pl.BlockSpec((tm, D_MODEL), lambda i: (i, 0)),
            pl.BlockSpec((TILE_SLOTS, PACKED), lambda i: (i, 0)),
            pl.BlockSpec((tm, LANES), lambda i: (i, 0)),
            pl.BlockSpec((None, N_EXPERTS, LANES), lambda i: (i, 0, 0)),
        ],
        out_shape=[
            jax.ShapeDtypeStruct((T, D_MODEL), F32),
            jax.ShapeDtypeStruct((nt * TILE_SLOTS, PACKED), U32),
            jax.ShapeDtypeStruct((T, LANES), F32),
            jax.ShapeDtypeStruct((nt, N_EXPERTS, LANES), F32),
        ],
        compiler_params=pltpu.CompilerParams(
            dimension_semantics=("arbitrary",), vmem_limit_bytes=VMEM_LIMIT),
        name="out_proj_router",
    )(x2, a2, c2, ag, wa, wc, n2g, wrt, brt)


def _group_copies(src_ref, hs_hbm, ys_hbm, xbuf, ybuf, gsem, ssem, block, slot, trash0, to_trash):
    def group_index(i):
        return src_ref[block * BLOCK_GROUPS + i]

    def rows_of(i):
        start = i * GROUP_ROWS
        return pl.ds(start if isinstance(i, int) else pl.multiple_of(start, GROUP_ROWS), GROUP_ROWS)

    def gather(i):
        g = jnp.maximum(group_index(i), 0)
        return pltpu.make_async_copy(hs_hbm.at[g], xbuf.at[slot, rows_of(i)], gsem.at[slot])

    def scatter(i):
        g = group_index(i)
        g = jnp.where((g < 0) | to_trash, trash0 + slot * BLOCK_GROUPS + i, g)
        return pltpu.make_async_copy(ybuf.at[slot, rows_of(i)], ys_hbm.at[g], ssem.at[slot])

    return gather, scatter


def _sorted_expert_body(be_ref, nu_ref, src_ref, tg_ref, hs_hbm, wg_ref, wu_ref, wd_ref, ys_hbm,
                        xbuf, ybuf, zero_ref, gsem, ssem, zsem, *, n_tiles, trash0):
    del be_ref
    b = pl.program_id(0)
    n_used = nu_ref[0]
    slot = b % 2

    @pl.when(b == 0)
    def _():
        zero_ref[...] = jnp.zeros_like(zero_ref)

        def zero_copy(g):
            return pltpu.make_async_copy(zero_ref, ys_hbm.at[g], zsem)

        def over_unused(fn):
            def tile(c, carry):
                def group(g, carry2):
                    fn(c * TILE_GROUPS + g)
                    return carry2
                return lax.fori_loop(tg_ref[c], TILE_GROUPS, group, carry)
            lax.fori_loop(0, n_tiles, tile, 0)

            def trash(i, carry):
                fn(trash0 + i)
                return carry
            lax.fori_loop(0, 2 * BLOCK_GROUPS, trash, 0)

        over_unused(lambda g: zero_copy(g).start())
        over_unused(lambda g: zero_copy(g).wait())

    def for_groups(fn):
        def step(i, carry):
            fn(i)
            return carry
        lax.fori_loop(0, BLOCK_GROUPS, step, 0, unroll=8)

    def copies(block, slot_, to_trash=False):
        return _group_copies(src_ref, hs_hbm, ys_hbm, xbuf, ybuf, gsem, ssem, block, slot_, trash0,
                             to_trash)

    def wait_gathers(slot_):
        gather, _ = copies(0, slot_)
        for_groups(lambda i: gather(i).wait())

    def wait_scatters(slot_):
        _, scatter = copies(0, slot_)
        for_groups(lambda i: scatter(i).wait())

    xslot = b % X_SLOTS

    def last(block):
        return jnp.minimum(block, n_used - 1)

    @pl.when((b == 0) & (n_used > 0))
    def _():
        ybuf[...] = jnp.zeros_like(ybuf)
        for block in range(X_SLOTS - 1):
            gather, _ = copies(last(block), block)
            for_groups(lambda i: gather(i).start())
        _, scatter = copies(0, 0, to_trash=True)
        for_groups(lambda i: scatter(i).start())

    @pl.when(b < n_used)
    def _():
        wait_gathers(xslot)
        ahead = b + (X_SLOTS - 1)
        gather, _ = copies(last(ahead), ahead % X_SLOTS)
        _, scatter = copies(jnp.maximum(b - 1, 0), 1 - slot, to_trash=b == 0)
        for i in range(BLOCK_GROUPS):
            gather(i).start()
            scatter(i).start()
        x = _unpack_bf16_pairs(xbuf[xslot])
        g = jnp.dot(x, wg_ref[...], preferred_element_type=F32)
        u = jnp.dot(x, wu_ref[...], preferred_element_type=F32)
        hid = (g * jax.nn.sigmoid(g) * u).astype(BF16)
        y = jnp.dot(hid, wd_ref[...], preferred_element_type=F32)
        packed = _pack_bf16_pairs(y.astype(BF16).astype(F32))
        _, landed = copies(0, slot)
        for i in range(BLOCK_GROUPS):
            landed(i).wait()
        ybuf[slot] = packed

    @pl.when((b == n_used) & (n_used > 0))
    def _():
        for ahead in range(X_SLOTS - 1):
            wait_gathers((b + ahead) % X_SLOTS)
        wait_scatters(slot)
        _, scatter = copies(b - 1, 1 - slot)
        for_groups(lambda i: scatter(i).start())
        wait_scatters(1 - slot)


def _sorted_experts(block_e, n_used, src, tile_groups, hs3, wg, wu, wd):
    n_groups = hs3.shape[0]
    n_tiles = tile_groups.shape[0]
    n_blocks = block_e.shape[0]
    blk = EXPERT_BLOCK
    grid_spec = pltpu.PrefetchScalarGridSpec(
        num_scalar_prefetch=4,
        grid=(n_blocks,),
        in_specs=[
            pl.BlockSpec(memory_space=pl.ANY),
            pl.BlockSpec((None, D_MODEL, D_EXPERT), lambda b, be, nu, src, tg: (be[b], 0, 0)),
            pl.BlockSpec((None, D_MODEL, D_EXPERT), lambda b, be, nu, src, tg: (be[b], 0, 0)),
            pl.BlockSpec((None, D_EXPERT, D_MODEL), lambda b, be, nu, src, tg: (be[b], 0, 0)),
        ],
        out_specs=pl.BlockSpec(memory_space=pl.ANY),
        scratch_shapes=[
            pltpu.VMEM((X_SLOTS, blk, PACKED), U32),
            pltpu.VMEM((2, blk, PACKED), U32),
            pltpu.VMEM((GROUP_ROWS, PACKED), U32),
            pltpu.SemaphoreType.DMA((X_SLOTS,)),
            pltpu.SemaphoreType.DMA((2,)),
            pltpu.SemaphoreType.DMA(()),
        ],
    )
    return pl.pallas_call(
        functools.partial(_sorted_expert_body, n_tiles=n_tiles, trash0=n_groups),
        grid_spec=grid_spec,
        out_shape=jax.ShapeDtypeStruct((n_groups + 2 * BLOCK_GROUPS, GROUP_ROWS, PACKED), U32),
        compiler_params=pltpu.CompilerParams(
            dimension_semantics=("arbitrary",), vmem_limit_bytes=VMEM_LIMIT),
        name="moe_experts",
    )(block_e, n_used, src, tile_groups, hs3, wg, wu, wd)


def _sorted_combine_body(x1_ref, route_ref, fg_ref, ys_ref, o_ref, *, tm):
    ys = _unpack_bf16_pairs(ys_ref[...])
    route = route_ref[...]
    slot = lax.broadcasted_iota(I32, (tm, TILE_SLOTS), 1).astype(F32)
    gate = (jnp.where(slot == route[:, COL_P0:COL_P0 + 1], route[:, COL_W0:COL_W0 + 1], 0.0)
            + jnp.where(slot == route[:, COL_P1:COL_P1 + 1], route[:, COL_W1:COL_W1 + 1], 0.0))
    moe = jnp.dot(gate.astype(BF16), ys, preferred_element_type=F32)
    o_ref[...] = _rms(x1_ref[...] + moe, fg_ref[...])


def _sorted_combine(x1, route, fg, ys2):
    T = x1.shape[0]
    tm = TOKEN_TILE
    return pl.pallas_call(
        functools.partial(_sorted_combine_body, tm=tm),
        grid=(T // tm,),
        in_specs=[
            pl.BlockSpec((tm, D_MODEL), lambda i: (i, 0)),
            pl.BlockSpec((tm, LANES), lambda i: (i, 0)),
            pl.BlockSpec((1, D_MODEL), lambda i: (0, 0)),
            pl.BlockSpec((TILE_SLOTS, PACKED), lambda i: (i, 0)),
        ],
        out_specs=pl.BlockSpec((tm, D_MODEL), lambda i: (i, 0)),
        out_shape=jax.ShapeDtypeStruct((T, D_MODEL), F32),
        compiler_params=pltpu.CompilerParams(
            dimension_semantics=("arbitrary",), vmem_limit_bytes=VMEM_LIMIT),
        name="moe_combine",
    )(x1, route, fg, ys2)


def _moe_plan(groups):
    nt = groups.shape[0]
    run_start = jnp.cumsum(groups, axis=1) - groups
    cum_tiles = jnp.cumsum(groups, axis=0)
    total = cum_tiles[-1]
    blocks = (total + BLOCK_GROUPS - 1) // BLOCK_GROUPS
    blk_end = jnp.cumsum(blocks)
    blk_start = blk_end - blocks
    n_blocks = (2 * TOKEN_TILE + (GROUP_ROWS - 1) * N_EXPERTS) * nt // EXPERT_BLOCK + N_EXPERTS + 1
    b = jnp.arange(n_blocks, dtype=I32)
    block_e = jnp.minimum(jnp.sum(b[:, None] >= blk_end[None, :], axis=1), N_EXPERTS - 1).astype(I32)
    n_used = blk_end[-1:].astype(I32)

    sg = jnp.arange(n_blocks * BLOCK_GROUPS, dtype=I32)
    sb = sg // BLOCK_GROUPS
    sg_e = jnp.broadcast_to(block_e[:, None], (n_blocks, BLOCK_GROUPS)).reshape(-1)
    onehot_e = (sg_e[:, None] == jnp.arange(N_EXPERTS, dtype=I32)[None, :]).astype(F32)

    def pick(table):
        return jnp.dot(onehot_e, table.astype(F32), precision=lax.Precision.HIGHEST).astype(I32)

    j = sg - pick(blk_start[:, None])[:, 0] * BLOCK_GROUPS
    valid = (sb < n_used[0]) & (j < pick(total[:, None])[:, 0])
    cum_e = pick(cum_tiles.T)
    tile = jnp.minimum(jnp.sum(cum_e <= j[:, None], axis=1), nt - 1)
    tile_hot = tile[:, None] == jnp.arange(nt, dtype=I32)[None, :]
    before = jnp.sum(jnp.where(tile_hot, cum_e - pick(groups.T), 0), axis=1)
    start = jnp.sum(jnp.where(tile_hot, pick(run_start.T), 0), axis=1)
    src = jnp.where(valid, tile * TILE_GROUPS + start + (j - before), -1).astype(I32)
    tile_groups = jnp.sum(groups, axis=1).astype(I32)
    return block_e, n_used, src, tile_groups


def _trunk(x, p):
    B, L, _ = x.shape
    T = B * L
    x2 = x.reshape(T, D_MODEL)
    qkv, c = _in_proj_conv(x2, L, p["norm1_g"], p["wqkv"], p["wconv"], p["conv_w"], p["conv_b"],
                           p["conv_ln_g"], p["conv_ln_b"], p["conv_out_g"])
    a = _attention(qkv.reshape(B, L, 3 * D_ATTN), p["na_bias"])
    x1, hs, route, grp = _sorted_out_proj(
        x2, a.reshape(T, D_ATTN), c.reshape(T, D_CONV), p["attn_out_g"], p["wout_a"],
        p["wout_c"], p["norm2_g"], p["w_router_t"], p["b_router_t"])
    block_e, n_used, src, tile_groups = _moe_plan(grp[:, :, 0].astype(I32))
    hs3 = hs.reshape(hs.shape[0] // GROUP_ROWS, GROUP_ROWS, PACKED)
    ys3 = _sorted_experts(block_e, n_used, src, tile_groups, hs3, p["w_gate"], p["w_up"], p["w_down"])
    ys2 = ys3.reshape(ys3.shape[0] * GROUP_ROWS, PACKED)
    out = _sorted_combine(x1, route, p["final_g"], ys2)
    return out.reshape(B, L, D_MODEL)


def kernel(x_prompt, x_sample, norm1_g, w_in, rpb, attn_out_g, conv_w, conv_b, conv_ln_g,
           conv_ln_b, conv_out_g, w_out, norm2_g, w_group, b_group, w_expert, b_expert,
           w_e_gate, w_e_up, w_e_down, final_g):
    l = 0
    w_router_t = jnp.zeros((ROUTER_ROWS, D_MODEL), F32)
    w_router_t = w_router_t.at[:N_GROUPS].set(w_group[l].T)
    w_router_t = w_router_t.at[EXPERT_ROW0:EXPERT_ROW0 + N_EXPERTS].set(
        w_expert[l].transpose(0, 2, 1).reshape(N_EXPERTS, D_MODEL))
    w_router_hi = w_router_t.astype(BF16)
    w_router_lo = (w_router_t - w_router_hi.astype(F32)).astype(BF16)
    w_router_t = jnp.concatenate([w_router_hi, w_router_lo], axis=0)
    b_router_t = jnp.zeros((ROUTER_ROWS,), F32)
    b_router_t = b_router_t.at[:N_GROUPS].set(b_group[l])
    b_router_t = b_router_t.at[EXPERT_ROW0:EXPERT_ROW0 + N_EXPERTS].set(b_expert[l].reshape(N_EXPERTS))
    b_router_t = b_router_t.reshape(ROUTER_ROWS, 1)
    p = {
        "norm1_g": norm1_g[l].reshape(1, D_MODEL),
        "wqkv": w_in[l][:, :3 * D_ATTN].astype(BF16),
        "wconv": w_in[l][:, 3 * D_ATTN:].astype(BF16),
        "na_bias": _na_bias_table(rpb[l]),
        "attn_out_g": attn_out_g[l].reshape(1, D_ATTN),
        "conv_w": jnp.pad(conv_w[l], ((0, 1), (0, 0))),
        "conv_b": conv_b[l].reshape(1, D_CONV),
        "conv_ln_g": conv_ln_g[l].reshape(1, D_CONV),
        "conv_ln_b": conv_ln_b[l].reshape(1, D_CONV),
        "conv_out_g": conv_out_g[l].reshape(1, D_CONV),
        "wout_a": w_out[l][:D_ATTN].astype(BF16),
        "wout_c": w_out[l][D_ATTN:].astype(BF16),
        "norm2_g": norm2_g[l].reshape(1, D_MODEL),
        "w_router_t": w_router_t,
        "b_router_t": b_router_t,
        "w_gate": w_e_gate[l].astype(BF16),
        "w_up": w_e_up[l].astype(BF16),
        "w_down": w_e_down[l].astype(BF16),
        "final_g": final_g.reshape(1, D_MODEL),
    }
    return (_trunk(x_prompt, p), _trunk(x_sample, p))
```

```python
import functools

import jax
import jax.numpy as jnp
from jax import lax
from jax.experimental import pallas as pl
from jax.experimental.pallas import tpu as pltpu

F32 = jnp.float32
BF16 = jnp.bfloat16
I32 = jnp.int32

D_MODEL = 1024
GRID_W = 64
D_ATTN = 512
D_CONV = 512
HEAD_DIM = 64
N_HEADS = 8
NA_ROWS = 8
NA_COLS = 16
CONV_WIDTH = 31
N_GROUPS = 4
EPG = 8
N_EXPERTS = 32
D_EXPERT = 512
EPS = 1e-6

LANES = 128
HEAD_PAIRS = N_HEADS * HEAD_DIM // LANES
NA_KEYS = NA_ROWS * GRID_W
MASKED = -1e30
TOKEN_TILE = 512
EXPERT_BLOCK = 512
CONV_TILE = 32
CONV_SEGMENT = 512
CONV_HALO = 16
VMEM_LIMIT = 56 * 1024 * 1024


def _rms(x, g):
    return x * lax.rsqrt(jnp.mean(x * x, axis=-1, keepdims=True) + EPS) * g


def _na_bias_table(rpb):
    c = jnp.arange(GRID_W)
    col_start = jnp.clip(c - NA_COLS // 2, 0, GRID_W - NA_COLS)
    cp = jnp.arange(GRID_W)
    valid = (cp[None, :] >= col_start[:, None]) & (cp[None, :] < col_start[:, None] + NA_COLS)
    col_off = cp[None, :] - c[:, None] + (NA_COLS - 1)
    sel = (col_off[None] == jnp.arange(2 * NA_COLS - 1)[:, None, None]) & valid[None]
    a = jnp.einsum("hrd,dcx->hrcx", rpb, sel.astype(F32), precision=lax.Precision.HIGHEST)
    a = jnp.where(valid[None, None], a, MASKED)
    t = jnp.stack([a[:, NA_ROWS - 1 - p:2 * NA_ROWS - 1 - p] for p in range(NA_ROWS)], axis=1)
    t = t.transpose(0, 1, 3, 2, 4)
    t = t.reshape(HEAD_PAIRS, 2, NA_ROWS, GRID_W, NA_KEYS)
    return t.transpose(0, 2, 1, 3, 4).reshape(HEAD_PAIRS, NA_ROWS, LANES, NA_KEYS).astype(F32)


def _attn_body(q_ref, k_ref, v_ref, bias_ref, o_ref, s_ref, p_ref, l_ref, *, rows, rows_per_step):
    rb = pl.program_id(2)
    first_head = lax.broadcasted_iota(I32, (GRID_W, LANES), 1) < HEAD_DIM
    zero = jnp.zeros((GRID_W, LANES), BF16)

    def window(rr):
        r = rb * rows_per_step + rr
        r_start = jnp.clip(r - NA_ROWS // 2, 0, rows - NA_ROWS)
        return r - r_start, pl.multiple_of(r_start * GRID_W, GRID_W)

    for rr in range(rows_per_step):
        p, k0 = window(rr)
        q = q_ref[rr * GRID_W:(rr + 1) * GRID_W, :] * jnp.asarray(HEAD_DIM ** -0.5, BF16)
        qbd = jnp.concatenate([jnp.where(first_head, q, zero), jnp.where(first_head, zero, q)], axis=0)
        ks = k_ref[pl.ds(k0, NA_KEYS), :]
        s = lax.dot_general(qbd, ks, (((1,), (1,)), ((), ())), preferred_element_type=F32)
        s_ref[rr] = s + bias_ref[p]
    for rr in range(rows_per_step):
        s = s_ref[rr]
        e = jnp.exp(s - jnp.max(s, axis=-1, keepdims=True))
        l_ref[rr] = jnp.broadcast_to(1.0 / jnp.sum(e, axis=-1, keepdims=True), (LANES, LANES))
        p_ref[rr] = e.astype(BF16)
    for rr in range(rows_per_step):
        _, k0 = window(rr)
        vs = v_ref[pl.ds(k0, NA_KEYS), :]
        o = jnp.dot(p_ref[rr], vs, preferred_element_type=F32) * l_ref[rr]
        out = jnp.where(first_head, o[:GRID_W], o[GRID_W:])
        o_ref[rr * GRID_W:(rr + 1) * GRID_W, :] = out.astype(BF16)


def _attention(qkv, bias):
    B, L, _ = qkv.shape
    rows = L // GRID_W
    rows_per_step = 32
    tq = rows_per_step * GRID_W
    body = functools.partial(_attn_body, rows=rows, rows_per_step=rows_per_step)
    return pl.pallas_call(
        body,
        grid=(HEAD_PAIRS, B, L // tq),
        in_specs=[
            pl.BlockSpec((None, tq, LANES), lambda hp, b, rb: (b, rb, hp)),
            pl.BlockSpec((None, L, LANES), lambda hp, b, rb: (b, 0, HEAD_PAIRS + hp)),
            pl.BlockSpec((None, L, LANES), lambda hp, b, rb: (b, 0, 2 * HEAD_PAIRS + hp)),
            pl.BlockSpec((None, NA_ROWS, LANES, NA_KEYS), lambda hp, b, rb: (hp, 0, 0, 0)),
        ],
        out_specs=pl.BlockSpec((None, tq, LANES), lambda hp, b, rb: (b, rb, hp)),
        out_shape=jax.ShapeDtypeStruct((B, L, D_ATTN), BF16),
        scratch_shapes=[
            pltpu.VMEM((rows_per_step, LANES, NA_KEYS), F32),
            pltpu.VMEM((rows_per_step, LANES, NA_KEYS), BF16),
            pltpu.VMEM((rows_per_step, LANES, LANES), F32),
        ],
        compiler_params=pltpu.CompilerParams(
            dimension_semantics=("arbitrary", "arbitrary", "arbitrary"),
            vmem_limit_bytes=VMEM_LIMIT),
        name="na_attention",
    )(qkv, qkv, qkv, bias)


def _glu(uc_rows):
    a = uc_rows[:, 0:D_CONV].astype(F32)
    gt = uc_rows[:, D_CONV:2 * D_CONV].astype(F32)
    return a * jax.nn.sigmoid(gt)


UC_RING = 4
COL_TILE = 256


def _in_proj_conv_body(x_ref, g_ref, wqkv_ref, wc_ref, w_ref, b_ref, lng_ref, lnb_ref, og_ref,
                       qkv_ref, o_ref, ring_ref, zs_ref, cbuf_ref, *, seg, n_seg):
    i = pl.program_id(0)
    halo = CONV_HALO
    n_lane_tiles = D_CONV // LANES

    @pl.when(i == 0)
    def _():
        ring_ref[...] = jnp.zeros_like(ring_ref)

    sg = jnp.maximum(i - 2, 0) % n_seg
    prev_slot = (i + UC_RING - 3) % UC_RING
    cur_slot = (i + UC_RING - 2) % UC_RING
    next_slot = (i + UC_RING - 1) % UC_RING
    write_slot = i % UC_RING

    def put_z(u0, n, z):
        for j in range(n_lane_tiles):
            zs_ref[0, j, pl.ds(u0, n), :] = z[:, j * LANES:(j + 1) * LANES]

    @pl.when(sg > 0)
    def _():
        put_z(0, halo, _glu(ring_ref[prev_slot, seg - halo:seg, :]))

    @pl.when(sg == 0)
    def _():
        put_z(0, halo, jnp.zeros((halo, D_CONV), F32))

    @pl.when(sg < n_seg - 1)
    def _():
        put_z(halo + seg, halo, _glu(ring_ref[next_slot, 0:halo, :]))

    @pl.when(sg == n_seg - 1)
    def _():
        put_z(halo + seg, halo, jnp.zeros((halo, D_CONV), F32))

    glu_rows = shift_chunk = tt = norm_rows = CONV_TILE
    first = halo - CONV_WIDTH // 2

    def glu(u0):
        put_z(halo + u0, glu_rows, _glu(ring_ref[cur_slot, u0:u0 + glu_rows, :]))

    def shift_rows(j, u0, n):
        x = zs_ref[0, j, u0:u0 + n + 8, :]
        for s in range(1, 8):
            zs_ref[s, j, u0:u0 + n, :] = x[s:s + n]

    def taps(j, t0):
        acc = jnp.broadcast_to(b_ref[j], (tt, LANES))
        for k in range(CONV_WIDTH):
            off = k + first
            lo = t0 + 8 * (off // 8)
            acc = acc + w_ref[j, k:k + 1, :] * zs_ref[off % 8, j, lo:lo + tt, :]
        cbuf_ref[j, t0:t0 + tt, :] = acc

    def norm(t0):
        y = jnp.concatenate([cbuf_ref[j, t0:t0 + norm_rows, :] for j in range(n_lane_tiles)], axis=1)
        mu = jnp.mean(y, axis=-1, keepdims=True)
        yc = y - mu
        var = jnp.mean(yc * yc, axis=-1, keepdims=True)
        yn = yc * lax.rsqrt(var + EPS) * lng_ref[...] + lnb_ref[...]
        sw = yn * jax.nn.sigmoid(yn)
        o_ref[t0:t0 + norm_rows, :] = _rms(sw, og_ref[...]).astype(BF16)

    pieces = [functools.partial(glu, u0) for u0 in range(0, seg, glu_rows)]
    for j in range(n_lane_tiles):
        pieces += [functools.partial(shift_rows, j, u0, shift_chunk) for u0 in range(0, seg, shift_chunk)]
        pieces.append(functools.partial(shift_rows, j, seg, 2 * halo - 8))
    pieces += [functools.partial(taps, j, t0) for j in range(n_lane_tiles) for t0 in range(0, seg, tt)]
    pieces += [functools.partial(norm, t0) for t0 in range(0, seg, norm_rows)]

    h = _rms(x_ref[...], g_ref[...]).astype(BF16)
    qkv_chunks = 3 * D_ATTN // COL_TILE
    uc_chunks = 2 * D_CONV // COL_TILE
    n_chunks = qkv_chunks + uc_chunks
    for c in range(n_chunks):
        if c < qkv_chunks:
            cols = slice(c * COL_TILE, (c + 1) * COL_TILE)
            qkv_ref[:, cols] = jnp.dot(h, wqkv_ref[:, cols], preferred_element_type=F32).astype(BF16)
        else:
            cols = slice((c - qkv_chunks) * COL_TILE, (c - qkv_chunks + 1) * COL_TILE)
            ring_ref[write_slot, :, cols] = jnp.dot(
                h, wc_ref[:, cols], preferred_element_type=F32).astype(BF16)
        for piece in pieces[len(pieces) * c // n_chunks:len(pieces) * (c + 1) // n_chunks]:
            piece()


def _in_proj_conv(x2, L, g, wqkv, wc, w, b, lng, lnb, og):
    T = x2.shape[0]
    seg = CONV_SEGMENT
    assert seg == TOKEN_TILE and L % seg == 0
    nt = T // seg
    n_lane_tiles = D_CONV // LANES
    w = w.reshape(CONV_WIDTH + 1, n_lane_tiles, LANES).transpose(1, 0, 2)
    b = b.reshape(n_lane_tiles, 1, LANES)
    vec = pl.BlockSpec((1, D_CONV), lambda i: (0, 0))
    tile = lambda i: (jnp.minimum(i, nt - 1), 0)
    return pl.pallas_call(
        functools.partial(_in_proj_conv_body, seg=seg, n_seg=L // seg),
        grid=(nt + 2,),
        in_specs=[
            pl.BlockSpec((seg, D_MODEL), tile),
            pl.BlockSpec((1, D_MODEL), lambda i: (0, 0)),
            pl.BlockSpec((D_MODEL, 3 * D_ATTN), lambda i: (0, 0)),
            pl.BlockSpec((D_MODEL, 2 * D_CONV), lambda i: (0, 0)),
            pl.BlockSpec((n_lane_tiles, CONV_WIDTH + 1, LANES), lambda i: (0, 0, 0)),
            pl.BlockSpec((n_lane_tiles, 1, LANES), lambda i: (0, 0, 0)),
            vec, vec, vec,
        ],
        out_specs=[
            pl.BlockSpec((seg, 3 * D_ATTN), tile),
            pl.BlockSpec((seg, D_CONV), lambda i: (jnp.maximum(i - 2, 0), 0)),
        ],
        out_shape=[
            jax.ShapeDtypeStruct((T, 3 * D_ATTN), BF16),
            jax.ShapeDtypeStruct((T, D_CONV), BF16),
        ],
        scratch_shapes=[
            pltpu.VMEM((UC_RING, seg, 2 * D_CONV), BF16),
            pltpu.VMEM((8, n_lane_tiles, seg + 2 * CONV_HALO, LANES), F32),
            pltpu.VMEM((n_lane_tiles, seg, LANES), F32),
        ],
        compiler_params=pltpu.CompilerParams(
            dimension_semantics=("arbitrary",), vmem_limit_bytes=VMEM_LIMIT),
        name="in_proj_conv",
    )(x2, g, wqkv, wc, w, b, lng, lnb, og)


GROUP_ROWS = 8
TILE_GROUPS = (2 * TOKEN_TILE + (GROUP_ROWS - 1) * N_EXPERTS + GROUP_ROWS - 1) // GROUP_ROWS
TILE_GROUPS = (TILE_GROUPS + 15) // 16 * 16
TILE_SLOTS = TILE_GROUPS * GROUP_ROWS
BLOCK_GROUPS = EXPERT_BLOCK // GROUP_ROWS
X_SLOTS = 3
ROUTER_ROWS = 128
EXPERT_ROW0 = 8
COL_P0, COL_P1, COL_W0, COL_W1 = range(4)


U32 = jnp.uint32
PACKED = D_MODEL // 2


def _pack_bf16_pairs(x):
    bits = pltpu.bitcast(x, U32)
    return bits[:, PACKED:] | (bits[:, :PACKED] >> 16)


def _unpack_bf16_pairs(words):
    lo = pltpu.bitcast(words << 16, F32)
    hi = pltpu.bitcast(words & jnp.uint32(0xFFFF0000), F32)
    return jnp.concatenate([lo, hi], axis=1).astype(BF16)


def _sorted_out_proj_body(x_ref, a_ref, c_ref, ag_ref, wa_ref, wc_ref, n2g_ref, wrt_ref, brt_ref,
                          x1_ref, hs_ref, route_ref, grp_ref, *, tm):
    an = _rms(a_ref[...].astype(F32), ag_ref[...]).astype(BF16)
    y = jnp.dot(an, wa_ref[...], preferred_element_type=F32)
    y = y + jnp.dot(c_ref[...], wc_ref[...], preferred_element_type=F32)
    x1 = x_ref[...] + y
    x1_ref[...] = x1
    h2 = _rms(x1, n2g_ref[...])

    h_hi = h2.astype(BF16)
    nt = (((1,), (1,)), ((), ()))
    logits = lax.dot_general(wrt_ref[...], h_hi, nt, preferred_element_type=F32) + brt_ref[...]

    gtop = logits[0:1]
    gsel = jnp.zeros((1, tm), F32)
    for g in range(1, N_GROUPS):
        cand = logits[g:g + 1]
        better = cand > gtop
        gsel = jnp.where(better, float(g), gsel)
        gtop = jnp.where(better, cand, gtop)
    denom = jnp.zeros((1, tm), F32)
    for g in range(N_GROUPS):
        denom = denom + jnp.exp(logits[g:g + 1] - gtop)
    p_g = 1.0 / denom

    el = logits[EXPERT_ROW0:EXPERT_ROW0 + EPG]
    for g in range(1, N_GROUPS):
        el = jnp.where(gsel == float(g), logits[EXPERT_ROW0 + g * EPG:EXPERT_ROW0 + (g + 1) * EPG], el)
    ninf = jnp.full((1, tm), -jnp.inf, F32)
    v0, v1 = ninf, ninf
    i0 = jnp.zeros((1, tm), F32)
    i1 = jnp.zeros((1, tm), F32)
    for j in range(EPG):
        cand = el[j:j + 1]
        gt0 = cand > v0
        gt1 = cand > v1
        v1 = jnp.where(gt0, v0, jnp.where(gt1, cand, v1))
        i1 = jnp.where(gt0, i0, jnp.where(gt1, float(j), i1))
        v0 = jnp.where(gt0, cand, v0)
        i0 = jnp.where(gt0, float(j), i0)
    t = jnp.exp(v1 - v0)
    w0 = p_g / (1.0 + t)
    w1 = p_g * t / (1.0 + t)
    e0 = gsel * EPG + i0
    e1 = gsel * EPG + i1

    expert = lax.broadcasted_iota(I32, (N_EXPERTS, tm), 0).astype(F32)
    hit0 = expert == e0
    hit1 = expert == e1
    onehot = jnp.where(hit0 | hit1, 1.0, 0.0).astype(BF16)
    ri = lax.broadcasted_iota(I32, (tm, tm), 0)
    ci = lax.broadcasted_iota(I32, (tm, tm), 1)
    upper = jnp.where(ri <= ci, 1.0, 0.0).astype(BF16)
    cum = jnp.dot(onehot, upper, preferred_element_type=F32)
    count = jnp.dot(onehot, jnp.ones((tm, LANES), BF16), preferred_element_type=F32)
    groups = jnp.floor((count + (GROUP_ROWS - 1)) * (1.0 / GROUP_ROWS))
    er = lax.broadcasted_iota(I32, (N_EXPERTS, N_EXPERTS), 0)
    ec = lax.broadcasted_iota(I32, (N_EXPERTS, N_EXPERTS), 1)
    below = jnp.where(ec < er, 1.0, 0.0).astype(BF16)
    run_start = jnp.dot(below, groups.astype(BF16), preferred_element_type=F32) * GROUP_ROWS
    run_start = jnp.concatenate([run_start] * (tm // LANES), axis=1)
    pos = run_start + cum - 1.0
    p0 = jnp.sum(jnp.where(hit0, pos, 0.0), axis=0, keepdims=True)
    p1 = jnp.sum(jnp.where(hit1, pos, 0.0), axis=0, keepdims=True)
    grp_ref[...] = groups

    slot = lax.broadcasted_iota(I32, (TILE_SLOTS, tm), 0).astype(F32)
    perm = jnp.where((slot == p0) | (slot == p1), 1.0, 0.0).astype(BF16)
    hs_ref[...] = _pack_bf16_pairs(jnp.dot(perm, h_hi, preferred_element_type=F32))

    rows = jnp.concatenate([p0, p1, w0, w1, jnp.zeros((ROUTER_ROWS - 4, tm), F32)], axis=0)
    route_ref[...] = rows.T


def _sorted_out_proj(x2, a2, c2, ag, wa, wc, n2g, wrt, brt):
    T = x2.shape[0]
    tm = TOKEN_TILE
    nt = T // tm
    full = lambda shape: pl.BlockSpec(shape, lambda i: (0,) * len(shape))
    return pl.pallas_call(
        functools.partial(_sorted_out_proj_body, tm=tm),
        grid=(nt,),
        in_specs=[
            pl.BlockSpec((tm, D_MODEL), lambda i: (i, 0)),
            pl.BlockSpec((tm, D_ATTN), lambda i: (i, 0)),
            pl.BlockSpec((tm, D_CONV), lambda i: (i, 0)),
            full((1, D_ATTN)),
            full((D_ATTN, D_MODEL)),
            full((D_CONV, D_MODEL)),
            full((1, D_MODEL)),
            full((ROUTER_ROWS, D_MODEL)),
            full((ROUTER_ROWS, 1)),
        ],
        out_specs=[
            pl.BlockSpec((tm, D_MODEL), lambda i: (i, 0)),
            pl.BlockSpec((TILE_SLOTS, PACKED), lambda i: (i, 0)),
            pl.BlockSpec((tm, LANES), lambda i: (i, 0)),
            pl.BlockSpec((None, N_EXPERTS, LANES), lambda i: (i, 0, 0)),
        ],
        out_shape=[
            jax.ShapeDtypeStruct((T, D_MODEL), F32),
            jax.ShapeDtypeStruct((nt * TILE_SLOTS, PACKED), U32),
            jax.ShapeDtypeStruct((T, LANES), F32),
            jax.ShapeDtypeStruct((nt, N_EXPERTS, LANES), F32),
        ],
        compiler_params=pltpu.CompilerParams(
            dimension_semantics=("arbitrary",), vmem_limit_bytes=VMEM_LIMIT),
        name="out_proj_router",
    )(x2, a2, c2, ag, wa, wc, n2g, wrt, brt)


def _group_copies(src_ref, hs_hbm, ys_hbm, xbuf, ybuf, gsem, ssem, block, slot, trash0, to_trash):
    def group_index(i):
        return src_ref[block * BLOCK_GROUPS + i]

    def rows_of(i):
        start = i * GROUP_ROWS
        return pl.ds(start if isinstance(i, int) else pl.multiple_of(start, GROUP_ROWS), GROUP_ROWS)

    def gather(i):
        g = jnp.maximum(group_index(i), 0)
        return pltpu.make_async_copy(hs_hbm.at[g], xbuf.at[slot, rows_of(i)], gsem.at[slot])

    def scatter(i):
        g = group_index(i)
        g = jnp.where((g < 0) | to_trash, trash0 + slot * BLOCK_GROUPS + i, g)
        return pltpu.make_async_copy(ybuf.at[slot, rows_of(i)], ys_hbm.at[g], ssem.at[slot])

    return gather, scatter


def _sorted_expert_body(be_ref, nu_ref, src_ref, tg_ref, hs_hbm, wg_ref, wu_ref, wd_ref, ys_hbm,
                        xbuf, ybuf, zero_ref, gsem, ssem, zsem, *, n_tiles, trash0):
    del be_ref
    b = pl.program_id(0)
    n_used = nu_ref[0]
    slot = b % 2

    @pl.when(b == 0)
    def _():
        zero_ref[...] = jnp.zeros_like(zero_ref)

        def zero_copy(g):
            return pltpu.make_async_copy(zero_ref, ys_hbm.at[g], zsem)

        def over_unused(fn):
            def tile(c, carry):
                def group(g, carry2):
                    fn(c * TILE_GROUPS + g)
                    return carry2
                return lax.fori_loop(tg_ref[c], TILE_GROUPS, group, carry)
            lax.fori_loop(0, n_tiles, tile, 0)

            def trash(i, carry):
                fn(trash0 + i)
                return carry
            lax.fori_loop(0, 2 * BLOCK_GROUPS, trash, 0)

        over_unused(lambda g: zero_copy(g).start())
        over_unused(lambda g: zero_copy(g).wait())

    def for_groups(fn):
        def step(i, carry):
            fn(i)
            return carry
        lax.fori_loop(0, BLOCK_GROUPS, step, 0, unroll=8)

    def copies(block, slot_, to_trash=False):
        return _group_copies(src_ref, hs_hbm, ys_hbm, xbuf, ybuf, gsem, ssem, block, slot_, trash0,
                             to_trash)

    def wait_gathers(slot_):
        gather, _ = copies(0, slot_)
        for_groups(lambda i: gather(i).wait())

    def wait_scatters(slot_):
        _, scatter = copies(0, slot_)
        for_groups(lambda i: scatter(i).wait())

    xslot = b % X_SLOTS

    def last(block):
        return jnp.minimum(block, n_used - 1)

    @pl.when((b == 0) & (n_used > 0))
    def _():
        ybuf[...] = jnp.zeros_like(ybuf)
        for block in range(X_SLOTS - 1):
            gather, _ = copies(last(block), block)
            for_groups(lambda i: gather(i).start())
        _, scatter = copies(0, 0, to_trash=True)
        for_groups(lambda i: scatter(i).start())

    @pl.when(b < n_used)
    def _():
        wait_gathers(xslot)
        ahead = b + (X_SLOTS - 1)
        gather, _ = copies(last(ahead), ahead % X_SLOTS)
        _, scatter = copies(jnp.maximum(b - 1, 0), 1 - slot, to_trash=b == 0)
        for i in range(BLOCK_GROUPS):
            gather(i).start()
            scatter(i).start()
        x = _unpack_bf16_pairs(xbuf[xslot])
        g = jnp.dot(x, wg_ref[...], preferred_element_type=F32)
        u = jnp.dot(x, wu_ref[...], preferred_element_type=F32)
        hid = (g * jax.nn.sigmoid(g) * u).astype(BF16)
        y = jnp.dot(hid, wd_ref[...], preferred_element_type=F32)
        packed = _pack_bf16_pairs(y.astype(BF16).astype(F32))
        _, landed = copies(0, slot)
        for i in range(BLOCK_GROUPS):
            landed(i).wait()
        ybuf[slot] = packed

    @pl.when((b == n_used) & (n_used > 0))
    def _():
        for ahead in range(X_SLOTS - 1):
            wait_gathers((b + ahead) % X_SLOTS)
        wait_scatters(slot)
        _, scatter = copies(b - 1, 1 - slot)
        for_groups(lambda i: scatter(i).start())
        wait_scatters(1 - slot)


def _sorted_experts(block_e, n_used, src, tile_groups, hs3, wg, wu, wd):
    n_groups = hs3.shape[0]
    n_tiles = tile_groups.shape[0]
    n_blocks = block_e.shape[0]
    blk = EXPERT_BLOCK
    grid_spec = pltpu.PrefetchScalarGridSpec(
        num_scalar_prefetch=4,
        grid=(n_blocks,),
        in_specs=[
            pl.BlockSpec(memory_space=pl.ANY),
            pl.BlockSpec((None, D_MODEL, D_EXPERT), lambda b, be, nu, src, tg: (be[b], 0, 0)),
            pl.BlockSpec((None, D_MODEL, D_EXPERT), lambda b, be, nu, src, tg: (be[b], 0, 0)),
            pl.BlockSpec((None, D_EXPERT, D_MODEL), lambda b, be, nu, src, tg: (be[b], 0, 0)),
        ],
        out_specs=pl.BlockSpec(memory_space=pl.ANY),
        scratch_shapes=[
            pltpu.VMEM((X_SLOTS, blk, PACKED), U32),
            pltpu.VMEM((2, blk, PACKED), U32),
            pltpu.VMEM((GROUP_ROWS, PACKED), U32),
            pltpu.SemaphoreType.DMA((X_SLOTS,)),
            pltpu.SemaphoreType.DMA((2,)),
            pltpu.SemaphoreType.DMA(()),
        ],
    )
    return pl.pallas_call(
        functools.partial(_sorted_expert_body, n_tiles=n_tiles, trash0=n_groups),
        grid_spec=grid_spec,
        out_shape=jax.ShapeDtypeStruct((n_groups + 2 * BLOCK_GROUPS, GROUP_ROWS, PACKED), U32),
        compiler_params=pltpu.CompilerParams(
            dimension_semantics=("arbitrary",), vmem_limit_bytes=VMEM_LIMIT),
        name="moe_experts",
    )(block_e, n_used, src, tile_groups, hs3, wg, wu, wd)


def _sorted_combine_body(x1_ref, route_ref, fg_ref, ys_ref, o_ref, *, tm):
    ys = _unpack_bf16_pairs(ys_ref[...])
    route = route_ref[...]
    slot = lax.broadcasted_iota(I32, (tm, TILE_SLOTS), 1).astype(F32)
    gate = (jnp.where(slot == route[:, COL_P0:COL_P0 + 1], route[:, COL_W0:COL_W0 + 1], 0.0)
            + jnp.where(slot == route[:, COL_P1:COL_P1 + 1], route[:, COL_W1:COL_W1 + 1], 0.0))
    moe = jnp.dot(gate.astype(BF16), ys, preferred_element_type=F32)
    o_ref[...] = _rms(x1_ref[...] + moe, fg_ref[...])


def _sorted_combine(x1, route, fg, ys2):
    T = x1.shape[0]
    tm = TOKEN_TILE
    return pl.pallas_call(
        functools.partial(_sorted_combine_body, tm=tm),
        grid=(T // tm,),
        in_specs=[
            pl.BlockSpec((tm, D_MODEL), lambda i: (i, 0)),
            pl.BlockSpec((tm, LANES), lambda i: (i, 0)),
            pl.BlockSpec((1, D_MODEL), lambda i: (0, 0)),
            pl.BlockSpec((TILE_SLOTS, PACKED), lambda i: (i, 0)),
        ],
        out_specs=pl.BlockSpec((tm, D_MODEL), lambda i: (i, 0)),
        out_shape=jax.ShapeDtypeStruct((T, D_MODEL), F32),
        compiler_params=pltpu.CompilerParams(
            dimension_semantics=("arbitrary",), vmem_limit_bytes=VMEM_LIMIT),
        name="moe_combine",
    )(x1, route, fg, ys2)


def _moe_plan(groups):
    nt = groups.shape[0]
    run_start = jnp.cumsum(groups, axis=1) - groups
    cum_tiles = jnp.cumsum(groups, axis=0)
    total = cum_tiles[-1]
    blocks = (total + BLOCK_GROUPS - 1) // BLOCK_GROUPS
    blk_end = jnp.cumsum(blocks)
    blk_start = blk_end - blocks
    n_blocks = (2 * TOKEN_TILE + (GROUP_ROWS - 1) * N_EXPERTS) * nt // EXPERT_BLOCK + N_EXPERTS + 1
    b = jnp.arange(n_blocks, dtype=I32)
    block_e = jnp.minimum(jnp.sum(b[:, None] >= blk_end[None, :], axis=1), N_EXPERTS - 1).astype(I32)
    n_used = blk_end[-1:].astype(I32)

    sg = jnp.arange(n_blocks * BLOCK_GROUPS, dtype=I32)
    sb = sg // BLOCK_GROUPS
    sg_e = jnp.broadcast_to(block_e[:, None], (n_blocks, BLOCK_GROUPS)).reshape(-1)
    onehot_e = (sg_e[:, None] == jnp.arange(N_EXPERTS, dtype=I32)[None, :]).astype(F32)

    def pick(table):
        return jnp.dot(onehot_e, table.astype(F32), precision=lax.Precision.HIGHEST).astype(I32)

    j = sg - pick(blk_start[:, None])[:, 0] * BLOCK_GROUPS
    valid = (sb < n_used[0]) & (j < pick(total[:, None])[:, 0])
    cum_e = pick(cum_tiles.T)
    tile = jnp.minimum(jnp.sum(cum_e <= j[:, None], axis=1), nt - 1)
    tile_hot = tile[:, None] == jnp.arange(nt, dtype=I32)[None, :]
    before = jnp.sum(jnp.where(tile_hot, cum_e - pick(groups.T), 0), axis=1)
    start = jnp.sum(jnp.where(tile_hot, pick(run_start.T), 0), axis=1)
    src = jnp.where(valid, tile * TILE_GROUPS + start + (j - before), -1).astype(I32)
    tile_groups = jnp.sum(groups, axis=1).astype(I32)
    return block_e, n_used, src, tile_groups


def _trunk(x, p):
    B, L, _ = x.shape
    T = B * L
    x2 = x.reshape(T, D_MODEL)
    qkv, c = _in_proj_conv(x2, L, p["norm1_g"], p["wqkv"], p["wconv"], p["conv_w"], p["conv_b"],
                           p["conv_ln_g"], p["conv_ln_b"], p["conv_out_g"])
    a = _attention(qkv.reshape(B, L, 3 * D_ATTN), p["na_bias"])
    x1, hs, route, grp = _sorted_out_proj(
        x2, a.reshape(T, D_ATTN), c.reshape(T, D_CONV), p["attn_out_g"], p["wout_a"],
        p["wout_c"], p["norm2_g"], p["w_router_t"], p["b_router_t"])
    block_e, n_used, src, tile_groups = _moe_plan(grp[:, :, 0].astype(I32))
    hs3 = hs.reshape(hs.shape[0] // GROUP_ROWS, GROUP_ROWS, PACKED)
    ys3 = _sorted_experts(block_e, n_used, src, tile_groups, hs3, p["w_gate"], p["w_up"], p["w_down"])
    ys2 = ys3.reshape(ys3.shape[0] * GROUP_ROWS, PACKED)
    out = _sorted_combine(x1, route, p["final_g"], ys2)
    return out.reshape(B, L, D_MODEL)


def kernel(x_prompt, x_sample, norm1_g, w_in, rpb, attn_out_g, conv_w, conv_b, conv_ln_g,
           conv_ln_b, conv_out_g, w_out, norm2_g, w_group, b_group, w_expert, b_expert,
           w_e_gate, w_e_up, w_e_down, final_g):
    l = 0
    w_router_t = jnp.zeros((ROUTER_ROWS, D_MODEL), F32)
    w_router_t = w_router_t.at[:N_GROUPS].set(w_group[l].T)
    w_router_t = w_router_t.at[EXPERT_ROW0:EXPERT_ROW0 + N_EXPERTS].set(
        w_expert[l].transpose(0, 2, 1).reshape(N_EXPERTS, D_MODEL))
    w_router_t = w_router_t.astype(BF16)
    b_router_t = jnp.zeros((ROUTER_ROWS,), F32)
    b_router_t = b_router_t.at[:N_GROUPS].set(b_group[l])
    b_router_t = b_router_t.at[EXPERT_ROW0:EXPERT_ROW0 + N_EXPERTS].set(b_expert[l].reshape(N_EXPERTS))
    b_router_t = b_router_t.reshape(ROUTER_ROWS, 1)
    p = {
        "norm1_g": norm1_g[l].reshape(1, D_MODEL),
        "wqkv": w_in[l][:, :3 * D_ATTN].astype(BF16),
        "wconv": w_in[l][:, 3 * D_ATTN:].astype(BF16),
        "na_bias": _na_bias_table(rpb[l]),
        "attn_out_g": attn_out_g[l].reshape(1, D_ATTN),
        "conv_w": jnp.pad(conv_w[l], ((0, 1), (0, 0))),
        "conv_b": conv_b[l].reshape(1, D_CONV),
        "conv_ln_g": conv_ln_g[l].reshape(1, D_CONV),
        "conv_ln_b": conv_ln_b[l].reshape(1, D_CONV),
        "conv_out_g": conv_out_g[l].reshape(1, D_CONV),
        "wout_a": w_out[l][:D_ATTN].astype(BF16),
        "wout_c": w_out[l][D_ATTN:].astype(BF16),
        "norm2_g": norm2_g[l].reshape(1, D_MODEL),
        "w_router_t": w_router_t,
        "b_router_t": b_router_t,
        "w_gate": w_e_gate[l].astype(BF16),
        "w_up": w_e_up[l].astype(BF16),
        "w_down": w_e_down[l].astype(BF16),
        "final_g": final_g.reshape(1, D_MODEL),
    }
    return (_trunk(x_prompt, p), _trunk(x_sample, p))
```

```python
import functools

import jax
import jax.numpy as jnp
from jax import lax
from jax.experimental import pallas as pl
from jax.experimental.pallas import tpu as pltpu

F32 = jnp.float32
BF16 = jnp.bfloat16
I32 = jnp.int32

D_MODEL = 1024
GRID_W = 64
D_ATTN = 512
D_CONV = 512
HEAD_DIM = 64
N_HEADS = 8
NA_ROWS = 8
NA_COLS = 16
CONV_WIDTH = 31
N_GROUPS = 4
EPG = 8
N_EXPERTS = 32
D_EXPERT = 512
EPS = 1e-6

LANES = 128
HEAD_PAIRS = N_HEADS * HEAD_DIM // LANES
NA_KEYS = NA_ROWS * GRID_W
MASKED = -1e30
TOKEN_TILE = 512
EXPERT_BLOCK = 512
CONV_TILE = 32
CONV_SEGMENT = 512
CONV_HALO = 16
VMEM_LIMIT = 56 * 1024 * 1024


def _rms(x, g):
    return x * lax.rsqrt(jnp.mean(x * x, axis=-1, keepdims=True) + EPS) * g


def _na_bias_table(rpb):
    c = jnp.arange(GRID_W)
    col_start = jnp.clip(c - NA_COLS // 2, 0, GRID_W - NA_COLS)
    cp = jnp.arange(GRID_W)
    valid = (cp[None, :] >= col_start[:, None]) & (cp[None, :] < col_start[:, None] + NA_COLS)
    col_off = cp[None, :] - c[:, None] + (NA_COLS - 1)
    sel = (col_off[None] == jnp.arange(2 * NA_COLS - 1)[:, None, None]) & valid[None]
    a = jnp.einsum("hrd,dcx->hrcx", rpb, sel.astype(F32), precision=lax.Precision.HIGHEST)
    a = jnp.where(valid[None, None], a, MASKED)
    t = jnp.stack([a[:, NA_ROWS - 1 - p:2 * NA_ROWS - 1 - p] for p in range(NA_ROWS)], axis=1)
    t = t.transpose(0, 1, 3, 2, 4)
    t = t.reshape(HEAD_PAIRS, 2, NA_ROWS, GRID_W, NA_KEYS)
    return t.transpose(0, 2, 1, 3, 4).reshape(HEAD_PAIRS, NA_ROWS, LANES, NA_KEYS).astype(F32)


def _attn_body(q_ref, k_ref, v_ref, bias_ref, o_ref, s_ref, p_ref, l_ref, *, rows, rows_per_step):
    rb = pl.program_id(2)
    first_head = lax.broadcasted_iota(I32, (GRID_W, LANES), 1) < HEAD_DIM
    zero = jnp.zeros((GRID_W, LANES), BF16)

    def window(rr):
        r = rb * rows_per_step + rr
        r_start = jnp.clip(r - NA_ROWS // 2, 0, rows - NA_ROWS)
        return r - r_start, pl.multiple_of(r_start * GRID_W, GRID_W)

    for rr in range(rows_per_step):
        p, k0 = window(rr)
        q = q_ref[rr * GRID_W:(rr + 1) * GRID_W, :] * jnp.asarray(HEAD_DIM ** -0.5, BF16)
        qbd = jnp.concatenate([jnp.where(first_head, q, zero), jnp.where(first_head, zero, q)], axis=0)
        ks = k_ref[pl.ds(k0, NA_KEYS), :]
        s = lax.dot_general(qbd, ks, (((1,), (1,)), ((), ())), preferred_element_type=F32)
        s_ref[rr] = s + bias_ref[p]
    for rr in range(rows_per_step):
        s = s_ref[rr]
        e = jnp.exp(s - jnp.max(s, axis=-1, keepdims=True))
        l_ref[rr] = jnp.broadcast_to(1.0 / jnp.sum(e, axis=-1, keepdims=True), (LANES, LANES))
        p_ref[rr] = e.astype(BF16)
    for rr in range(rows_per_step):
        _, k0 = window(rr)
        vs = v_ref[pl.ds(k0, NA_KEYS), :]
        o = jnp.dot(p_ref[rr], vs, preferred_element_type=F32) * l_ref[rr]
        out = jnp.where(first_head, o[:GRID_W], o[GRID_W:])
        o_ref[rr * GRID_W:(rr + 1) * GRID_W, :] = out.astype(BF16)


def _attention(qkv, bias):
    B, L, _ = qkv.shape
    rows = L // GRID_W
    rows_per_step = min(64, rows)
    tq = rows_per_step * GRID_W
    body = functools.partial(_attn_body, rows=rows, rows_per_step=rows_per_step)
    return pl.pallas_call(
        body,
        grid=(HEAD_PAIRS, B, L // tq),
        in_specs=[
            pl.BlockSpec((None, tq, LANES), lambda hp, b, rb: (b, rb, hp)),
            pl.BlockSpec((None, L, LANES), lambda hp, b, rb: (b, 0, HEAD_PAIRS + hp)),
            pl.BlockSpec((None, L, LANES), lambda hp, b, rb: (b, 0, 2 * HEAD_PAIRS + hp)),
            pl.BlockSpec((None, NA_ROWS, LANES, NA_KEYS), lambda hp, b, rb: (hp, 0, 0, 0)),
        ],
        out_specs=pl.BlockSpec((None, tq, LANES), lambda hp, b, rb: (b, rb, hp)),
        out_shape=jax.ShapeDtypeStruct((B, L, D_ATTN), BF16),
        scratch_shapes=[
            pltpu.VMEM((rows_per_step, LANES, NA_KEYS), F32),
            pltpu.VMEM((rows_per_step, LANES, NA_KEYS), BF16),
            pltpu.VMEM((rows_per_step, LANES, LANES), F32),
        ],
        compiler_params=pltpu.CompilerParams(
            dimension_semantics=("arbitrary", "arbitrary", "arbitrary"),
            vmem_limit_bytes=VMEM_LIMIT),
        name="na_attention",
    )(qkv, qkv, qkv, bias)


def _glu(uc_rows):
    a = uc_rows[:, 0:D_CONV].astype(F32)
    gt = uc_rows[:, D_CONV:2 * D_CONV].astype(F32)
    return a * jax.nn.sigmoid(gt)


UC_RING = 4
COL_TILE = 256


def _in_proj_conv_body(x_ref, g_ref, wqkv_ref, wc_ref, w_ref, b_ref, lng_ref, lnb_ref, og_ref,
                       qkv_ref, o_ref, ring_ref, zs_ref, cbuf_ref, *, seg, n_seg):
    i = pl.program_id(0)
    halo = CONV_HALO
    n_lane_tiles = D_CONV // LANES

    @pl.when(i == 0)
    def _():
        ring_ref[...] = jnp.zeros_like(ring_ref)

    sg = jnp.maximum(i - 2, 0) % n_seg
    prev_slot = (i + UC_RING - 3) % UC_RING
    cur_slot = (i + UC_RING - 2) % UC_RING
    next_slot = (i + UC_RING - 1) % UC_RING
    write_slot = i % UC_RING

    def put_z(u0, n, z):
        for j in range(n_lane_tiles):
            zs_ref[0, j, pl.ds(u0, n), :] = z[:, j * LANES:(j + 1) * LANES]

    @pl.when(sg > 0)
    def _():
        put_z(0, halo, _glu(ring_ref[prev_slot, seg - halo:seg, :]))

    @pl.when(sg == 0)
    def _():
        put_z(0, halo, jnp.zeros((halo, D_CONV), F32))

    @pl.when(sg < n_seg - 1)
    def _():
        put_z(halo + seg, halo, _glu(ring_ref[next_slot, 0:halo, :]))

    @pl.when(sg == n_seg - 1)
    def _():
        put_z(halo + seg, halo, jnp.zeros((halo, D_CONV), F32))

    glu_rows = shift_chunk = tt = norm_rows = CONV_TILE
    first = halo - CONV_WIDTH // 2

    def glu(u0):
        put_z(halo + u0, glu_rows, _glu(ring_ref[cur_slot, u0:u0 + glu_rows, :]))

    def shift_rows(j, u0, n):
        x = zs_ref[0, j, u0:u0 + n + 8, :]
        for s in range(1, 8):
            zs_ref[s, j, u0:u0 + n, :] = x[s:s + n]

    def taps(j, t0):
        acc = jnp.broadcast_to(b_ref[j], (tt, LANES))
        for k in range(CONV_WIDTH):
            off = k + first
            lo = t0 + 8 * (off // 8)
            acc = acc + w_ref[j, k:k + 1, :] * zs_ref[off % 8, j, lo:lo + tt, :]
        cbuf_ref[j, t0:t0 + tt, :] = acc

    def norm(t0):
        y = jnp.concatenate([cbuf_ref[j, t0:t0 + norm_rows, :] for j in range(n_lane_tiles)], axis=1)
        mu = jnp.mean(y, axis=-1, keepdims=True)
        yc = y - mu
        var = jnp.mean(yc * yc, axis=-1, keepdims=True)
        yn = yc * lax.rsqrt(var + EPS) * lng_ref[...] + lnb_ref[...]
        sw = yn * jax.nn.sigmoid(yn)
        o_ref[t0:t0 + norm_rows, :] = _rms(sw, og_ref[...]).astype(BF16)

    pieces = [functools.partial(glu, u0) for u0 in range(0, seg, glu_rows)]
    for j in range(n_lane_tiles):
        pieces += [functools.partial(shift_rows, j, u0, shift_chunk) for u0 in range(0, seg, shift_chunk)]
        pieces.append(functools.partial(shift_rows, j, seg, 2 * halo - 8))
    pieces += [functools.partial(taps, j, t0) for j in range(n_lane_tiles) for t0 in range(0, seg, tt)]
    pieces += [functools.partial(norm, t0) for t0 in range(0, seg, norm_rows)]

    h = _rms(x_ref[...], g_ref[...]).astype(BF16)
    qkv_chunks = 3 * D_ATTN // COL_TILE
    uc_chunks = 2 * D_CONV // COL_TILE
    n_chunks = qkv_chunks + uc_chunks
    for c in range(n_chunks):
        if c < qkv_chunks:
            cols = slice(c * COL_TILE, (c + 1) * COL_TILE)
            qkv_ref[:, cols] = jnp.dot(h, wqkv_ref[:, cols], preferred_element_type=F32).astype(BF16)
        else:
            cols = slice((c - qkv_chunks) * COL_TILE, (c - qkv_chunks + 1) * COL_TILE)
            ring_ref[write_slot, :, cols] = jnp.dot(
                h, wc_ref[:, cols], preferred_element_type=F32).astype(BF16)
        for piece in pieces[len(pieces) * c // n_chunks:len(pieces) * (c + 1) // n_chunks]:
            piece()


def _in_proj_conv(x2, L, g, wqkv, wc, w, b, lng, lnb, og):
    T = x2.shape[0]
    seg = CONV_SEGMENT
    assert seg == TOKEN_TILE and L % seg == 0
    nt = T // seg
    n_lane_tiles = D_CONV // LANES
    w = w.reshape(CONV_WIDTH + 1, n_lane_tiles, LANES).transpose(1, 0, 2)
    b = b.reshape(n_lane_tiles, 1, LANES)
    vec = pl.BlockSpec((1, D_CONV), lambda i: (0, 0))
    tile = lambda i: (jnp.minimum(i, nt - 1), 0)
    return pl.pallas_call(
        functools.partial(_in_proj_conv_body, seg=seg, n_seg=L // seg),
        grid=(nt + 2,),
        in_specs=[
            pl.BlockSpec((seg, D_MODEL), tile),
            pl.BlockSpec((1, D_MODEL), lambda i: (0, 0)),
            pl.BlockSpec((D_MODEL, 3 * D_ATTN), lambda i: (0, 0)),
            pl.BlockSpec((D_MODEL, 2 * D_CONV), lambda i: (0, 0)),
            pl.BlockSpec((n_lane_tiles, CONV_WIDTH + 1, LANES), lambda i: (0, 0, 0)),
            pl.BlockSpec((n_lane_tiles, 1, LANES), lambda i: (0, 0, 0)),
            vec, vec, vec,
        ],
        out_specs=[
            pl.BlockSpec((seg, 3 * D_ATTN), tile),
            pl.BlockSpec((seg, D_CONV), lambda i: (jnp.maximum(i - 2, 0), 0)),
        ],
        out_shape=[
            jax.ShapeDtypeStruct((T, 3 * D_ATTN), BF16),
            jax.ShapeDtypeStruct((T, D_CONV), BF16),
        ],
        scratch_shapes=[
            pltpu.VMEM((UC_RING, seg, 2 * D_CONV), BF16),
            pltpu.VMEM((8, n_lane_tiles, seg + 2 * CONV_HALO, LANES), F32),
            pltpu.VMEM((n_lane_tiles, seg, LANES), F32),
        ],
        compiler_params=pltpu.CompilerParams(
            dimension_semantics=("arbitrary",), vmem_limit_bytes=VMEM_LIMIT),
        name="in_proj_conv",
    )(x2, g, wqkv, wc, w, b, lng, lnb, og)


GROUP_ROWS = 8
TILE_GROUPS = (2 * TOKEN_TILE + (GROUP_ROWS - 1) * N_EXPERTS + GROUP_ROWS - 1) // GROUP_ROWS
TILE_GROUPS = (TILE_GROUPS + 15) // 16 * 16
TILE_SLOTS = TILE_GROUPS * GROUP_ROWS
BLOCK_GROUPS = EXPERT_BLOCK // GROUP_ROWS
X_SLOTS = 3
ROUTER_ROWS = 128
EXPERT_ROW0 = 8
COL_P0, COL_P1, COL_W0, COL_W1 = range(4)


U32 = jnp.uint32
PACKED = D_MODEL // 2


def _pack_bf16_pairs(x):
    bits = pltpu.bitcast(x, U32)
    return bits[:, PACKED:] | (bits[:, :PACKED] >> 16)


def _unpack_bf16_pairs(words):
    lo = pltpu.bitcast(words << 16, F32)
    hi = pltpu.bitcast(words & jnp.uint32(0xFFFF0000), F32)
    return jnp.concatenate([lo, hi], axis=1).astype(BF16)


def _sorted_out_proj_body(x_ref, a_ref, c_ref, ag_ref, wa_ref, wc_ref, n2g_ref, wrt_ref, brt_ref,
                          x1_ref, hs_ref, route_ref, grp_ref, *, tm):
    an = _rms(a_ref[...].astype(F32), ag_ref[...]).astype(BF16)
    y = jnp.dot(an, wa_ref[...], preferred_element_type=F32)
    y = y + jnp.dot(c_ref[...], wc_ref[...], preferred_element_type=F32)
    x1 = x_ref[...] + y
    x1_ref[...] = x1
    h2 = _rms(x1, n2g_ref[...])

    h_hi = h2.astype(BF16)
    nt = (((1,), (1,)), ((), ()))
    logits = lax.dot_general(wrt_ref[...], h_hi, nt, preferred_element_type=F32) + brt_ref[...]

    gtop = logits[0:1]
    gsel = jnp.zeros((1, tm), F32)
    for g in range(1, N_GROUPS):
        cand = logits[g:g + 1]
        better = cand > gtop
        gsel = jnp.where(better, float(g), gsel)
        gtop = jnp.where(better, cand, gtop)
    denom = jnp.zeros((1, tm), F32)
    for g in range(N_GROUPS):
        denom = denom + jnp.exp(logits[g:g + 1] - gtop)
    p_g = 1.0 / denom

    el = logits[EXPERT_ROW0:EXPERT_ROW0 + EPG]
    for g in range(1, N_GROUPS):
        el = jnp.where(gsel == float(g), logits[EXPERT_ROW0 + g * EPG:EXPERT_ROW0 + (g + 1) * EPG], el)
    ninf = jnp.full((1, tm), -jnp.inf, F32)
    v0, v1 = ninf, ninf
    i0 = jnp.zeros((1, tm), F32)
    i1 = jnp.zeros((1, tm), F32)
    for j in range(EPG):
        cand = el[j:j + 1]
        gt0 = cand > v0
        gt1 = cand > v1
        v1 = jnp.where(gt0, v0, jnp.where(gt1, cand, v1))
        i1 = jnp.where(gt0, i0, jnp.where(gt1, float(j), i1))
        v0 = jnp.where(gt0, cand, v0)
        i0 = jnp.where(gt0, float(j), i0)
    t = jnp.exp(v1 - v0)
    w0 = p_g / (1.0 + t)
    w1 = p_g * t / (1.0 + t)
    e0 = gsel * EPG + i0
    e1 = gsel * EPG + i1

    expert = lax.broadcasted_iota(I32, (N_EXPERTS, tm), 0).astype(F32)
    hit0 = expert == e0
    hit1 = expert == e1
    onehot = jnp.where(hit0 | hit1, 1.0, 0.0).astype(BF16)
    ri = lax.broadcasted_iota(I32, (tm, tm), 0)
    ci = lax.broadcasted_iota(I32, (tm, tm), 1)
    upper = jnp.where(ri <= ci, 1.0, 0.0).astype(BF16)
    cum = jnp.dot(onehot, upper, preferred_element_type=F32)
    count = jnp.dot(onehot, jnp.ones((tm, LANES), BF16), preferred_element_type=F32)
    groups = jnp.floor((count + (GROUP_ROWS - 1)) * (1.0 / GROUP_ROWS))
    er = lax.broadcasted_iota(I32, (N_EXPERTS, N_EXPERTS), 0)
    ec = lax.broadcasted_iota(I32, (N_EXPERTS, N_EXPERTS), 1)
    below = jnp.where(ec < er, 1.0, 0.0).astype(BF16)
    run_start = jnp.dot(below, groups.astype(BF16), preferred_element_type=F32) * GROUP_ROWS
    run_start = jnp.concatenate([run_start] * (tm // LANES), axis=1)
    pos = run_start + cum - 1.0
    p0 = jnp.sum(jnp.where(hit0, pos, 0.0), axis=0, keepdims=True)
    p1 = jnp.sum(jnp.where(hit1, pos, 0.0), axis=0, keepdims=True)
    grp_ref[...] = groups

    slot = lax.broadcasted_iota(I32, (TILE_SLOTS, tm), 0).astype(F32)
    perm = jnp.where((slot == p0) | (slot == p1), 1.0, 0.0).astype(BF16)
    hs_ref[...] = _pack_bf16_pairs(jnp.dot(perm, h_hi, preferred_element_type=F32))

    rows = jnp.concatenate([p0, p1, w0, w1, jnp.zeros((ROUTER_ROWS - 4, tm), F32)], axis=0)
    route_ref[...] = rows.T


def _sorted_out_proj(x2, a2, c2, ag, wa, wc, n2g, wrt, brt):
    T = x2.shape[0]
    tm = TOKEN_TILE
    nt = T // tm
    full = lambda shape: pl.BlockSpec(shape, lambda i: (0,) * len(shape))
    return pl.pallas_call(
        functools.partial(_sorted_out_proj_body, tm=tm),
        grid=(nt,),
        in_specs=[
            pl.BlockSpec((tm, D_MODEL), lambda i: (i, 0)),
            pl.BlockSpec((tm, D_ATTN), lambda i: (i, 0)),
            pl.BlockSpec((tm, D_CONV), lambda i: (i, 0)),
            full((1, D_ATTN)),
            full((D_ATTN, D_MODEL)),
            full((D_CONV, D_MODEL)),
            full((1, D_MODEL)),
            full((ROUTER_ROWS, D_MODEL)),
            full((ROUTER_ROWS, 1)),
        ],
        out_specs=[
            pl.BlockSpec((tm, D_MODEL), lambda i: (i, 0)),
            pl.BlockSpec((TILE_SLOTS, PACKED), lambda i: (i, 0)),
            pl.BlockSpec((tm, LANES), lambda i: (i, 0)),
            pl.BlockSpec((None, N_EXPERTS, LANES), lambda i: (i, 0, 0)),
        ],
        out_shape=[
            jax.ShapeDtypeStruct((T, D_MODEL), F32),
            jax.ShapeDtypeStruct((nt * TILE_SLOTS, PACKED), U32),
            jax.ShapeDtypeStruct((T, LANES), F32),
            jax.ShapeDtypeStruct((nt, N_EXPERTS, LANES), F32),
        ],
        compiler_params=pltpu.CompilerParams(
            dimension_semantics=("arbitrary",), vmem_limit_bytes=VMEM_LIMIT),
        name="out_proj_router",
    )(x2, a2, c2, ag, wa, wc, n2g, wrt, brt)


def _group_copies(src_ref, hs_hbm, ys_hbm, xbuf, ybuf, gsem, ssem, block, slot, trash0, to_trash):
    def group_index(i):
        return src_ref[block * BLOCK_GROUPS + i]

    def rows_of(i):
        start = i * GROUP_ROWS
        return pl.ds(start if isinstance(i, int) else pl.multiple_of(start, GROUP_ROWS), GROUP_ROWS)

    def gather(i):
        g = jnp.maximum(group_index(i), 0)
        return pltpu.make_async_copy(hs_hbm.at[g], xbuf.at[slot, rows_of(i)], gsem.at[slot])

    def scatter(i):
        g = group_index(i)
        g = jnp.where((g < 0) | to_trash, trash0 + slot * BLOCK_GROUPS + i, g)
        return pltpu.make_async_copy(ybuf.at[slot, rows_of(i)], ys_hbm.at[g], ssem.at[slot])

    return gather, scatter


def _sorted_expert_body(be_ref, nu_ref, src_ref, tg_ref, hs_hbm, wg_ref, wu_ref, wd_ref, ys_hbm,
                        xbuf, ybuf, zero_ref, gsem, ssem, zsem, *, n_tiles, trash0):
    del be_ref
    b = pl.program_id(0)
    n_used = nu_ref[0]
    slot = b % 2

    @pl.when(b == 0)
    def _():
        zero_ref[...] = jnp.zeros_like(zero_ref)

        def zero_copy(g):
            return pltpu.make_async_copy(zero_ref, ys_hbm.at[g], zsem)

        def over_unused(fn):
            def tile(c, carry):
                def group(g, carry2):
                    fn(c * TILE_GROUPS + g)
                    return carry2
                return lax.fori_loop(tg_ref[c], TILE_GROUPS, group, carry)
            lax.fori_loop(0, n_tiles, tile, 0)

            def trash(i, carry):
                fn(trash0 + i)
                return carry
            lax.fori_loop(0, 2 * BLOCK_GROUPS, trash, 0)

        over_unused(lambda g: zero_copy(g).start())
        over_unused(lambda g: zero_copy(g).wait())

    def for_groups(fn):
        def step(i, carry):
            fn(i)
            return carry
        lax.fori_loop(0, BLOCK_GROUPS, step, 0, unroll=8)

    def copies(block, slot_, to_trash=False):
        return _group_copies(src_ref, hs_hbm, ys_hbm, xbuf, ybuf, gsem, ssem, block, slot_, trash0,
                             to_trash)

    def wait_gathers(slot_):
        gather, _ = copies(0, slot_)
        for_groups(lambda i: gather(i).wait())

    def wait_scatters(slot_):
        _, scatter = copies(0, slot_)
        for_groups(lambda i: scatter(i).wait())

    xslot = b % X_SLOTS

    def last(block):
        return jnp.minimum(block, n_used - 1)

    @pl.when((b == 0) & (n_used > 0))
    def _():
        ybuf[...] = jnp.zeros_like(ybuf)
        for block in range(X_SLOTS - 1):
            gather, _ = copies(last(block), block)
            for_groups(lambda i: gather(i).start())
        _, scatter = copies(0, 0, to_trash=True)
        for_groups(lambda i: scatter(i).start())

    @pl.when(b < n_used)
    def _():
        wait_gathers(xslot)
        ahead = b + (X_SLOTS - 1)
        gather, _ = copies(last(ahead), ahead % X_SLOTS)
        _, scatter = copies(jnp.maximum(b - 1, 0), 1 - slot, to_trash=b == 0)
        for i in range(BLOCK_GROUPS):
            gather(i).start()
            scatter(i).start()
        x = _unpack_bf16_pairs(xbuf[xslot])
        g = jnp.dot(x, wg_ref[...], preferred_element_type=F32)
        u = jnp.dot(x, wu_ref[...], preferred_element_type=F32)
        hid = (g * jax.nn.sigmoid(g) * u).astype(BF16)
        y = jnp.dot(hid, wd_ref[...], preferred_element_type=F32)
        packed = _pack_bf16_pairs(y.astype(BF16).astype(F32))
        _, landed = copies(0, slot)
        for i in range(BLOCK_GROUPS):
            landed(i).wait()
        ybuf[slot] = packed

    @pl.when((b == n_used) & (n_used > 0))
    def _():
        for ahead in range(X_SLOTS - 1):
            wait_gathers((b + ahead) % X_SLOTS)
        wait_scatters(slot)
        _, scatter = copies(b - 1, 1 - slot)
        for_groups(lambda i: scatter(i).start())
        wait_scatters(1 - slot)


def _sorted_experts(block_e, n_used, src, tile_groups, hs3, wg, wu, wd):
    n_groups = hs3.shape[0]
    n_tiles = tile_groups.shape[0]
    n_blocks = block_e.shape[0]
    blk = EXPERT_BLOCK
    grid_spec = pltpu.PrefetchScalarGridSpec(
        num_scalar_prefetch=4,
        grid=(n_blocks,),
        in_specs=[
            pl.BlockSpec(memory_space=pl.ANY),
            pl.BlockSpec((None, D_MODEL, D_EXPERT), lambda b, be, nu, src, tg: (be[b], 0, 0)),
            pl.BlockSpec((None, D_MODEL, D_EXPERT), lambda b, be, nu, src, tg: (be[b], 0, 0)),
            pl.BlockSpec((None, D_EXPERT, D_MODEL), lambda b, be, nu, src, tg: (be[b], 0, 0)),
        ],
        out_specs=pl.BlockSpec(memory_space=pl.ANY),
        scratch_shapes=[
            pltpu.VMEM((X_SLOTS, blk, PACKED), U32),
            pltpu.VMEM((2, blk, PACKED), U32),
            pltpu.VMEM((GROUP_ROWS, PACKED), U32),
            pltpu.SemaphoreType.DMA((X_SLOTS,)),
            pltpu.SemaphoreType.DMA((2,)),
            pltpu.SemaphoreType.DMA(()),
        ],
    )
    return pl.pallas_call(
        functools.partial(_sorted_expert_body, n_tiles=n_tiles, trash0=n_groups),
        grid_spec=grid_spec,
        out_shape=jax.ShapeDtypeStruct((n_groups + 2 * BLOCK_GROUPS, GROUP_ROWS, PACKED), U32),
        compiler_params=pltpu.CompilerParams(
            dimension_semantics=("arbitrary",), vmem_limit_bytes=VMEM_LIMIT),
        name="moe_experts",
    )(block_e, n_used, src, tile_groups, hs3, wg, wu, wd)


def _sorted_combine_body(x1_ref, route_ref, fg_ref, ys_ref, o_ref, *, tm):
    ys = _unpack_bf16_pairs(ys_ref[...])
    route = route_ref[...]
    slot = lax.broadcasted_iota(I32, (tm, TILE_SLOTS), 1).astype(F32)
    gate = (jnp.where(slot == route[:, COL_P0:COL_P0 + 1], route[:, COL_W0:COL_W0 + 1], 0.0)
            + jnp.where(slot == route[:, COL_P1:COL_P1 + 1], route[:, COL_W1:COL_W1 + 1], 0.0))
    moe = jnp.dot(gate.astype(BF16), ys, preferred_element_type=F32)
    o_ref[...] = _rms(x1_ref[...] + moe, fg_ref[...])


def _sorted_combine(x1, route, fg, ys2):
    T = x1.shape[0]
    tm = TOKEN_TILE
    return pl.pallas_call(
        functools.partial(_sorted_combine_body, tm=tm),
        grid=(T // tm,),
        in_specs=[
            pl.BlockSpec((tm, D_MODEL), lambda i: (i, 0)),
            pl.BlockSpec((tm, LANES), lambda i: (i, 0)),
            pl.BlockSpec((1, D_MODEL), lambda i: (0, 0)),
            pl.BlockSpec((TILE_SLOTS, PACKED), lambda i: (i, 0)),
        ],
        out_specs=pl.BlockSpec((tm, D_MODEL), lambda i: (i, 0)),
        out_shape=jax.ShapeDtypeStruct((T, D_MODEL), F32),
        compiler_params=pltpu.CompilerParams(
            dimension_semantics=("arbitrary",), vmem_limit_bytes=VMEM_LIMIT),
        name="moe_combine",
    )(x1, route, fg, ys2)


def _moe_plan(groups):
    nt = groups.shape[0]
    run_start = jnp.cumsum(groups, axis=1) - groups
    cum_tiles = jnp.cumsum(groups, axis=0)
    total = cum_tiles[-1]
    blocks = (total + BLOCK_GROUPS - 1) // BLOCK_GROUPS
    blk_end = jnp.cumsum(blocks)
    blk_start = blk_end - blocks
    n_blocks = (2 * TOKEN_TILE + (GROUP_ROWS - 1) * N_EXPERTS) * nt // EXPERT_BLOCK + N_EXPERTS + 1
    b = jnp.arange(n_blocks, dtype=I32)
    block_e = jnp.minimum(jnp.sum(b[:, None] >= blk_end[None, :], axis=1), N_EXPERTS - 1).astype(I32)
    n_used = blk_end[-1:].astype(I32)

    sg = jnp.arange(n_blocks * BLOCK_GROUPS, dtype=I32)
    sb = sg // BLOCK_GROUPS
    sg_e = jnp.broadcast_to(block_e[:, None], (n_blocks, BLOCK_GROUPS)).reshape(-1)
    onehot_e = (sg_e[:, None] == jnp.arange(N_EXPERTS, dtype=I32)[None, :]).astype(F32)

    def pick(table):
        return jnp.dot(onehot_e, table.astype(F32), precision=lax.Precision.HIGHEST).astype(I32)

    j = sg - pick(blk_start[:, None])[:, 0] * BLOCK_GROUPS
    valid = (sb < n_used[0]) & (j < pick(total[:, None])[:, 0])
    cum_e = pick(cum_tiles.T)
    tile = jnp.minimum(jnp.sum(cum_e <= j[:, None], axis=1), nt - 1)
    tile_hot = tile[:, None] == jnp.arange(nt, dtype=I32)[None, :]
    before = jnp.sum(jnp.where(tile_hot, cum_e - pick(groups.T), 0), axis=1)
    start = jnp.sum(jnp.where(tile_hot, pick(run_start.T), 0), axis=1)
    src = jnp.where(valid, tile * TILE_GROUPS + start + (j - before), -1).astype(I32)
    tile_groups = jnp.sum(groups, axis=1).astype(I32)
    return block_e, n_used, src, tile_groups


def _trunk(x, p):
    B, L, _ = x.shape
    T = B * L
    x2 = x.reshape(T, D_MODEL)
    qkv, c = _in_proj_conv(x2, L, p["norm1_g"], p["wqkv"], p["wconv"], p["conv_w"], p["conv_b"],
                           p["conv_ln_g"], p["conv_ln_b"], p["conv_out_g"])
    a = _attention(qkv.reshape(B, L, 3 * D_ATTN), p["na_bias"])
    x1, hs, route, grp = _sorted_out_proj(
        x2, a.reshape(T, D_ATTN), c.reshape(T, D_CONV), p["attn_out_g"], p["wout_a"],
        p["wout_c"], p["norm2_g"], p["w_router_t"], p["b_router_t"])
    block_e, n_used, src, tile_groups = _moe_plan(grp[:, :, 0].astype(I32))
    hs3 = hs.reshape(hs.shape[0] // GROUP_ROWS, GROUP_ROWS, PACKED)
    ys3 = _sorted_experts(block_e, n_used, src, tile_groups, hs3, p["w_gate"], p["w_up"], p["w_down"])
    ys2 = ys3.reshape(ys3.shape[0] * GROUP_ROWS, PACKED)
    out = _sorted_combine(x1, route, p["final_g"], ys2)
    return out.reshape(B, L, D_MODEL)


def kernel(x_prompt, x_sample, norm1_g, w_in, rpb, attn_out_g, conv_w, conv_b, conv_ln_g,
           conv_ln_b, conv_out_g, w_out, norm2_g, w_group, b_group, w_expert, b_expert,
           w_e_gate, w_e_up, w_e_down, final_g):
    l = 0
    w_router_t = jnp.zeros((ROUTER_ROWS, D_MODEL), F32)
    w_router_t = w_router_t.at[:N_GROUPS].set(w_group[l].T)
    w_router_t = w_router_t.at[EXPERT_ROW0:EXPERT_ROW0 + N_EXPERTS].set(
        w_expert[l].transpose(0, 2, 1).reshape(N_EXPERTS, D_MODEL))
    w_router_t = w_router_t.astype(BF16)
    b_router_t = jnp.zeros((ROUTER_ROWS,), F32)
    b_router_t = b_router_t.at[:N_GROUPS].set(b_group[l])
    b_router_t = b_router_t.at[EXPERT_ROW0:EXPERT_ROW0 + N_EXPERTS].set(b_expert[l].reshape(N_EXPERTS))
    b_router_t = b_router_t.reshape(ROUTER_ROWS, 1)
    p = {
        "norm1_g": norm1_g[l].reshape(1, D_MODEL),
        "wqkv": w_in[l][:, :3 * D_ATTN].astype(BF16),
        "wconv": w_in[l][:, 3 * D_ATTN:].astype(BF16),
        "na_bias": _na_bias_table(rpb[l]),
        "attn_out_g": attn_out_g[l].reshape(1, D_ATTN),
        "conv_w": jnp.pad(conv_w[l], ((0, 1), (0, 0))),
        "conv_b": conv_b[l].reshape(1, D_CONV),
        "conv_ln_g": conv_ln_g[l].reshape(1, D_CONV),
        "conv_ln_b": conv_ln_b[l].reshape(1, D_CONV),
        "conv_out_g": conv_out_g[l].reshape(1, D_CONV),
        "wout_a": w_out[l][:D_ATTN].astype(BF16),
        "wout_c": w_out[l][D_ATTN:].astype(BF16),
        "norm2_g": norm2_g[l].reshape(1, D_MODEL),
        "w_router_t": w_router_t,
        "b_router_t": b_router_t,
        "w_gate": w_e_gate[l].astype(BF16),
        "w_up": w_e_up[l].astype(BF16),
        "w_down": w_e_down[l].astype(BF16),
        "final_g": final_g.reshape(1, D_MODEL),
    }
    return (_trunk(x_prompt, p), _trunk(x_sample, p))
```
